```python
import jax, jax.numpy as jnp
from jax import lax
import numpy as np

D_MODEL = 1024
BATCH = 8
SEQ = 2048
DEPTH = 4
DEC_BATCH = 128
DEC_SEQ = 8
PAST_LEN = 8192
PAGE_SIZE = 128

N_MIXERS = 3
N_POOL_LAYERS = (DEPTH + 2) // 3
N_CONV_LAYERS = (DEPTH + 1) // 3
N_MLA_LAYERS = DEPTH // 3
POOL_WINDOWS = (2, 4, 8, 16)
POOL_GROUPS = len(POOL_WINDOWS)
POOL_GROUP_DIM = D_MODEL // POOL_GROUPS
POOL_STATE = max(POOL_WINDOWS) - 1
CONV_WIDTH = 31
N_HEADS = D_MODEL // 128
QK_NOPE = 128
QK_ROPE = 64
V_DIM = 128
KV_LORA = D_MODEL // 4
Q_LORA = 3 * D_MODEL // 8
ROPE_THETA = 10000.0
SOFTMAX_SCALE = (QK_NOPE + QK_ROPE) ** -0.5
Q_BLOCK = 128
N_GROUPS = 4
EXPERTS_PER_GROUP = 8
N_EXPERTS = N_GROUPS * EXPERTS_PER_GROUP
TOP_K = 2
EXPERT_FF = D_MODEL // 2
MOE_BLOCK = 128
PLE_DIM = 256
RMS_EPS = 1e-6
LN_EPS = 1e-5
NEG_INF = -1e30

kernel_name = 'hybrid_pool_conv_mla_hmoe_step'


def _rmsnorm(x, g):
    xf = x.astype(jnp.float32)
    y = xf * lax.rsqrt(jnp.mean(xf * xf, axis=-1, keepdims=True) + RMS_EPS)
    return (y * g.astype(jnp.float32)).astype(x.dtype)


def _layernorm(x, g, b):
    xf = x.astype(jnp.float32)
    mu = jnp.mean(xf, axis=-1, keepdims=True)
    var = jnp.mean(jnp.square(xf - mu), axis=-1, keepdims=True)
    y = (xf - mu) * lax.rsqrt(var + LN_EPS) * g.astype(jnp.float32) + b.astype(jnp.float32)
    return y.astype(x.dtype)


def _rope(x, pos):
    half = QK_ROPE // 2
    inv = 1.0 / (ROPE_THETA ** (jnp.arange(half, dtype=jnp.float32) * (2.0 / QK_ROPE)))
    ang = pos.astype(jnp.float32)[:, None] * inv[None, :]
    shape = (1, pos.shape[0]) + (1,) * (x.ndim - 3) + (half,)
    cos, sin = jnp.cos(ang).reshape(shape), jnp.sin(ang).reshape(shape)
    xf = x.astype(jnp.float32)
    x1, x2 = xf[..., :half], xf[..., half:]
    return jnp.concatenate([x1 * cos - x2 * sin, x1 * sin + x2 * cos], axis=-1).astype(x.dtype)


def _pool_mixer(hn, pos, prev, w_in, w_grp, scale):
    b, t, d = hn.shape
    u = hn @ w_in
    if prev is None:
        prev = jnp.zeros((b, POOL_STATE, d), u.dtype)
    ext = jnp.concatenate([prev.astype(u.dtype), u], axis=1)
    csum = jnp.cumsum(ext.astype(jnp.float32), axis=1)
    csum = jnp.pad(csum, ((0, 0), (1, 0), (0, 0)))
    u32 = u.astype(jnp.float32)
    end = POOL_STATE + 1
    parts = []
    for g, w in enumerate(POOL_WINDOWS):
        c0, c1 = g * POOL_GROUP_DIM, (g + 1) * POOL_GROUP_DIM
        win_sum = csum[:, end:end + t, c0:c1] - csum[:, end - w:end - w + t, c0:c1]
        count = jnp.minimum(pos + 1, w).astype(jnp.float32)[None, :, None]
        parts.append(win_sum / count - u32[:, :, c0:c1])
    z = jnp.stack(parts, axis=2).astype(u.dtype)
    y = jnp.einsum('btgi,gio->btgo', z, w_grp).reshape(b, t, d)
    return y * scale, ext[:, -POOL_STATE:]


def _conv_module(hn, prev, w_pw1, b_pw1, w_dw, b_dw, ln_g, ln_b, w_pw2, b_pw2):
    b, t, d = hn.shape
    a = hn @ w_pw1 + b_pw1
    g = a[..., :d] * jax.nn.sigmoid(a[..., d:])
    if prev is None:
        prev = jnp.zeros((b, CONV_WIDTH - 1, d), g.dtype)
    ext = jnp.concatenate([prev.astype(g.dtype), g], axis=1)
    c = lax.conv_general_dilated(ext, w_dw[:, None, :].astype(ext.dtype), (1,), 'VALID',
                                 dimension_numbers=('NWC', 'WIO', 'NWC'),
                                 feature_group_count=d) + b_dw
    n = _layernorm(c, ln_g, ln_b)
    return jax.nn.silu(n) @ w_pw2 + b_pw2, ext[:, -(CONV_WIDTH - 1):]


def _latent_attention(q_lat, q_pe, key_sets):
    scores = []
    for c, kr, m in key_sets:
        s = (jnp.einsum('bqhc,bsc->bhqs', q_lat, c).astype(jnp.float32)
             + jnp.einsum('bqhr,bsr->bhqs', q_pe, kr).astype(jnp.float32)) * SOFTMAX_SCALE
        if m is not None:
            s = jnp.where(m, s, NEG_INF)
        scores.append(s)
    prob = jax.nn.softmax(jnp.concatenate(scores, axis=-1), axis=-1)
    out, off = 0.0, 0
    for (c, _, _), s in zip(key_sets, scores):
        n = s.shape[-1]
        out = out + jnp.einsum('bhqs,bsc->bqhc', prob[..., off:off + n].astype(c.dtype), c)
        off += n
    return out


def _mla(hn, pos, past, w_dqkv, q_norm, kv_norm, w_uq, w_uk, w_uv, w_o):
    b, t, _ = hn.shape
    dq = hn @ w_dqkv
    cq = _rmsnorm(dq[..., :Q_LORA], q_norm)
    ckv = _rmsnorm(dq[..., Q_LORA:Q_LORA + KV_LORA], kv_norm)
    kpe = _rope(dq[..., Q_LORA + KV_LORA:], pos)
    q = (cq @ w_uq).reshape(b, t, N_HEADS, QK_NOPE + QK_ROPE)
    q_lat = jnp.einsum('bthn,hcn->bthc', q[..., :QK_NOPE], w_uk)
    q_pe = _rope(q[..., QK_NOPE:], pos)
    if past is None:
        nb = t // Q_BLOCK
        split = lambda a: jnp.moveaxis(a.reshape((b, nb, Q_BLOCK) + a.shape[2:]), 1, 0)
        def attend_block(args):
            ql, qp, qpos = args
            mask = pos[None, :] <= qpos[:, None]
            return _latent_attention(ql, qp, [(ckv, kpe, mask)])
        o = lax.map(attend_block, (split(q_lat), split(q_pe), pos.reshape(nb, Q_BLOCK)))
        o = jnp.moveaxis(o, 0, 1).reshape(b, t, N_HEADS, KV_LORA)
    else:
        cache_latent, cache_krope, page_table, j = past
        past_c = cache_latent[j, page_table].reshape(b, -1, KV_LORA)
        past_kr = cache_krope[j, page_table].reshape(b, -1, QK_ROPE)
        causal = pos[None, :] <= pos[:, None]
        o = _latent_attention(q_lat, q_pe, [(past_c, past_kr, None), (ckv, kpe, causal)])
    v = jnp.einsum('bthc,hcv->bthv', o, w_uv).reshape(b, t, N_HEADS * V_DIM)
    return v @ w_o, ckv, kpe


def _moe(hn, w_group, b_group, w_expert, b_expert, w1, w3, w2):
    b, t, d = hn.shape
    x = hn.reshape(-1, d)
    n = x.shape[0]
    gp = jax.nn.softmax((x @ w_group + b_group).astype(jnp.float32), axis=-1)
    gsel = jnp.argmax(gp, axis=-1)
    gprob = jnp.max(gp, axis=-1, keepdims=True)
    el = (x @ w_expert + b_expert).astype(jnp.float32).reshape(n, N_GROUPS, EXPERTS_PER_GROUP)
    ep = jax.nn.softmax(el[jnp.arange(n), gsel], axis=-1)
    topv, topi = lax.top_k(ep, TOP_K)
    gates = gprob * topv / jnp.sum(topv, axis=-1, keepdims=True)
    eid = (gsel[:, None] * EXPERTS_PER_GROUP + topi).astype(jnp.int32)
    a_tot = n * TOP_K
    flat_e = eid.reshape(-1)
    flat_tok = jnp.repeat(jnp.arange(n, dtype=jnp.int32), TOP_K)
    flat_gate = gates.reshape(-1)
    order = jnp.argsort(flat_e, stable=True)
    se = flat_e[order]
    counts = jnp.bincount(flat_e, length=N_EXPERTS)
    starts = jnp.cumsum(counts) - counts
    padded = (counts + MOE_BLOCK - 1) // MOE_BLOCK * MOE_BLOCK
    pad_ends = jnp.cumsum(padded)
    pad_starts = pad_ends - padded
    dest = pad_starts[se] + (jnp.arange(a_tot, dtype=jnp.int32) - starts[se])
    n_rows = (a_tot + N_EXPERTS * (MOE_BLOCK - 1) + MOE_BLOCK - 1) // MOE_BLOCK * MOE_BLOCK
    n_blk = n_rows // MOE_BLOCK
    row_tok = jnp.full((n_rows,), n, jnp.int32).at[dest].set(flat_tok[order])
    row_gate = jnp.zeros((n_rows,), jnp.float32).at[dest].set(flat_gate[order])
    blk_e = jnp.minimum(jnp.searchsorted(pad_ends, jnp.arange(n_blk) * MOE_BLOCK, side='right'),
                        N_EXPERTS - 1)
    x_pad = jnp.concatenate([x, jnp.zeros((1, d), x.dtype)], axis=0)
    xb = x_pad[row_tok].reshape(n_blk, MOE_BLOCK, d)
    def expert_block(args):
        xs, e = args
        return (jax.nn.silu(xs @ w1[e]) * (xs @ w3[e])) @ w2[e]
    yb = lax.map(expert_block, (xb, blk_e)).reshape(n_rows, d)
    y = jnp.zeros((n + 1, d), x.dtype).at[row_tok].add(yb * row_gate[:, None].astype(x.dtype))
    return y[:n].reshape(b, t, d)


def _trunk(x, p, pos, pool_state, conv_state, mla_cache, prm):
    h = x
    new_pool, new_conv, new_lat, new_kr = [], [], [], []
    for i in range(DEPTH):
        kind, j = i % N_MIXERS, i // N_MIXERS
        hn = _rmsnorm(h, prm['norm_mix'][i])
        if kind == 0:
            prev = None if pool_state is None else pool_state[j]
            out, st = _pool_mixer(hn, pos, prev, prm['pool_w_in'][j], prm['pool_w_grp'][j],
                                  prm['pool_scale'][j])
            new_pool.append(st)
        elif kind == 1:
            prev = None if conv_state is None else conv_state[j]
            out, st = _conv_module(hn, prev, prm['conv_w_pw1'][j], prm['conv_b_pw1'][j],
                                   prm['conv_w_dw'][j], prm['conv_b_dw'][j], prm['conv_ln_g'][j],
                                   prm['conv_ln_b'][j], prm['conv_w_pw2'][j], prm['conv_b_pw2'][j])
            new_conv.append(st)
        else:
            past = None if mla_cache is None else (mla_cache[0], mla_cache[1], mla_cache[2], j)
            out, ckv, kpe = _mla(hn, pos, past, prm['mla_w_dqkv'][j], prm['mla_q_norm'][j],
                                 prm['mla_kv_norm'][j], prm['mla_w_uq'][j], prm['mla_w_uk'][j],
                                 prm['mla_w_uv'][j], prm['mla_w_o'][j])
            new_lat.append(ckv)
            new_kr.append(kpe)
        h = h + out
        h = h + _moe(_rmsnorm(h, prm['norm_ffn'][i]), prm['moe_w_group'][i], prm['moe_b_group'][i],
                     prm['moe_w_expert'][i], prm['moe_b_expert'][i], prm['moe_w1'][i],
                     prm['moe_w3'][i], prm['moe_w2'][i])
        gate = jax.nn.sigmoid(_rmsnorm(h, prm['norm_ple'][i]) @ prm['ple_w_gate'][i])
        h = h + gate * (p[i].astype(h.dtype) @ prm['ple_w_proj'][i])
    return _rmsnorm(h, prm['norm_final']), new_pool, new_conv, new_lat, new_kr


def setup_inputs(seed: int = 0) -> dict:
    key = jax.random.key(seed)
    keys = iter(jax.random.split(key, 64))
    f32 = jnp.float32
    def nrm(shape, scale):
        return jax.random.normal(next(keys), shape, f32) * scale
    def gain(shape):
        return 1.0 + 0.05 * jax.random.normal(next(keys), shape, f32)
    D = D_MODEL
    n_pages = PAST_LEN // PAGE_SIZE
    n_phys = (DEC_BATCH * n_pages * 5) // 4
    perm = jax.random.permutation(next(keys), n_phys)
    page_table = perm[:DEC_BATCH * n_pages].reshape(DEC_BATCH, n_pages).astype(jnp.int32)
    return {
        'x_prompt': nrm((BATCH, SEQ, D), 1.0),
        'x_sample': nrm((DEC_BATCH, DEC_SEQ, D), 1.0),
        'state_pool': nrm((N_POOL_LAYERS, DEC_BATCH, POOL_STATE, D), 1.0),
        'state_conv': nrm((N_CONV_LAYERS, DEC_BATCH, CONV_WIDTH - 1, D), 0.5),
        'cache_latent': nrm((N_MLA_LAYERS, n_phys, PAGE_SIZE, KV_LORA), 1.0),
        'cache_krope': nrm((N_MLA_LAYERS, n_phys, PAGE_SIZE, QK_ROPE), 1.0),
        'page_table': page_table,
        'p_prompt': nrm((DEPTH, BATCH, SEQ, PLE_DIM), 1.0),
        'p_sample': nrm((DEPTH, DEC_BATCH, DEC_SEQ, PLE_DIM), 1.0),
        'norm_mix': gain((DEPTH, D)),
        'norm_ffn': gain((DEPTH, D)),
        'norm_ple': gain((DEPTH, D)),
        'norm_final': gain((D,)),
        'pool_w_in': nrm((N_POOL_LAYERS, D, D), D ** -0.5),
        'pool_w_grp': nrm((N_POOL_LAYERS, POOL_GROUPS, POOL_GROUP_DIM, POOL_GROUP_DIM), POOL_GROUP_DIM ** -0.5),
        'pool_scale': gain((N_POOL_LAYERS, D)),
        'conv_w_pw1': nrm((N_CONV_LAYERS, D, 2 * D), D ** -0.5),
        'conv_b_pw1': nrm((N_CONV_LAYERS, 2 * D), 0.02),
        'conv_w_dw': nrm((N_CONV_LAYERS, CONV_WIDTH, D), CONV_WIDTH ** -0.5),
        'conv_b_dw': nrm((N_CONV_LAYERS, D), 0.02),
        'conv_ln_g': gain((N_CONV_LAYERS, D)),
        'conv_ln_b': nrm((N_CONV_LAYERS, D), 0.02),
        'conv_w_pw2': nrm((N_CONV_LAYERS, D, D), D ** -0.5),
        'conv_b_pw2': nrm((N_CONV_LAYERS, D), 0.02),
        'mla_w_dqkv': nrm((N_MLA_LAYERS, D, Q_LORA + KV_LORA + QK_ROPE), D ** -0.5),
        'mla_q_norm': gain((N_MLA_LAYERS, Q_LORA)),
        'mla_kv_norm': gain((N_MLA_LAYERS, KV_LORA)),
        'mla_w_uq': nrm((N_MLA_LAYERS, Q_LORA, N_HEADS * (QK_NOPE + QK_ROPE)), Q_LORA ** -0.5),
        'mla_w_uk': nrm((N_MLA_LAYERS, N_HEADS, KV_LORA, QK_NOPE), KV_LORA ** -0.5),
        'mla_w_uv': nrm((N_MLA_LAYERS, N_HEADS, KV_LORA, V_DIM), KV_LORA ** -0.5),
        'mla_w_o': nrm((N_MLA_LAYERS, N_HEADS * V_DIM, D), (N_HEADS * V_DIM) ** -0.5),
        'moe_w_group': nrm((DEPTH, D, N_GROUPS), D ** -0.5),
        'moe_b_group': nrm((DEPTH, N_GROUPS), 0.01),
        'moe_w_expert': nrm((DEPTH, D, N_EXPERTS), D ** -0.5),
        'moe_b_expert': nrm((DEPTH, N_EXPERTS), 0.01),
        'moe_w1': nrm((DEPTH, N_EXPERTS, D, EXPERT_FF), D ** -0.5),
        'moe_w3': nrm((DEPTH, N_EXPERTS, D, EXPERT_FF), D ** -0.5),
        'moe_w2': nrm((DEPTH, N_EXPERTS, EXPERT_FF, D), EXPERT_FF ** -0.5),
        'ple_w_gate': nrm((DEPTH, D, D), D ** -0.5),
        'ple_w_proj': nrm((DEPTH, PLE_DIM, D), PLE_DIM ** -0.5),
    }


def reference(x_prompt, x_sample, state_pool, state_conv, cache_latent, cache_krope, page_table,
              p_prompt, p_sample, norm_mix, norm_ffn, norm_ple, norm_final,
              pool_w_in, pool_w_grp, pool_scale,
              conv_w_pw1, conv_b_pw1, conv_w_dw, conv_b_dw, conv_ln_g, conv_ln_b, conv_w_pw2, conv_b_pw2,
              mla_w_dqkv, mla_q_norm, mla_kv_norm, mla_w_uq, mla_w_uk, mla_w_uv, mla_w_o,
              moe_w_group, moe_b_group, moe_w_expert, moe_b_expert, moe_w1, moe_w3, moe_w2,
              ple_w_gate, ple_w_proj):
    prm = {
        'norm_mix': norm_mix, 'norm_ffn': norm_ffn, 'norm_ple': norm_ple, 'norm_final': norm_final,
        'pool_w_in': pool_w_in, 'pool_w_grp': pool_w_grp, 'pool_scale': pool_scale,
        'conv_w_pw1': conv_w_pw1, 'conv_b_pw1': conv_b_pw1, 'conv_w_dw': conv_w_dw,
        'conv_b_dw': conv_b_dw, 'conv_ln_g': conv_ln_g, 'conv_ln_b': conv_ln_b,
        'conv_w_pw2': conv_w_pw2, 'conv_b_pw2': conv_b_pw2,
        'mla_w_dqkv': mla_w_dqkv, 'mla_q_norm': mla_q_norm, 'mla_kv_norm': mla_kv_norm,
        'mla_w_uq': mla_w_uq, 'mla_w_uk': mla_w_uk, 'mla_w_uv': mla_w_uv, 'mla_w_o': mla_w_o,
        'moe_w_group': moe_w_group, 'moe_b_group': moe_b_group, 'moe_w_expert': moe_w_expert,
        'moe_b_expert': moe_b_expert, 'moe_w1': moe_w1, 'moe_w3': moe_w3, 'moe_w2': moe_w2,
        'ple_w_gate': ple_w_gate, 'ple_w_proj': ple_w_proj,
    }
    past_len = page_table.shape[1] * cache_latent.shape[2]
    pos_prompt = jnp.arange(x_prompt.shape[1], dtype=jnp.int32)
    pos_sample = past_len + jnp.arange(x_sample.shape[1], dtype=jnp.int32)
    y_prompt, pool_p, conv_p, lat_p, kr_p = _trunk(x_prompt, p_prompt, pos_prompt, None, None, None, prm)
    y_sample, pool_s, conv_s, lat_s, kr_s = _trunk(x_sample, p_sample, pos_sample, state_pool, state_conv,
                                                   (cache_latent, cache_krope, page_table), prm)
    new_pool_prompt = jnp.stack(pool_p)
    new_pool_sample = jnp.stack(pool_s)
    new_conv_prompt = jnp.stack(conv_p)
    new_conv_sample = jnp.stack(conv_s)
    new_latent_prompt = jnp.stack(lat_p)
    new_krope_prompt = jnp.stack(kr_p)
    new_latent_sample = jnp.stack(lat_s)
    new_krope_sample = jnp.stack(kr_s)
    return (y_prompt, y_sample, new_pool_prompt, new_pool_sample, new_conv_prompt, new_conv_sample,
            new_latent_prompt, new_krope_prompt, new_latent_sample, new_krope_sample)
```

```python
import functools

import jax
import jax.numpy as jnp
from jax import lax
from jax.experimental import pallas as pl
from jax.experimental.pallas import tpu as pltpu

F32 = jnp.float32
BF16 = jnp.bfloat16

POOL_WINDOWS = (2, 4, 8, 16)
TOP_K = 2
ROPE_THETA = 10000.0
RMS_EPS = 1e-6
LN_EPS = 1e-5
NEG_INF = -1e30
N_MIXERS = 3

V7X_LANES = 128
V7X_SUBLANES = 8
V7X_VMEM_LIMIT_BYTES = 48 * 1024 * 1024

TOKEN_TILE = 512
MOE_TILE = 256
CONV_TILE = 256
ATTN_Q_STEPS = 128
ATTN_K_TILE = 512
DECODE_PAGES = 8
SAMPLE_SEQ_BLOCK = 32


def _cparams(sem):
    return pltpu.CompilerParams(dimension_semantics=sem, vmem_limit_bytes=V7X_VMEM_LIMIT_BYTES)


def _rms(x, g):
    ms = jnp.mean(x * x, axis=-1, keepdims=True)
    return x * lax.rsqrt(ms + RMS_EPS) * g


def _dot(a, b):
    return jnp.dot(a, b, preferred_element_type=F32)


def _dot_t(a, b):
    return lax.dot_general(a, b, (((1,), (1,)), ((), ())), preferred_element_type=F32)


def _sigmoid(x):
    return 1.0 / (1.0 + jnp.exp(-x))


def _prep_kernel(xp_ref, xs_ref, g_ref, h_ref, hn_ref, *, npt):
    i = pl.program_id(0)
    x = jnp.where(i < npt, xp_ref[...], xs_ref[...])
    h_ref[...] = x
    hn_ref[...] = _rms(x, g_ref[...]).astype(BF16)


def _prep(xp, xs, g, tm):
    np_, d = xp.shape
    ns = xs.shape[0]
    npt, nst = np_ // tm, ns // tm
    n = np_ + ns
    return pl.pallas_call(
        functools.partial(_prep_kernel, npt=npt),
        grid=(npt + nst,),
        in_specs=[
            pl.BlockSpec((tm, d), lambda i: (jnp.minimum(i, npt - 1), 0)),
            pl.BlockSpec((tm, d), lambda i: (jnp.maximum(i - npt, 0), 0)),
            pl.BlockSpec((1, d), lambda i: (0, 0)),
        ],
        out_specs=[pl.BlockSpec((tm, d), lambda i: (i, 0)),
                   pl.BlockSpec((tm, d), lambda i: (i, 0))],
        out_shape=[jax.ShapeDtypeStruct((n, d), F32), jax.ShapeDtypeStruct((n, d), BF16)],
        compiler_params=_cparams(("arbitrary",)),
        name="prep",
    )(xp, xs, g)


def _pool_prompt_kernel(hn_ref, win_ref, wg_ref, sc_ref, mo_ref, st_ref, ext_ref, *, tm, carry, state):
    j = pl.program_id(1)
    nj = pl.num_programs(1)
    d = hn_ref.shape[1]
    gd = d // len(POOL_WINDOWS)
    u = _dot(hn_ref[...], win_ref[...])

    @pl.when(j == 0)
    def _():
        ext_ref[0:carry, :] = jnp.zeros((carry, d), F32)

    @pl.when(j > 0)
    def _():
        ext_ref[0:carry, :] = ext_ref[tm:tm + carry, :]

    ext_ref[carry:carry + tm, :] = u
    pos = j * tm + lax.broadcasted_iota(jnp.int32, (tm, 1), 0)
    for g, w in enumerate(POOL_WINDOWS):
        c0, c1 = g * gd, (g + 1) * gd
        ug = u[:, c0:c1]
        s = ug
        for k in range(1, w):
            s = s + ext_ref[carry - k:carry - k + tm, c0:c1]
        cnt = jnp.minimum(pos + 1, w).astype(F32)
        z = s / cnt - ug
        y = _dot(z.astype(BF16), wg_ref[g])
        mo_ref[:, c0:c1] = y * sc_ref[:, c0:c1]

    @pl.when(j == nj - 1)
    def _():
        st_ref[0] = ext_ref[carry + tm - state:carry + tm, :]


def _pool_prompt(hn, w_in, w_grp, scale, b, t, tm):
    d = hn.shape[1]
    state = max(POOL_WINDOWS) - 1
    carry = 16
    ntj = t // tm
    return pl.pallas_call(
        functools.partial(_pool_prompt_kernel, tm=tm, carry=carry, state=state),
        grid=(b, ntj),
        in_specs=[
            pl.BlockSpec((tm, d), lambda bi, j: (bi * ntj + j, 0)),
            pl.BlockSpec((d, d), lambda bi, j: (0, 0)),
            pl.BlockSpec(w_grp.shape, lambda bi, j: (0, 0, 0)),
            pl.BlockSpec((1, d), lambda bi, j: (0, 0)),
        ],
        out_specs=[pl.BlockSpec((tm, d), lambda bi, j: (bi * ntj + j, 0)),
                   pl.BlockSpec((1, state, d), lambda bi, j: (bi, 0, 0))],
        out_shape=[jax.ShapeDtypeStruct((b * t, d), F32), jax.ShapeDtypeStruct((b, state, d), F32)],
        scratch_shapes=[pltpu.VMEM((carry + tm, d), F32)],
        compiler_params=_cparams(("arbitrary", "arbitrary")),
        name="pool_prompt",
    )(hn, w_in, w_grp, scale)


def _pool_sample_kernel(hn_ref, prev_ref, win_ref, wg_ref, sc_ref, mo_ref, st_ref, u_ref, z_ref,
                        *, bs, t, state, pos0):
    d = hn_ref.shape[1]
    gd = d // len(POOL_WINDOWS)
    u = _dot(hn_ref[...], win_ref[...])
    u_ref[...] = u.reshape(bs, t, d)

    def row(i, c0, c1):
        if i >= 0:
            return u_ref[:, i, c0:c1]
        return prev_ref[:, state + i, c0:c1]

    for g, w in enumerate(POOL_WINDOWS):
        c0, c1 = g * gd, (g + 1) * gd
        for ti in range(t):
            s = row(ti, c0, c1)
            for k in range(1, w):
                s = s + row(ti - k, c0, c1)
            cnt = float(min(pos0 + ti + 1, w))
            z_ref[:, ti, c0:c1] = s / cnt - row(ti, c0, c1)
    z = z_ref[...].reshape(bs * t, d)
    for g in range(len(POOL_WINDOWS)):
        c0, c1 = g * gd, (g + 1) * gd
        y = _dot(z[:, c0:c1].astype(BF16), wg_ref[g])
        mo_ref[:, c0:c1] = y * sc_ref[:, c0:c1]
    if t >= state:
        st_ref[...] = u_ref[:, t - state:t, :]
    else:
        st_ref[:, 0:state - t, :] = prev_ref[:, t:state, :]
        st_ref[:, state - t:state, :] = u_ref[...]


def _pool_sample(hn, row0, prev, w_in, w_grp, scale, b, t, pos0):
    d = hn.shape[1]
    state = prev.shape[1]
    bs = SAMPLE_SEQ_BLOCK
    blk0 = row0 // (bs * t)
    return pl.pallas_call(
        functools.partial(_pool_sample_kernel, bs=bs, t=t, state=state, pos0=pos0),
        grid=(b // bs,),
        in_specs=[
            pl.BlockSpec((bs * t, d), lambda i: (blk0 + i, 0)),
            pl.BlockSpec((bs, state, d), lambda i: (i, 0, 0)),
            pl.BlockSpec((d, d), lambda i: (0, 0)),
            pl.BlockSpec(w_grp.shape, lambda i: (0, 0, 0)),
            pl.BlockSpec((1, d), lambda i: (0, 0)),
        ],
        out_specs=[pl.BlockSpec((bs * t, d), lambda i: (i, 0)),
                   pl.BlockSpec((bs, state, d), lambda i: (i, 0, 0))],
        out_shape=[jax.ShapeDtypeStruct((b * t, d), F32), jax.ShapeDtypeStruct((b, state, d), F32)],
        scratch_shapes=[pltpu.VMEM((bs, t, d), F32), pltpu.VMEM((bs, t, d), F32)],
        compiler_params=_cparams(("arbitrary",)),
        name="pool_sample",
    )(hn, prev, w_in, w_grp, scale)


def _layernorm_silu(c, g, b):
    mu = jnp.mean(c, axis=-1, keepdims=True)
    xc = c - mu
    var = jnp.mean(xc * xc, axis=-1, keepdims=True)
    n = xc * lax.rsqrt(var + LN_EPS) * g + b
    return n * _sigmoid(n)


def _conv_prompt_kernel(hn_ref, w1_ref, b1_ref, wdw_ref, bdw_ref, lg_ref, lb_ref, w2_ref, b2_ref,
                        mo_ref, st_ref, ext_ref, c_ref, *, tm, carry, width, rc):
    j = pl.program_id(1)
    nj = pl.num_programs(1)
    d = hn_ref.shape[1]
    a = _dot(hn_ref[...], w1_ref[...]) + b1_ref[...]
    glu = a[:, :d] * _sigmoid(a[:, d:])

    length = carry + tm

    @pl.when(j == 0)
    def _():
        ext_ref[0, 0:carry, :] = jnp.zeros((carry, d), F32)

    @pl.when(j > 0)
    def _():
        ext_ref[0, 0:carry, :] = ext_ref[0, tm:tm + carry, :]

    ext_ref[0, carry:carry + tm, :] = glu
    for m in range(1, V7X_SUBLANES):
        ext_ref[m, 0:length - V7X_SUBLANES, :] = ext_ref[0, m:length - V7X_SUBLANES + m, :]
    off = carry - (width - 1)

    def chunk(ci, _):
        r0 = pl.multiple_of(ci * rc, rc)
        acc = jnp.zeros((rc, d), F32) + bdw_ref[...]
        for k in range(width):
            q, m = divmod(off + k, V7X_SUBLANES)
            acc = acc + ext_ref[m, pl.ds(r0 + q * V7X_SUBLANES, rc), :] * wdw_ref[k:k + 1, :]
        c_ref[pl.ds(r0, rc), :] = acc
        return 0

    lax.fori_loop(0, tm // rc, chunk, 0)
    s = _layernorm_silu(c_ref[...], lg_ref[...], lb_ref[...])
    mo_ref[...] = _dot(s.astype(BF16), w2_ref[...]) + b2_ref[...]

    @pl.when(j == nj - 1)
    def _():
        st_ref[0] = ext_ref[0, carry + tm - (width - 1):carry + tm, :]


def _conv_prompt(hn, w1, b1, wdw, bdw, lg, lb, w2, b2, b, t, tm):
    d = hn.shape[1]
    width = wdw.shape[0]
    carry = 32
    ntj = t // tm
    const = lambda bi, j: (0, 0)
    return pl.pallas_call(
        functools.partial(_conv_prompt_kernel, tm=tm, carry=carry, width=width, rc=16),
        grid=(b, ntj),
        in_specs=[
            pl.BlockSpec((tm, d), lambda bi, j: (bi * ntj + j, 0)),
            pl.BlockSpec((d, 2 * d), const), pl.BlockSpec((1, 2 * d), const),
            pl.BlockSpec((width, d), const), pl.BlockSpec((1, d), const),
            pl.BlockSpec((1, d), const), pl.BlockSpec((1, d), const),
            pl.BlockSpec((d, d), const), pl.BlockSpec((1, d), const),
        ],
        out_specs=[pl.BlockSpec((tm, d), lambda bi, j: (bi * ntj + j, 0)),
                   pl.BlockSpec((1, width - 1, d), lambda bi, j: (bi, 0, 0))],
        out_shape=[jax.ShapeDtypeStruct((b * t, d), F32),
                   jax.ShapeDtypeStruct((b, width - 1, d), F32)],
        scratch_shapes=[pltpu.VMEM((V7X_SUBLANES, carry + tm, d), F32), pltpu.VMEM((tm, d), F32)],
        compiler_params=_cparams(("arbitrary", "arbitrary")),
        name="conv_prompt",
    )(hn, w1, b1, wdw, bdw, lg, lb, w2, b2)


def _conv_sample_kernel(hn_ref, prev_ref, w1_ref, b1_ref, wdw_ref, bdw_ref, lg_ref, lb_ref, w2_ref,
                        b2_ref, mo_ref, st_ref, g_ref, c_ref, *, bs, t, width):
    d = hn_ref.shape[1]
    state = width - 1
    a = _dot(hn_ref[...], w1_ref[...]) + b1_ref[...]
    glu = a[:, :d] * _sigmoid(a[:, d:])
    g_ref[...] = glu.reshape(bs, t, d)

    def row(i):
        if i >= 0:
            return g_ref[:, i, :]
        return prev_ref[:, state + i, :]

    for ti in range(t):
        acc = jnp.zeros((bs, d), F32) + bdw_ref[...]
        for k in range(width):
            acc = acc + row(ti + k - state) * wdw_ref[k:k + 1, :]
        c_ref[:, ti, :] = acc
    s = _layernorm_silu(c_ref[...].reshape(bs * t, d), lg_ref[...], lb_ref[...])
    mo_ref[...] = _dot(s.astype(BF16), w2_ref[...]) + b2_ref[...]
    if t >= state:
        st_ref[...] = g_ref[:, t - state:t, :]
    else:
        st_ref[:, 0:state - t, :] = prev_ref[:, t:state, :]
        st_ref[:, state - t:state, :] = g_ref[...]


def _conv_sample(hn, row0, prev, w1, b1, wdw, bdw, lg, lb, w2, b2, b, t):
    d = hn.shape[1]
    width = wdw.shape[0]
    bs = SAMPLE_SEQ_BLOCK
    blk0 = row0 // (bs * t)
    const = lambda i: (0, 0)
    return pl.pallas_call(
        functools.partial(_conv_sample_kernel, bs=bs, t=t, width=width),
        grid=(b // bs,),
        in_specs=[
            pl.BlockSpec((bs * t, d), lambda i: (blk0 + i, 0)),
            pl.BlockSpec((bs, width - 1, d), lambda i: (i, 0, 0)),
            pl.BlockSpec((d, 2 * d), const), pl.BlockSpec((1, 2 * d), const),
            pl.BlockSpec((width, d), const), pl.BlockSpec((1, d), const),
            pl.BlockSpec((1, d), const), pl.BlockSpec((1, d), const),
            pl.BlockSpec((d, d), const), pl.BlockSpec((1, d), const),
        ],
        out_specs=[pl.BlockSpec((bs * t, d), lambda i: (i, 0)),
                   pl.BlockSpec((bs, width - 1, d), lambda i: (i, 0, 0))],
        out_shape=[jax.ShapeDtypeStruct((b * t, d), F32),
                   jax.ShapeDtypeStruct((b, width - 1, d), F32)],
        scratch_shapes=[pltpu.VMEM((bs, t, d), F32), pltpu.VMEM((bs, t, d), F32)],
        compiler_params=_cparams(("arbitrary",)),
        name="conv_sample",
    )(hn, prev, w1, b1, wdw, bdw, lg, lb, w2, b2)


def _mla_qkv_kernel(hn_ref, wd_ref, qn_ref, kvn_ref, wuq_ref, wuk_ref, cos_ref, sin_ref,
                    ckv_ref, kpe_ref, ckvb_ref, kpeb_ref, ql_ref, qp_ref,
                    *, q_lora, kv_lora, rope, nh, nope, scale):
    dq = _dot(hn_ref[...], wd_ref[...])
    cq = _rms(dq[:, :q_lora], qn_ref[...]).astype(BF16)
    ckv = _rms(dq[:, q_lora:q_lora + kv_lora], kvn_ref[...])
    o = q_lora + kv_lora
    cos = cos_ref[...]
    sin = sin_ref[...]
    kpe = dq[:, o:o + rope] * cos[:, :rope] + dq[:, o + rope:o + 2 * rope] * sin[:, :rope]
    ckv_ref[...] = ckv
    kpe_ref[...] = kpe
    ckvb_ref[...] = ckv.astype(BF16)
    kpeb_ref[...] = kpe.astype(BF16)
    q = _dot(cq, wuq_ref[...])
    qn = nh * nope
    qr = nh * rope
    qp_ref[...] = ((q[:, qn:qn + qr] * cos + q[:, qn + qr:qn + 2 * qr] * sin) * scale).astype(BF16)
    for h in range(nh):
        ql = _dot(q[:, h * nope:(h + 1) * nope].astype(BF16), wuk_ref[h])
        ql_ref[:, h * kv_lora:(h + 1) * kv_lora] = (ql * scale).astype(BF16)


def _mla_qkv(hn, blk0, nrows, wd, qn, kvn, wuq, wuk, cos, sin, cos_map, tm, dims):
    q_lora, kv_lora, rope, nh, nope, scale = dims
    d = hn.shape[1]
    const2 = lambda i: (0, 0)
    return pl.pallas_call(
        functools.partial(_mla_qkv_kernel, q_lora=q_lora, kv_lora=kv_lora, rope=rope, nh=nh,
                          nope=nope, scale=scale),
        grid=(nrows // tm,),
        in_specs=[
            pl.BlockSpec((tm, d), lambda i: (blk0 + i, 0)),
            pl.BlockSpec(wd.shape, const2), pl.BlockSpec((1, q_lora), const2),
            pl.BlockSpec((1, kv_lora), const2), pl.BlockSpec(wuq.shape, const2),
            pl.BlockSpec(wuk.shape, lambda i: (0, 0, 0)),
            pl.BlockSpec((tm, nh * rope), cos_map), pl.BlockSpec((tm, nh * rope), cos_map),
        ],
        out_specs=[pl.BlockSpec((tm, kv_lora), lambda i: (i, 0)),
                   pl.BlockSpec((tm, rope), lambda i: (i, 0)),
                   pl.BlockSpec((tm, kv_lora), lambda i: (i, 0)),
                   pl.BlockSpec((tm, rope), lambda i: (i, 0)),
                   pl.BlockSpec((tm, nh * kv_lora), lambda i: (i, 0)),
                   pl.BlockSpec((tm, nh * rope), lambda i: (i, 0))],
        out_shape=[jax.ShapeDtypeStruct((nrows, kv_lora), F32),
                   jax.ShapeDtypeStruct((nrows, rope), F32),
                   jax.ShapeDtypeStruct((nrows, kv_lora), BF16),
                   jax.ShapeDtypeStruct((nrows, rope), BF16),
                   jax.ShapeDtypeStruct((nrows, nh * kv_lora), BF16),
                   jax.ShapeDtypeStruct((nrows, nh * rope), BF16)],
        compiler_params=_cparams(("arbitrary",)),
        name="mla_qkv",
    )(hn, wd, qn, kvn, wuq, wuk, cos, sin)


def _softmax_update(s, v, m_ref, l_ref, acc_ref):
    m_old = m_ref[...]
    m_new = jnp.maximum(m_old, jnp.max(s, axis=1, keepdims=True))
    alpha = jnp.exp(m_old - m_new)
    p = jnp.exp(s - m_new)
    l_ref[...] = alpha * l_ref[...] + jnp.sum(p, axis=1, keepdims=True)
    acc_ref[...] = alpha * acc_ref[...] + _dot(p.astype(BF16), v)
    m_ref[...] = m_new


def _attn_prompt_kernel(ql_ref, qp_ref, c_ref, kr_ref, o_ref, m_ref, l_ref, acc_ref,
                        *, tq, tk, head_shift):
    i = pl.program_id(1)
    rows = ql_ref.shape[0]
    m_ref[...] = jnp.full(m_ref.shape, NEG_INF, F32)
    l_ref[...] = jnp.zeros(l_ref.shape, F32)
    acc_ref[...] = jnp.zeros(acc_ref.shape, F32)
    ql = ql_ref[...]
    qp = qp_ref[...]
    qpos = i * tq + lax.shift_right_logical(lax.broadcasted_iota(jnp.int32, (rows, 1), 0), head_shift)
    n_chunks = (i * tq + tq - 1) // tk + 1

    def body(kk, _):
        ks = pl.multiple_of(kk * tk, tk)
        ck = c_ref[pl.ds(ks, tk), :]
        krk = kr_ref[pl.ds(ks, tk), :]
        s = _dot_t(ql, ck) + _dot_t(qp, krk)
        kpos = ks + lax.broadcasted_iota(jnp.int32, (rows, tk), 1)
        s = jnp.where(kpos <= qpos, s, NEG_INF)
        _softmax_update(s, ck, m_ref, l_ref, acc_ref)
        return 0

    lax.fori_loop(0, n_chunks, body, 0)
    o_ref[...] = (acc_ref[...] / l_ref[...]).astype(BF16)


def _attn_prompt(ql, qp, c, kr, b, t, nh):
    kv = c.shape[1]
    rope = kr.shape[1]
    tq = min(ATTN_Q_STEPS, t)
    tk = min(ATTN_K_TILE, t)
    rows = tq * nh
    nq = t // tq
    head_shift = nh.bit_length() - 1
    assert 1 << head_shift == nh
    return pl.pallas_call(
        functools.partial(_attn_prompt_kernel, tq=tq, tk=tk, head_shift=head_shift),
        grid=(b, nq),
        in_specs=[
            pl.BlockSpec((rows, kv), lambda bi, i: (bi * nq + i, 0)),
            pl.BlockSpec((rows, rope), lambda bi, i: (bi * nq + i, 0)),
            pl.BlockSpec((t, kv), lambda bi, i: (bi, 0)),
            pl.BlockSpec((t, rope), lambda bi, i: (bi, 0)),
        ],
        out_specs=pl.BlockSpec((rows, kv), lambda bi, i: (bi * nq + i, 0)),
        out_shape=jax.ShapeDtypeStruct((b * t * nh, kv), BF16),
        scratch_shapes=[pltpu.VMEM((rows, 1), F32), pltpu.VMEM((rows, 1), F32),
                        pltpu.VMEM((rows, kv), F32)],
        compiler_params=_cparams(("arbitrary", "arbitrary")),
        name="mla_attn_prompt",
    )(ql, qp, c, kr)


def _attn_sample_kernel(pt_ref, ql_ref, qp_ref, cn_ref, krn_ref, *rest, pages, t, head_shift, page):
    lat_refs = rest[:pages]
    kr_refs = rest[pages:2 * pages]
    o_ref, m_ref, l_ref, acc_ref = rest[2 * pages:]
    del pt_ref
    j = pl.program_id(1)
    nj = pl.num_programs(1)
    rows = ql_ref.shape[0]

    @pl.when(j == 0)
    def _():
        m_ref[...] = jnp.full(m_ref.shape, NEG_INF, F32)
        l_ref[...] = jnp.zeros(l_ref.shape, F32)
        acc_ref[...] = jnp.zeros(acc_ref.shape, F32)

    ql = ql_ref[...]
    qp = qp_ref[...]
    cs = [lat_refs[p][...].astype(BF16) for p in range(pages)]
    s = jnp.concatenate(
        [_dot_t(ql, cs[p]) + _dot_t(qp, kr_refs[p][...].astype(BF16)) for p in range(pages)], axis=1)
    _softmax_update(s, jnp.concatenate(cs, axis=0), m_ref, l_ref, acc_ref)

    @pl.when(j == nj - 1)
    def _():
        kv = cn_ref.shape[1]
        rope = krn_ref.shape[1]
        cn = jnp.concatenate([cn_ref[...], jnp.zeros((page - t, kv), F32)], axis=0).astype(BF16)
        krn = jnp.concatenate([krn_ref[...], jnp.zeros((page - t, rope), F32)], axis=0).astype(BF16)
        sn = _dot_t(ql, cn) + _dot_t(qp, krn)
        qstep = lax.shift_right_logical(lax.broadcasted_iota(jnp.int32, (rows, 1), 0), head_shift)
        kstep = lax.broadcasted_iota(jnp.int32, (rows, page), 1)
        sn = jnp.where(kstep <= qstep, sn, NEG_INF)
        _softmax_update(sn, cn, m_ref, l_ref, acc_ref)
        o_ref[...] = (acc_ref[...] / l_ref[...]).astype(BF16)


def _attn_sample(page_table, ql, qp, cn, krn, cache_lat, cache_kr, layer, b, t, nh):
    kv = cn.shape[1]
    rope = krn.shape[1]
    page = cache_lat.shape[2]
    npages = page_table.shape[1]
    pages = min(DECODE_PAGES, npages)
    assert npages % pages == 0 and t % V7X_SUBLANES == 0 and t <= page
    rows = t * nh
    head_shift = nh.bit_length() - 1
    pt = page_table.reshape(-1)

    def lat_map(p):
        return lambda bi, j, ptr: (layer, ptr[bi * npages + j * pages + p], 0, 0)

    in_specs = [
        pl.BlockSpec((rows, kv), lambda bi, j, ptr: (bi, 0)),
        pl.BlockSpec((rows, rope), lambda bi, j, ptr: (bi, 0)),
        pl.BlockSpec((t, kv), lambda bi, j, ptr: (bi, 0)),
        pl.BlockSpec((t, rope), lambda bi, j, ptr: (bi, 0)),
    ]
    in_specs += [pl.BlockSpec((None, None, page, kv), lat_map(p)) for p in range(pages)]
    in_specs += [pl.BlockSpec((None, None, page, rope), lat_map(p)) for p in range(pages)]
    grid_spec = pltpu.PrefetchScalarGridSpec(
        num_scalar_prefetch=1,
        grid=(b, npages // pages),
        in_specs=in_specs,
        out_specs=pl.BlockSpec((rows, kv), lambda bi, j, ptr: (bi, 0)),
        scratch_shapes=[pltpu.VMEM((rows, 1), F32), pltpu.VMEM((rows, 1), F32),
                        pltpu.VMEM((rows, kv), F32)],
    )
    return pl.pallas_call(
        functools.partial(_attn_sample_kernel, pages=pages, t=t, head_shift=head_shift, page=page),
        grid_spec=grid_spec,
        out_shape=jax.ShapeDtypeStruct((b * rows, kv), BF16),
        compiler_params=_cparams(("arbitrary", "arbitrary")),
        name="mla_attn_sample",
    )(pt, ql, qp, cn, krn, *([cache_lat] * pages), *([cache_kr] * pages))


def _mla_out_kernel(o_ref, wuv_ref, wo_ref, mo_ref, *, nh, kv):
    vs = [_dot(o_ref[:, h * kv:(h + 1) * kv], wuv_ref[h]).astype(BF16) for h in range(nh)]
    mo_ref[...] = _dot(jnp.concatenate(vs, axis=1), wo_ref[...])


def _mla_out(o, wuv, wo, tm):
    n = o.shape[0]
    nh, kv, vd = wuv.shape
    d = wo.shape[1]
    return pl.pallas_call(
        functools.partial(_mla_out_kernel, nh=nh, kv=kv),
        grid=(n // tm,),
        in_specs=[pl.BlockSpec((tm, nh * kv), lambda i: (i, 0)),
                  pl.BlockSpec(wuv.shape, lambda i: (0, 0, 0)),
                  pl.BlockSpec(wo.shape, lambda i: (0, 0))],
        out_specs=pl.BlockSpec((tm, d), lambda i: (i, 0)),
        out_shape=jax.ShapeDtypeStruct((n, d), F32),
        compiler_params=_cparams(("arbitrary",)),
        name="mla_out",
    )(o, wuv, wo)


def _route_kernel(h_ref, mop_ref, mos_ref, g_ref, whi_ref, wlo_ref, br_ref,
                  h1_ref, xn_ref, slab_ref, cnt_ref, carry_ref, *, npt, tm, n_groups, epg):
    i = pl.program_id(0)

    @pl.when(i == 0)
    def _():
        carry_ref[...] = jnp.zeros(carry_ref.shape, F32)

    h1 = h_ref[...] + jnp.where(i < npt, mop_ref[...], mos_ref[...])
    h1_ref[...] = h1
    xn = _rms(h1, g_ref[...])
    xn_ref[...] = xn
    x_hi = xn.astype(BF16)
    x_lo = (xn - x_hi.astype(F32)).astype(BF16)
    whi = whi_ref[...]
    logits = _dot(x_hi, whi) + _dot(x_lo, whi) + _dot(x_hi, wlo_ref[...]) + br_ref[...]

    lane = lax.broadcasted_iota(jnp.int32, (tm, V7X_LANES), 1).astype(F32)
    big = float(V7X_LANES)
    gl = jnp.where(lane < n_groups, logits, NEG_INF)
    gmax = jnp.max(gl, axis=1, keepdims=True)
    gsel = jnp.min(jnp.where(gl == gmax, lane, big), axis=1, keepdims=True)
    gprob = 1.0 / jnp.sum(jnp.exp(gl - gmax), axis=1, keepdims=True)
    lo = n_groups + gsel * epg
    el = jnp.where(lane >= lo, jnp.where(lane < lo + epg, logits, NEG_INF), NEG_INF)
    m1 = jnp.max(el, axis=1, keepdims=True)
    i1 = jnp.min(jnp.where(el == m1, lane, big), axis=1, keepdims=True)
    el2 = jnp.where(lane == i1, NEG_INF, el)
    m2 = jnp.max(el2, axis=1, keepdims=True)
    i2 = jnp.min(jnp.where(el2 == m2, lane, big), axis=1, keepdims=True)
    r = jnp.exp(m2 - m1)
    g1 = gprob / (1.0 + r)
    g2 = gprob * r / (1.0 + r)

    oh1 = jnp.where(lane == i1, 1.0, 0.0)
    oh2 = jnp.where(lane == i2, 1.0, 0.0)
    oh = oh1 + oh2
    rr = lax.broadcasted_iota(jnp.int32, (tm, tm), 0)
    cc = lax.broadcasted_iota(jnp.int32, (tm, tm), 1)
    tri = jnp.where(rr > cc, 1.0, 0.0).astype(BF16)
    carry = carry_ref[0:1, :]
    cum = _dot(tri, oh.astype(BF16)) + carry
    rank1 = jnp.sum(oh1 * cum, axis=1, keepdims=True)
    rank2 = jnp.sum(oh2 * cum, axis=1, keepdims=True)
    new_carry = carry + jnp.sum(oh, axis=0, keepdims=True)
    carry_ref[...] = jnp.broadcast_to(new_carry, carry_ref.shape)
    cnt_ref[...] = jnp.broadcast_to(new_carry, cnt_ref.shape)

    slab = jnp.where(lane == 0, i1 - n_groups, 0.0)
    slab = jnp.where(lane == 1, i2 - n_groups, slab)
    slab = jnp.where(lane == 2, g1, slab)
    slab = jnp.where(lane == 3, g2, slab)
    slab = jnp.where(lane == 4, rank1, slab)
    slab = jnp.where(lane == 5, rank2, slab)
    slab_ref[...] = slab


def _route(h, mo_p, mo_s, g, whi, wlo, br, tm, n_groups, epg):
    n, d = h.shape
    npt = mo_p.shape[0] // tm
    const = lambda i: (0, 0)
    return pl.pallas_call(
        functools.partial(_route_kernel, npt=npt, tm=tm, n_groups=n_groups, epg=epg),
        grid=(n // tm,),
        in_specs=[
            pl.BlockSpec((tm, d), lambda i: (i, 0)),
            pl.BlockSpec((tm, d), lambda i: (jnp.minimum(i, npt - 1), 0)),
            pl.BlockSpec((tm, d), lambda i: (jnp.maximum(i - npt, 0), 0)),
            pl.BlockSpec((1, d), const),
            pl.BlockSpec((d, V7X_LANES), const), pl.BlockSpec((d, V7X_LANES), const),
            pl.BlockSpec((1, V7X_LANES), const),
        ],
        out_specs=[pl.BlockSpec((tm, d), lambda i: (i, 0)),
                   pl.BlockSpec((tm, d), lambda i: (i, 0)),
                   pl.BlockSpec((tm, V7X_LANES), lambda i: (i, 0)),
                   pl.BlockSpec((V7X_SUBLANES, V7X_LANES), const)],
        out_shape=[jax.ShapeDtypeStruct((n, d), F32), jax.ShapeDtypeStruct((n, d), F32),
                   jax.ShapeDtypeStruct((n, V7X_LANES), F32),
                   jax.ShapeDtypeStruct((V7X_SUBLANES, V7X_LANES), F32)],
        scratch_shapes=[pltpu.VMEM((V7X_SUBLANES, V7X_LANES), F32)],
        compiler_params=_cparams(("arbitrary",)),
        name="route",
    )(h, mo_p, mo_s, g, whi, wlo, br)


def _in_row_copy(xn_hbm, xbuf, sem, tok, r):
    return pltpu.make_async_copy(xn_hbm.at[pl.ds(tok, 1)], xbuf.at[pl.ds(r, 1)], sem)


def _out_row_copy(ybuf, y_hbm, sem, dst, r):
    return pltpu.make_async_copy(ybuf.at[pl.ds(r, 1)], y_hbm.at[pl.ds(dst, 1)], sem)


def _wait_rows(src, dst, sem, count):
    aligned = pl.multiple_of((count // V7X_SUBLANES) * V7X_SUBLANES, V7X_SUBLANES)

    @pl.when(aligned > 0)
    def _():
        pltpu.make_async_copy(src.at[pl.ds(0, aligned)], dst.at[pl.ds(0, aligned)], sem).wait()

    def one(r, c):
        pltpu.make_async_copy(src.at[pl.ds(0, 1)], dst.at[pl.ds(0, 1)], sem).wait()
        return c

    lax.fori_loop(0, count - aligned, one, 0)


def _experts_kernel(blk_e_ref, nblk_ref, valid_ref, tok_ref, dst_ref, xn_hbm, w1_ref, w3_ref, w2_ref,
                    y_hbm, xbuf, ybuf, w1b, w3b, w2b, sem_in, sem_out, *, tm):
    i = pl.program_id(0)

    @pl.when(i == 0)
    def _():
        xbuf[...] = jnp.zeros(xbuf.shape, F32)

    @pl.when(i < nblk_ref[0])
    def _():
        base = i * tm
        valid = valid_ref[i]

        def issue_in(r, c):
            _in_row_copy(xn_hbm, xbuf, sem_in, tok_ref[base + r], r).start()
            return c

        lax.fori_loop(0, valid, issue_in, 0)
        e = blk_e_ref[i]
        e_prev = blk_e_ref[jnp.maximum(i - 1, 0)]

        @pl.when((i == 0) | (e != e_prev))
        def _():
            w1b[...] = w1_ref[...].astype(BF16)
            w3b[...] = w3_ref[...].astype(BF16)
            w2b[...] = w2_ref[...].astype(BF16)

        _wait_rows(xn_hbm, xbuf, sem_in, valid)
        x = xbuf[...].astype(BF16)
        a = _dot(x, w1b[...])
        hmid = (a * _sigmoid(a)) * _dot(x, w3b[...])
        ybuf[...] = _dot(hmid.astype(BF16), w2b[...])

        def issue_out(r, c):
            _out_row_copy(ybuf, y_hbm, sem_out, dst_ref[base + r], r).start()
            return c

        lax.fori_loop(0, valid, issue_out, 0)
        _wait_rows(ybuf, y_hbm, sem_out, valid)


def _experts(blk_e, nblk, valid, row_tok, row_dst, xn, w1, w3, w2, layer, n_out_rows, tm):
    n, d = xn.shape
    f = w1.shape[-1]
    n_blk = blk_e.shape[0]
    wmap = lambda i, be, nb, va, rt, rd: (layer, be[i], 0, 0)
    grid_spec = pltpu.PrefetchScalarGridSpec(
        num_scalar_prefetch=5,
        grid=(n_blk,),
        in_specs=[
            pl.BlockSpec(memory_space=pl.ANY),
            pl.BlockSpec((None, None, d, f), wmap),
            pl.BlockSpec((None, None, d, f), wmap),
            pl.BlockSpec((None, None, f, d), wmap),
        ],
        out_specs=pl.BlockSpec(memory_space=pl.ANY),
        scratch_shapes=[
            pltpu.VMEM((tm, d), F32), pltpu.VMEM((tm, d), F32),
            pltpu.VMEM((d, f), BF16), pltpu.VMEM((d, f), BF16), pltpu.VMEM((f, d), BF16),
            pltpu.SemaphoreType.DMA, pltpu.SemaphoreType.DMA,
        ],
    )
    return pl.pallas_call(
        functools.partial(_experts_kernel, tm=tm),
        grid_spec=grid_spec,
        out_shape=jax.ShapeDtypeStruct((n_out_rows, d), F32),
        compiler_params=_cparams(("arbitrary",)),
        name="moe_experts",
    )(blk_e, nblk, valid, row_tok, row_dst, xn, w1, w3, w2)


def _dispatch_tables(slab, cnt, n, n_experts, n_groups, tm):
    e = slab[:, 0:TOP_K].astype(jnp.int32)
    rank = slab[:, 4:4 + TOP_K].astype(jnp.int32)
    counts = cnt[0, n_groups:n_groups + n_experts].astype(jnp.int32)
    padded = (counts + tm - 1) // tm * tm
    pad_ends = jnp.cumsum(padded)
    pad_starts = pad_ends - padded
    dest = (pad_starts[e] + rank).reshape(-1)
    n_rows = (n * TOP_K + n_experts * (tm - 1) + tm - 1) // tm * tm
    n_blk = n_rows // tm
    tok = jnp.repeat(jnp.arange(n, dtype=jnp.int32), TOP_K)
    choice = jnp.tile(jnp.arange(TOP_K, dtype=jnp.int32), n)
    row_tok = jnp.zeros((n_rows,), jnp.int32).at[dest].set(tok)
    row_dst = jnp.zeros((n_rows,), jnp.int32).at[dest].set(choice * n + tok)
    blk_start = jnp.arange(n_blk, dtype=jnp.int32) * tm
    blk_e = jnp.minimum(jnp.searchsorted(pad_ends, blk_start, side='right'),
                        n_experts - 1).astype(jnp.int32)
    valid = jnp.clip(pad_starts[blk_e] + counts[blk_e] - blk_start, 0, tm).astype(jnp.int32)
    nblk = (pad_ends[-1:] // tm).astype(jnp.int32)
    return blk_e, nblk, valid, row_tok, row_dst


def _ple_kernel(h1_ref, y0_ref, y1_ref, slab_ref, pp_ref, ps_ref, gple_ref, wg_ref, wp_ref, gn_ref,
                *outs, npt, last):
    i = pl.program_id(0)
    slab = slab_ref[...]
    h2 = h1_ref[...] + slab[:, 2:3] * y0_ref[...] + slab[:, 3:4] * y1_ref[...]
    gate = _sigmoid(_dot(_rms(h2, gple_ref[...]).astype(BF16), wg_ref[...]))
    p = jnp.where(i < npt, pp_ref[...], ps_ref[...]).astype(BF16)
    h3 = h2 + gate * _dot(p, wp_ref[...])
    hn = _rms(h3, gn_ref[...])
    if last:
        yp_ref, ys_ref = outs

        @pl.when(i < npt)
        def _():
            yp_ref[...] = hn

        @pl.when(i >= npt)
        def _():
            ys_ref[...] = hn
    else:
        h_ref, hn_ref = outs
        h_ref[...] = h3
        hn_ref[...] = hn.astype(BF16)


def _ple(h1, y_all, slab, p_p, p_s, layer, gple, wg, wp, gn, tm, last):
    n, d = h1.shape
    np_ = p_p.shape[1]
    npt = np_ // tm
    nt = n // tm
    pd = p_p.shape[2]
    const = lambda i: (0, 0)
    if last:
        out_specs = [pl.BlockSpec((tm, d), lambda i: (jnp.minimum(i, npt - 1), 0)),
                     pl.BlockSpec((tm, d), lambda i: (jnp.maximum(i - npt, 0), 0))]
        out_shape = [jax.ShapeDtypeStruct((np_, d), F32), jax.ShapeDtypeStruct((n - np_, d), F32)]
    else:
        out_specs = [pl.BlockSpec((tm, d), lambda i: (i, 0)), pl.BlockSpec((tm, d), lambda i: (i, 0))]
        out_shape = [jax.ShapeDtypeStruct((n, d), F32), jax.ShapeDtypeStruct((n, d), BF16)]
    return pl.pallas_call(
        functools.partial(_ple_kernel, npt=npt, last=last),
        grid=(nt,),
        in_specs=[
            pl.BlockSpec((tm, d), lambda i: (i, 0)),
            pl.BlockSpec((tm, d), lambda i: (i, 0)),
            pl.BlockSpec((tm, d), lambda i: (i + nt, 0)),
            pl.BlockSpec((tm, V7X_LANES), lambda i: (i, 0)),
            pl.BlockSpec((None, tm, pd), lambda i: (layer, jnp.minimum(i, npt - 1), 0)),
            pl.BlockSpec((None, tm, pd), lambda i: (layer, jnp.maximum(i - npt, 0), 0)),
            pl.BlockSpec((1, d), const), pl.BlockSpec((d, d), const), pl.BlockSpec((pd, d), const),
            pl.BlockSpec((1, d), const),
        ],
        out_specs=out_specs,
        out_shape=out_shape,
        compiler_params=_cparams(("arbitrary",)),
        name="ple",
    )(h1, y_all, y_all, slab, p_p, p_s, gple, wg, wp, gn)


def _rope_tables(pos, rope, nh):
    half = rope // 2
    inv = 1.0 / (ROPE_THETA ** (jnp.arange(half, dtype=F32) * (2.0 / rope)))
    ang = pos.astype(F32)[:, None] * inv[None, :]
    cos = jnp.tile(jnp.cos(ang), (1, 2 * nh))
    sin = jnp.tile(jnp.sin(ang), (1, 2 * nh))
    return cos, sin


def _rot_cols(w, rope):
    half = rope // 2
    w3 = w.reshape(w.shape[0], -1, rope)
    return jnp.concatenate([-w3[..., half:], w3[..., :half]], axis=-1).reshape(w.shape)


def kernel(x_prompt, x_sample, state_pool, state_conv, cache_latent, cache_krope, page_table, p_prompt, p_sample, norm_mix, norm_ffn, norm_ple, norm_final, pool_w_in, pool_w_grp, pool_scale, conv_w_pw1, conv_b_pw1, conv_w_dw, conv_b_dw, conv_ln_g, conv_ln_b, conv_w_pw2, conv_b_pw2, mla_w_dqkv, mla_q_norm, mla_kv_norm, mla_w_uq, mla_w_uk, mla_w_uv, mla_w_o, moe_w_group, moe_b_group, moe_w_expert, moe_b_expert, moe_w1, moe_w3, moe_w2, ple_w_gate, ple_w_proj):
    b, t, d = x_prompt.shape
    bs, ts, _ = x_sample.shape
    depth = norm_mix.shape[0]
    np_, ns = b * t, bs * ts
    n = np_ + ns
    tm = min(TOKEN_TILE, ns)
    assert np_ % tm == 0 and ns % tm == 0
    past_len = page_table.shape[1] * cache_latent.shape[2]
    n_groups = moe_w_group.shape[-1]
    n_experts = moe_w_expert.shape[-1]
    epg = n_experts // n_groups
    assert n_groups + n_experts <= V7X_LANES
    pd = p_prompt.shape[-1]
    row = lambda v: v.reshape(1, -1)

    nh, kv_lora, nope = mla_w_uk.shape[1:]
    rope = cache_krope.shape[-1]
    q_lora = mla_w_dqkv.shape[-1] - kv_lora - rope
    scale = float((nope + rope) ** -0.5)
    mla_dims = (q_lora, kv_lora, rope, nh, nope, scale)

    h, hn = _prep(x_prompt.reshape(np_, d), x_sample.reshape(ns, d), row(norm_mix[0]), tm)
    p_p = p_prompt.reshape(depth, np_, pd)
    p_s = p_sample.reshape(depth, ns, pd)

    new_pool_p, new_pool_s, new_conv_p, new_conv_s = [], [], [], []
    new_lat_p, new_kr_p, new_lat_s, new_kr_s = [], [], [], []
    y_p = y_s = None
    for i in range(depth):
        kind, j = i % N_MIXERS, i // N_MIXERS
        if kind == 0:
            w_in = pool_w_in[j].astype(BF16)
            w_grp = pool_w_grp[j].astype(BF16)
            sc = row(pool_scale[j])
            mo_p, st_p = _pool_prompt(hn, w_in, w_grp, sc, b, t, min(tm, t))
            mo_s, st_s = _pool_sample(hn, np_, state_pool[j], w_in, w_grp, sc, bs, ts, past_len)
            new_pool_p.append(st_p)
            new_pool_s.append(st_s)
        elif kind == 1:
            w1 = conv_w_pw1[j].astype(BF16)
            w2 = conv_w_pw2[j].astype(BF16)
            args = (w1, row(conv_b_pw1[j]), conv_w_dw[j], row(conv_b_dw[j]), row(conv_ln_g[j]),
                    row(conv_ln_b[j]), w2, row(conv_b_pw2[j]))
            mo_p, st_p = _conv_prompt(hn, *args, b, t, min(CONV_TILE, t))
            mo_s, st_s = _conv_sample(hn, np_, state_conv[j], *args, bs, ts)
            new_conv_p.append(st_p)
            new_conv_s.append(st_s)
        else:
            wd = mla_w_dqkv[j]
            wd = jnp.concatenate([wd, _rot_cols(wd[:, q_lora + kv_lora:], rope)], axis=1).astype(BF16)
            wuq = mla_w_uq[j].reshape(q_lora, nh, nope + rope)
            wq_nope = wuq[:, :, :nope].reshape(q_lora, nh * nope)
            wq_rope = wuq[:, :, nope:].reshape(q_lora, nh * rope)
            wuq = jnp.concatenate([wq_nope, wq_rope, _rot_cols(wq_rope, rope)], axis=1).astype(BF16)
            wuk = jnp.swapaxes(mla_w_uk[j], 1, 2).astype(BF16)
            wuv = mla_w_uv[j].astype(BF16)
            wo = mla_w_o[j].astype(BF16)
            qn, kvn = row(mla_q_norm[j]), row(mla_kv_norm[j])
            tmp = min(tm, t)
            cos_p, sin_p = _rope_tables(jnp.arange(t, dtype=jnp.int32), rope, nh)
            ntj = t // tmp
            ckv_p, kpe_p, ckvb_p, kpeb_p, ql_p, qp_p = _mla_qkv(
                hn, 0, np_, wd, qn, kvn, wuq, wuk, cos_p, sin_p, lambda ii: (ii % ntj, 0), tmp, mla_dims)
            cos_s, sin_s = _rope_tables(past_len + jnp.arange(ts, dtype=jnp.int32), rope, nh)
            cos_s, sin_s = jnp.tile(cos_s, (bs, 1)), jnp.tile(sin_s, (bs, 1))
            ckv_s, kpe_s, _, _, ql_s, qp_s = _mla_qkv(
                hn, np_ // tm, ns, wd, qn, kvn, wuq, wuk, cos_s, sin_s, lambda ii: (ii, 0), tm, mla_dims)
            o_p = _attn_prompt(ql_p.reshape(np_ * nh, kv_lora), qp_p.reshape(np_ * nh, rope),
                               ckvb_p, kpeb_p, b, t, nh)
            o_s = _attn_sample(page_table, ql_s.reshape(ns * nh, kv_lora), qp_s.reshape(ns * nh, rope),
                               ckv_s, kpe_s, cache_latent, cache_krope, j, bs, ts, nh)
            mo_p = _mla_out(o_p.reshape(np_, nh * kv_lora), wuv, wo, tm)
            mo_s = _mla_out(o_s.reshape(ns, nh * kv_lora), wuv, wo, tm)
            new_lat_p.append(ckv_p.reshape(b, t, kv_lora))
            new_kr_p.append(kpe_p.reshape(b, t, rope))
            new_lat_s.append(ckv_s.reshape(bs, ts, kv_lora))
            new_kr_s.append(kpe_s.reshape(bs, ts, rope))

        wr = jnp.zeros((d, V7X_LANES), F32)
        wr = wr.at[:, :n_groups].set(moe_w_group[i]).at[:, n_groups:n_groups + n_experts].set(moe_w_expert[i])
        br = jnp.zeros((1, V7X_LANES), F32)
        br = br.at[0, :n_groups].set(moe_b_group[i]).at[0, n_groups:n_groups + n_experts].set(moe_b_expert[i])
        whi = wr.astype(BF16)
        wlo = (wr - whi.astype(F32)).astype(BF16)
        h1, xn, slab, cnt = _route(h, mo_p, mo_s, row(norm_ffn[i]), whi, wlo, br, tm, n_groups, epg)
        blk_e, nblk, valid, row_tok, row_dst = _dispatch_tables(slab, cnt, n, n_experts, n_groups, MOE_TILE)
        y_all = _experts(blk_e, nblk, valid, row_tok, row_dst, xn, moe_w1, moe_w3, moe_w2, i,
                         TOP_K * n, MOE_TILE)
        last = i == depth - 1
        gn = row(norm_final) if last else row(norm_mix[i + 1])
        out = _ple(h1, y_all, slab, p_p, p_s, i, row(norm_ple[i]), ple_w_gate[i].astype(BF16),
                   ple_w_proj[i].astype(BF16), gn, tm, last)
        if last:
            y_p, y_s = out
        else:
            h, hn = out

    return (y_p.reshape(b, t, d), y_s.reshape(bs, ts, d),
            jnp.stack(new_pool_p), jnp.stack(new_pool_s),
            jnp.stack(new_conv_p), jnp.stack(new_conv_s),
            jnp.stack(new_lat_p), jnp.stack(new_kr_p),
            jnp.stack(new_lat_s), jnp.stack(new_kr_s))
```

```python
import functools

import jax
import jax.numpy as jnp
from jax import lax
from jax.experimental import pallas as pl
from jax.experimental.pallas import tpu as pltpu

F32 = jnp.float32
BF16 = jnp.bfloat16

POOL_WINDOWS = (2, 4, 8, 16)
TOP_K = 2
ROPE_THETA = 10000.0
RMS_EPS = 1e-6
LN_EPS = 1e-5
NEG_INF = -1e30
N_MIXERS = 3

V7X_LANES = 128
V7X_SUBLANES = 8
V7X_VMEM_LIMIT_BYTES = 48 * 1024 * 1024

TOKEN_TILE = 512
MOE_TILE = 256
CONV_TILE = 256
ATTN_Q_TILE = 256
ATTN_K_TILE = 512
DECODE_PAGE_CHUNK = 8
SAMPLE_SEQ_BLOCK = 32
DMA_ISSUE_UNROLL = 8


def _cparams(sem):
    return pltpu.CompilerParams(dimension_semantics=sem, vmem_limit_bytes=V7X_VMEM_LIMIT_BYTES)


def _rms(x, g):
    ms = jnp.mean(x * x, axis=-1, keepdims=True)
    return x * lax.rsqrt(ms + RMS_EPS) * g


def _dot(a, b):
    return jnp.dot(a, b, preferred_element_type=F32)


def _dot_t(a, b):
    return lax.dot_general(a, b, (((1,), (1,)), ((), ())), preferred_element_type=F32)


def _sigmoid(x):
    return 1.0 / (1.0 + jnp.exp(-x))


def _prep_kernel(xp_ref, xs_ref, g_ref, h_ref, hn_ref, *, npt):
    i = pl.program_id(0)
    x = jnp.where(i < npt, xp_ref[...], xs_ref[...])
    h_ref[...] = x
    hn_ref[...] = _rms(x, g_ref[...]).astype(BF16)


def _prep(xp, xs, g, tm):
    np_, d = xp.shape
    ns = xs.shape[0]
    npt, nst = np_ // tm, ns // tm
    n = np_ + ns
    return pl.pallas_call(
        functools.partial(_prep_kernel, npt=npt),
        grid=(npt + nst,),
        in_specs=[
            pl.BlockSpec((tm, d), lambda i: (jnp.minimum(i, npt - 1), 0)),
            pl.BlockSpec((tm, d), lambda i: (jnp.maximum(i - npt, 0), 0)),
            pl.BlockSpec((1, d), lambda i: (0, 0)),
        ],
        out_specs=[pl.BlockSpec((tm, d), lambda i: (i, 0)),
                   pl.BlockSpec((tm, d), lambda i: (i, 0))],
        out_shape=[jax.ShapeDtypeStruct((n, d), F32), jax.ShapeDtypeStruct((n, d), BF16)],
        compiler_params=_cparams(("arbitrary",)),
        name="prep",
    )(xp, xs, g)


def _pool_prompt_kernel(hn_ref, win_ref, wg_ref, sc_ref, mo_ref, st_ref, ext_ref, *, tm, carry, state):
    j = pl.program_id(1)
    nj = pl.num_programs(1)
    d = hn_ref.shape[1]
    gd = d // len(POOL_WINDOWS)
    u = _dot(hn_ref[...], win_ref[...])

    @pl.when(j == 0)
    def _():
        ext_ref[0:carry, :] = jnp.zeros((carry, d), F32)

    @pl.when(j > 0)
    def _():
        ext_ref[0:carry, :] = ext_ref[tm:tm + carry, :]

    ext_ref[carry:carry + tm, :] = u
    pos = j * tm + lax.broadcasted_iota(jnp.int32, (tm, 1), 0)
    for g, w in enumerate(POOL_WINDOWS):
        c0, c1 = g * gd, (g + 1) * gd
        ug = u[:, c0:c1]
        s = ug
        for k in range(1, w):
            s = s + ext_ref[carry - k:carry - k + tm, c0:c1]
        cnt = jnp.minimum(pos + 1, w).astype(F32)
        z = s / cnt - ug
        y = _dot(z.astype(BF16), wg_ref[g])
        mo_ref[:, c0:c1] = y * sc_ref[:, c0:c1]

    @pl.when(j == nj - 1)
    def _():
        st_ref[0] = ext_ref[carry + tm - state:carry + tm, :]


def _pool_prompt(hn, w_in, w_grp, scale, b, t, tm):
    d = hn.shape[1]
    state = max(POOL_WINDOWS) - 1
    carry = 16
    ntj = t // tm
    return pl.pallas_call(
        functools.partial(_pool_prompt_kernel, tm=tm, carry=carry, state=state),
        grid=(b, ntj),
        in_specs=[
            pl.BlockSpec((tm, d), lambda bi, j: (bi * ntj + j, 0)),
            pl.BlockSpec((d, d), lambda bi, j: (0, 0)),
            pl.BlockSpec(w_grp.shape, lambda bi, j: (0, 0, 0)),
            pl.BlockSpec((1, d), lambda bi, j: (0, 0)),
        ],
        out_specs=[pl.BlockSpec((tm, d), lambda bi, j: (bi * ntj + j, 0)),
                   pl.BlockSpec((1, state, d), lambda bi, j: (bi, 0, 0))],
        out_shape=[jax.ShapeDtypeStruct((b * t, d), F32), jax.ShapeDtypeStruct((b, state, d), F32)],
        scratch_shapes=[pltpu.VMEM((carry + tm, d), F32)],
        compiler_params=_cparams(("arbitrary", "arbitrary")),
        name="pool_prompt",
    )(hn, w_in, w_grp, scale)


def _pool_sample_kernel(hn_ref, prev_ref, win_ref, wg_ref, sc_ref, mo_ref, st_ref, u_ref, z_ref,
                        *, bs, t, state, pos0):
    d = hn_ref.shape[1]
    gd = d // len(POOL_WINDOWS)
    u = _dot(hn_ref[...], win_ref[...])
    u_ref[...] = u.reshape(bs, t, d)

    def row(i, c0, c1):
        if i >= 0:
            return u_ref[:, i, c0:c1]
        return prev_ref[:, state + i, c0:c1]

    for g, w in enumerate(POOL_WINDOWS):
        c0, c1 = g * gd, (g + 1) * gd
        for ti in range(t):
            s = row(ti, c0, c1)
            for k in range(1, w):
                s = s + row(ti - k, c0, c1)
            cnt = float(min(pos0 + ti + 1, w))
            z_ref[:, ti, c0:c1] = s / cnt - row(ti, c0, c1)
    z = z_ref[...].reshape(bs * t, d)
    for g in range(len(POOL_WINDOWS)):
        c0, c1 = g * gd, (g + 1) * gd
        y = _dot(z[:, c0:c1].astype(BF16), wg_ref[g])
        mo_ref[:, c0:c1] = y * sc_ref[:, c0:c1]
    if t >= state:
        st_ref[...] = u_ref[:, t - state:t, :]
    else:
        st_ref[:, 0:state - t, :] = prev_ref[:, t:state, :]
        st_ref[:, state - t:state, :] = u_ref[...]


def _pool_sample(hn, row0, prev, w_in, w_grp, scale, b, t, pos0):
    d = hn.shape[1]
    state = prev.shape[1]
    bs = SAMPLE_SEQ_BLOCK
    blk0 = row0 // (bs * t)
    return pl.pallas_call(
        functools.partial(_pool_sample_kernel, bs=bs, t=t, state=state, pos0=pos0),
        grid=(b // bs,),
        in_specs=[
            pl.BlockSpec((bs * t, d), lambda i: (blk0 + i, 0)),
            pl.BlockSpec((bs, state, d), lambda i: (i, 0, 0)),
            pl.BlockSpec((d, d), lambda i: (0, 0)),
            pl.BlockSpec(w_grp.shape, lambda i: (0, 0, 0)),
            pl.BlockSpec((1, d), lambda i: (0, 0)),
        ],
        out_specs=[pl.BlockSpec((bs * t, d), lambda i: (i, 0)),
                   pl.BlockSpec((bs, state, d), lambda i: (i, 0, 0))],
        out_shape=[jax.ShapeDtypeStruct((b * t, d), F32), jax.ShapeDtypeStruct((b, state, d), F32)],
        scratch_shapes=[pltpu.VMEM((bs, t, d), F32), pltpu.VMEM((bs, t, d), F32)],
        compiler_params=_cparams(("arbitrary",)),
        name="pool_sample",
    )(hn, prev, w_in, w_grp, scale)


def _layernorm_silu(c, g, b):
    mu = jnp.mean(c, axis=-1, keepdims=True)
    xc = c - mu
    var = jnp.mean(xc * xc, axis=-1, keepdims=True)
    n = xc * lax.rsqrt(var + LN_EPS) * g + b
    return n * _sigmoid(n)


def _conv_prompt_kernel(hn_ref, w1_ref, b1_ref, wdw_ref, bdw_ref, lg_ref, lb_ref, w2_ref, b2_ref,
                        mo_ref, st_ref, ext_ref, c_ref, *, tm, carry, width, rc):
    j = pl.program_id(1)
    nj = pl.num_programs(1)
    d = hn_ref.shape[1]
    a = _dot(hn_ref[...], w1_ref[...]) + b1_ref[...]
    glu = a[:, :d] * _sigmoid(a[:, d:])

    length = carry + tm

    @pl.when(j == 0)
    def _():
        ext_ref[0, 0:carry, :] = jnp.zeros((carry, d), F32)

    @pl.when(j > 0)
    def _():
        ext_ref[0, 0:carry, :] = ext_ref[0, tm:tm + carry, :]

    ext_ref[0, carry:carry + tm, :] = glu
    for m in range(1, V7X_SUBLANES):
        ext_ref[m, 0:length - V7X_SUBLANES, :] = ext_ref[0, m:length - V7X_SUBLANES + m, :]
    off = carry - (width - 1)

    def chunk(ci, _):
        r0 = pl.multiple_of(ci * rc, rc)
        acc = jnp.zeros((rc, d), F32) + bdw_ref[...]
        for k in range(width):
            q, m = divmod(off + k, V7X_SUBLANES)
            acc = acc + ext_ref[m, pl.ds(r0 + q * V7X_SUBLANES, rc), :] * wdw_ref[k:k + 1, :]
        c_ref[pl.ds(r0, rc), :] = acc
        return 0

    lax.fori_loop(0, tm // rc, chunk, 0)
    s = _layernorm_silu(c_ref[...], lg_ref[...], lb_ref[...])
    mo_ref[...] = _dot(s.astype(BF16), w2_ref[...]) + b2_ref[...]

    @pl.when(j == nj - 1)
    def _():
        st_ref[0] = ext_ref[0, carry + tm - (width - 1):carry + tm, :]


def _conv_prompt(hn, w1, b1, wdw, bdw, lg, lb, w2, b2, b, t, tm):
    d = hn.shape[1]
    width = wdw.shape[0]
    carry = 32
    ntj = t // tm
    const = lambda bi, j: (0, 0)
    return pl.pallas_call(
        functools.partial(_conv_prompt_kernel, tm=tm, carry=carry, width=width, rc=16),
        grid=(b, ntj),
        in_specs=[
            pl.BlockSpec((tm, d), lambda bi, j: (bi * ntj + j, 0)),
            pl.BlockSpec((d, 2 * d), const), pl.BlockSpec((1, 2 * d), const),
            pl.BlockSpec((width, d), const), pl.BlockSpec((1, d), const),
            pl.BlockSpec((1, d), const), pl.BlockSpec((1, d), const),
            pl.BlockSpec((d, d), const), pl.BlockSpec((1, d), const),
        ],
        out_specs=[pl.BlockSpec((tm, d), lambda bi, j: (bi * ntj + j, 0)),
                   pl.BlockSpec((1, width - 1, d), lambda bi, j: (bi, 0, 0))],
        out_shape=[jax.ShapeDtypeStruct((b * t, d), F32),
                   jax.ShapeDtypeStruct((b, width - 1, d), F32)],
        scratch_shapes=[pltpu.VMEM((V7X_SUBLANES, carry + tm, d), F32), pltpu.VMEM((tm, d), F32)],
        compiler_params=_cparams(("arbitrary", "arbitrary")),
        name="conv_prompt",
    )(hn, w1, b1, wdw, bdw, lg, lb, w2, b2)


def _conv_sample_kernel(hn_ref, prev_ref, w1_ref, b1_ref, wdw_ref, bdw_ref, lg_ref, lb_ref, w2_ref,
                        b2_ref, mo_ref, st_ref, g_ref, c_ref, *, bs, t, width):
    d = hn_ref.shape[1]
    state = width - 1
    a = _dot(hn_ref[...], w1_ref[...]) + b1_ref[...]
    glu = a[:, :d] * _sigmoid(a[:, d:])
    g_ref[...] = glu.reshape(bs, t, d)

    def row(i):
        if i >= 0:
            return g_ref[:, i, :]
        return prev_ref[:, state + i, :]

    for ti in range(t):
        acc = jnp.zeros((bs, d), F32) + bdw_ref[...]
        for k in range(width):
            acc = acc + row(ti + k - state) * wdw_ref[k:k + 1, :]
        c_ref[:, ti, :] = acc
    s = _layernorm_silu(c_ref[...].reshape(bs * t, d), lg_ref[...], lb_ref[...])
    mo_ref[...] = _dot(s.astype(BF16), w2_ref[...]) + b2_ref[...]
    if t >= state:
        st_ref[...] = g_ref[:, t - state:t, :]
    else:
        st_ref[:, 0:state - t, :] = prev_ref[:, t:state, :]
        st_ref[:, state - t:state, :] = g_ref[...]


def _conv_sample(hn, row0, prev, w1, b1, wdw, bdw, lg, lb, w2, b2, b, t):
    d = hn.shape[1]
    width = wdw.shape[0]
    bs = SAMPLE_SEQ_BLOCK
    blk0 = row0 // (bs * t)
    const = lambda i: (0, 0)
    return pl.pallas_call(
        functools.partial(_conv_sample_kernel, bs=bs, t=t, width=width),
        grid=(b // bs,),
        in_specs=[
            pl.BlockSpec((bs * t, d), lambda i: (blk0 + i, 0)),
            pl.BlockSpec((bs, width - 1, d), lambda i: (i, 0, 0)),
            pl.BlockSpec((d, 2 * d), const), pl.BlockSpec((1, 2 * d), const),
            pl.BlockSpec((width, d), const), pl.BlockSpec((1, d), const),
            pl.BlockSpec((1, d), const), pl.BlockSpec((1, d), const),
            pl.BlockSpec((d, d), const), pl.BlockSpec((1, d), const),
        ],
        out_specs=[pl.BlockSpec((bs * t, d), lambda i: (i, 0)),
                   pl.BlockSpec((bs, width - 1, d), lambda i: (i, 0, 0))],
        out_shape=[jax.ShapeDtypeStruct((b * t, d), F32),
                   jax.ShapeDtypeStruct((b, width - 1, d), F32)],
        scratch_shapes=[pltpu.VMEM((bs, t, d), F32), pltpu.VMEM((bs, t, d), F32)],
        compiler_params=_cparams(("arbitrary",)),
        name="conv_sample",
    )(hn, prev, w1, b1, wdw, bdw, lg, lb, w2, b2)


def _mla_qkv_kernel(hn_ref, wd_ref, qn_ref, kvn_ref, wuq_ref, wuk_ref, cos_ref, sin_ref,
                    ckv_ref, kpe_ref, kcat_ref, qcat_ref, *, q_lora, kv_lora, rope, nh, nope, scale):
    lanes = V7X_LANES
    qk = kv_lora + lanes
    dq = _dot(hn_ref[...], wd_ref[...])
    cq = _rms(dq[:, :q_lora], qn_ref[...]).astype(BF16)
    ckv = _rms(dq[:, q_lora:q_lora + kv_lora], kvn_ref[...])
    o = q_lora + kv_lora
    cos = cos_ref[...]
    sin = sin_ref[...]
    kblk = dq[:, o:o + lanes] * cos + dq[:, o + lanes:o + 2 * lanes] * sin
    ckv_ref[...] = ckv
    kpe_ref[...] = kblk[:, :rope]
    kcat_ref[:, :kv_lora] = ckv.astype(BF16)
    kcat_ref[:, kv_lora:] = kblk.astype(BF16)
    q = _dot(cq, wuq_ref[...])
    qn = nh * nope
    for h in range(nh):
        ql = _dot(q[:, h * nope:(h + 1) * nope].astype(BF16), wuk_ref[h])
        r0 = qn + h * lanes
        r1 = qn + (nh + h) * lanes
        qpe = q[:, r0:r0 + lanes] * cos + q[:, r1:r1 + lanes] * sin
        qcat_ref[:, h * qk:h * qk + kv_lora] = (ql * scale).astype(qcat_ref.dtype)
        qcat_ref[:, h * qk + kv_lora:(h + 1) * qk] = (qpe * scale).astype(qcat_ref.dtype)


def _mla_qkv(hn, blk0, nrows, wd, qn, kvn, wuq, wuk, cos, sin, cos_map, tm, dims, q_dtype):
    q_lora, kv_lora, rope, nh, nope, scale = dims
    d = hn.shape[1]
    qk = kv_lora + V7X_LANES
    const2 = lambda i: (0, 0)
    return pl.pallas_call(
        functools.partial(_mla_qkv_kernel, q_lora=q_lora, kv_lora=kv_lora, rope=rope, nh=nh,
                          nope=nope, scale=scale),
        grid=(nrows // tm,),
        in_specs=[
            pl.BlockSpec((tm, d), lambda i: (blk0 + i, 0)),
            pl.BlockSpec(wd.shape, const2), pl.BlockSpec((1, q_lora), const2),
            pl.BlockSpec((1, kv_lora), const2), pl.BlockSpec(wuq.shape, const2),
            pl.BlockSpec(wuk.shape, lambda i: (0, 0, 0)),
            pl.BlockSpec((tm, V7X_LANES), cos_map), pl.BlockSpec((tm, V7X_LANES), cos_map),
        ],
        out_specs=[pl.BlockSpec((tm, kv_lora), lambda i: (i, 0)),
                   pl.BlockSpec((tm, rope), lambda i: (i, 0)),
                   pl.BlockSpec((tm, qk), lambda i: (i, 0)),
                   pl.BlockSpec((tm, nh * qk), lambda i: (i, 0))],
        out_shape=[jax.ShapeDtypeStruct((nrows, kv_lora), F32),
                   jax.ShapeDtypeStruct((nrows, rope), F32),
                   jax.ShapeDtypeStruct((nrows, qk), BF16),
                   jax.ShapeDtypeStruct((nrows, nh * qk), q_dtype)],
        compiler_params=_cparams(("arbitrary",)),
        name="mla_qkv",
    )(hn, wd, qn, kvn, wuq, wuk, cos, sin)


def _softmax_update(s, v, m_ref, l_ref, acc_ref):
    m_old = m_ref[...]
    m_new = jnp.maximum(m_old, jnp.max(s, axis=1, keepdims=True))
    alpha = jnp.exp(m_old - m_new)
    p = jnp.exp(s - m_new)
    l_ref[...] = alpha * l_ref[...] + jnp.sum(p, axis=1, keepdims=True)
    acc_ref[...] = alpha * acc_ref[...] + _dot(p.astype(BF16), v)
    m_ref[...] = m_new


def _attn_prompt_kernel(q_ref, k_ref, o_ref, m_ref, l_ref, acc_ref, *, tq, tk, nh, qk, kv):
    i = pl.program_id(1)
    m_ref[...] = jnp.full(m_ref.shape, NEG_INF, F32)
    l_ref[...] = jnp.zeros(l_ref.shape, F32)
    acc_ref[...] = jnp.zeros(acc_ref.shape, F32)
    qpos = i * tq + lax.broadcasted_iota(jnp.int32, (tq, 1), 0)
    n_full = (i * tq + 1) // tk
    n_chunks = (i * tq + tq - 1) // tk + 1

    def make_body(masked):
        def body(kk, c):
            ks = pl.multiple_of(kk * tk, tk)
            kc = k_ref[pl.ds(ks, tk), :]
            v = kc[:, :kv]
            if masked:
                vis = ks + lax.broadcasted_iota(jnp.int32, (tq, tk), 1) <= qpos
            for h in range(nh):
                s = _dot_t(q_ref[:, h * qk:(h + 1) * qk], kc)
                if masked:
                    s = jnp.where(vis, s, NEG_INF)
                _softmax_update(s, v, m_ref.at[h], l_ref.at[h], acc_ref.at[h])
            return c
        return body

    lax.fori_loop(0, n_full, make_body(False), 0)
    lax.fori_loop(n_full, n_chunks, make_body(True), 0)
    for h in range(nh):
        o_ref[:, h * kv:(h + 1) * kv] = (acc_ref[h] / l_ref[h]).astype(BF16)


def _attn_prompt(qcat, kcat, b, t, nh, kv):
    qk = kcat.shape[1]
    tq = min(ATTN_Q_TILE, t)
    tk = min(ATTN_K_TILE, t)
    nq = t // tq
    return pl.pallas_call(
        functools.partial(_attn_prompt_kernel, tq=tq, tk=tk, nh=nh, qk=qk, kv=kv),
        grid=(b, nq),
        in_specs=[
            pl.BlockSpec((tq, nh * qk), lambda bi, i: (bi * nq + i, 0)),
            pl.BlockSpec((t, qk), lambda bi, i: (bi, 0)),
        ],
        out_specs=pl.BlockSpec((tq, nh * kv), lambda bi, i: (bi * nq + i, 0)),
        out_shape=jax.ShapeDtypeStruct((b * t, nh * kv), BF16),
        scratch_shapes=[pltpu.VMEM((nh, tq, 1), F32), pltpu.VMEM((nh, tq, 1), F32),
                        pltpu.VMEM((nh, tq, kv), F32)],
        compiler_params=_cparams(("arbitrary", "arbitrary")),
        name="mla_attn_prompt",
    )(qcat, kcat)


def _page_copies(pt_ref, lat_hbm, kr_hbm, lat_buf, kr_buf, sem, layer, npages, seq, slot):
    def body(p, c):
        idx = pt_ref[seq * npages + p]
        pltpu.make_async_copy(lat_hbm.at[layer, idx], lat_buf.at[slot, p], sem.at[0, slot]).start()
        pltpu.make_async_copy(kr_hbm.at[layer, idx], kr_buf.at[slot, p], sem.at[1, slot]).start()
        return c

    lax.fori_loop(0, npages, body, 0, unroll=DMA_ISSUE_UNROLL)


def _attn_sample_kernel(pt_ref, q_ref, cn_ref, krn_ref, lat_hbm, kr_hbm, o_ref,
                        lat_buf, kr_buf, sem, m_ref, l_ref, acc_ref,
                        *, layer, npages, chunk, t, nh, qk, kv, rope, page):
    bi = pl.program_id(0)
    nb = pl.num_programs(0)
    slot = bi % 2
    fetch = functools.partial(_page_copies, pt_ref, lat_hbm, kr_hbm, lat_buf, kr_buf, sem, layer, npages)

    @pl.when(bi == 0)
    def _():
        fetch(0, 0)

    @pl.when(bi + 1 < nb)
    def _():
        fetch(bi + 1, 1 - slot)

    m_ref[...] = jnp.full(m_ref.shape, NEG_INF, F32)
    l_ref[...] = jnp.zeros(l_ref.shape, F32)
    acc_ref[...] = jnp.zeros(acc_ref.shape, F32)
    q = jnp.concatenate([q_ref[:, h * qk:(h + 1) * qk] for h in range(nh)], axis=0).astype(BF16)
    ql = q[:, :kv]
    qp = q[:, kv:kv + rope]

    pltpu.make_async_copy(lat_hbm.at[layer, pl.ds(0, npages)], lat_buf.at[slot], sem.at[0, slot]).wait()
    pltpu.make_async_copy(kr_hbm.at[layer, pl.ds(0, npages)], kr_buf.at[slot], sem.at[1, slot]).wait()

    for ch in range(npages // chunk):
        c = lat_buf[slot, ch * chunk:(ch + 1) * chunk].reshape(chunk * page, kv).astype(BF16)
        krt = jnp.concatenate([kr_buf[slot, ch * chunk + p] for p in range(chunk)], axis=1).astype(BF16)
        s = _dot_t(ql, c) + _dot(qp, krt)
        _softmax_update(s, c, m_ref, l_ref, acc_ref)

    rows = nh * t
    cn = jnp.concatenate([cn_ref[...], jnp.zeros((page - t, kv), F32)], axis=0).astype(BF16)
    krn = jnp.concatenate([krn_ref[...], jnp.zeros((page - t, rope), F32)], axis=0).astype(BF16)
    sn = _dot_t(ql, cn) + _dot_t(qp, krn)
    qstep = lax.broadcasted_iota(jnp.int32, (rows, 1), 0) & (t - 1)
    kstep = lax.broadcasted_iota(jnp.int32, (rows, page), 1)
    sn = jnp.where(kstep <= qstep, sn, NEG_INF)
    _softmax_update(sn, cn, m_ref, l_ref, acc_ref)
    o = acc_ref[...] / l_ref[...]
    for h in range(nh):
        o_ref[:, h * kv:(h + 1) * kv] = o[h * t:(h + 1) * t, :]


def _attn_sample(page_table, qcat, cn, krn, cache_lat, cache_kr_t, layer, b, t, nh):
    kv = cn.shape[1]
    rope = krn.shape[1]
    qk = qcat.shape[1] // nh
    page = cache_lat.shape[2]
    npages = page_table.shape[1]
    chunk = min(DECODE_PAGE_CHUNK, npages)
    assert npages % chunk == 0 and t % V7X_SUBLANES == 0 and t <= page and t & (t - 1) == 0
    rows = t * nh
    grid_spec = pltpu.PrefetchScalarGridSpec(
        num_scalar_prefetch=1,
        grid=(b,),
        in_specs=[
            pl.BlockSpec((t, nh * qk), lambda bi, ptr: (bi, 0)),
            pl.BlockSpec((t, kv), lambda bi, ptr: (bi, 0)),
            pl.BlockSpec((t, rope), lambda bi, ptr: (bi, 0)),
            pl.BlockSpec(memory_space=pl.ANY),
            pl.BlockSpec(memory_space=pl.ANY),
        ],
        out_specs=pl.BlockSpec((t, nh * kv), lambda bi, ptr: (bi, 0)),
        scratch_shapes=[
            pltpu.VMEM((2, npages, page, kv), F32),
            pltpu.VMEM((2, npages, rope, page), F32),
            pltpu.SemaphoreType.DMA((2, 2)),
            pltpu.VMEM((rows, 1), F32), pltpu.VMEM((rows, 1), F32), pltpu.VMEM((rows, kv), F32),
        ],
    )
    return pl.pallas_call(
        functools.partial(_attn_sample_kernel, layer=layer, npages=npages, chunk=chunk, t=t, nh=nh,
                          qk=qk, kv=kv, rope=rope, page=page),
        grid_spec=grid_spec,
        out_shape=jax.ShapeDtypeStruct((b * t, nh * kv), F32),
        compiler_params=_cparams(("arbitrary",)),
        name="mla_attn_sample",
    )(page_table.reshape(-1), qcat, cn, krn, cache_lat, cache_kr_t)


def _mla_out_kernel(o_ref, wuv_ref, wo_ref, mo_ref, *, nh, kv):
    vs = [_dot(o_ref[:, h * kv:(h + 1) * kv].astype(BF16), wuv_ref[h]).astype(BF16) for h in range(nh)]
    mo_ref[...] = _dot(jnp.concatenate(vs, axis=1), wo_ref[...])


def _mla_out(o, wuv, wo, tm):
    n = o.shape[0]
    nh, kv, vd = wuv.shape
    d = wo.shape[1]
    return pl.pallas_call(
        functools.partial(_mla_out_kernel, nh=nh, kv=kv),
        grid=(n // tm,),
        in_specs=[pl.BlockSpec((tm, nh * kv), lambda i: (i, 0)),
                  pl.BlockSpec(wuv.shape, lambda i: (0, 0, 0)),
                  pl.BlockSpec(wo.shape, lambda i: (0, 0))],
        out_specs=pl.BlockSpec((tm, d), lambda i: (i, 0)),
        out_shape=jax.ShapeDtypeStruct((n, d), F32),
        compiler_params=_cparams(("arbitrary",)),
        name="mla_out",
    )(o, wuv, wo)


def _route_kernel(h_ref, mop_ref, mos_ref, g_ref, whi_ref, wlo_ref, br_ref,
                  h1_ref, xn_ref, slab_ref, cnt_ref, carry_ref, *, npt, tm, n_groups, epg):
    i = pl.program_id(0)

    @pl.when(i == 0)
    def _():
        carry_ref[...] = jnp.zeros(carry_ref.shape, F32)

    h1 = h_ref[...] + jnp.where(i < npt, mop_ref[...], mos_ref[...])
    h1_ref[...] = h1
    xn = _rms(h1, g_ref[...])
    xn_ref[...] = xn
    x_hi = xn.astype(BF16)
    x_lo = (xn - x_hi.astype(F32)).astype(BF16)
    whi = whi_ref[...]
    logits = _dot(x_hi, whi) + _dot(x_lo, whi) + _dot(x_hi, wlo_ref[...]) + br_ref[...]

    lane = lax.broadcasted_iota(jnp.int32, (tm, V7X_LANES), 1).astype(F32)
    big = float(V7X_LANES)
    gl = jnp.where(lane < n_groups, logits, NEG_INF)
    gmax = jnp.max(gl, axis=1, keepdims=True)
    gsel = jnp.min(jnp.where(gl == gmax, lane, big), axis=1, keepdims=True)
    gprob = 1.0 / jnp.sum(jnp.exp(gl - gmax), axis=1, keepdims=True)
    lo = n_groups + gsel * epg
    el = jnp.where(lane >= lo, jnp.where(lane < lo + epg, logits, NEG_INF), NEG_INF)
    m1 = jnp.max(el, axis=1, keepdims=True)
    i1 = jnp.min(jnp.where(el == m1, lane, big), axis=1, keepdims=True)
    el2 = jnp.where(lane == i1, NEG_INF, el)
    m2 = jnp.max(el2, axis=1, keepdims=True)
    i2 = jnp.min(jnp.where(el2 == m2, lane, big), axis=1, keepdims=True)
    r = jnp.exp(m2 - m1)
    g1 = gprob / (1.0 + r)
    g2 = gprob * r / (1.0 + r)

    oh1 = jnp.where(lane == i1, 1.0, 0.0)
    oh2 = jnp.where(lane == i2, 1.0, 0.0)
    oh = oh1 + oh2
    rr = lax.broadcasted_iota(jnp.int32, (tm, tm), 0)
    cc = lax.broadcasted_iota(jnp.int32, (tm, tm), 1)
    tri = jnp.where(rr > cc, 1.0, 0.0).astype(BF16)
    carry = carry_ref[0:1, :]
    cum = _dot(tri, oh.astype(BF16)) + carry
    rank1 = jnp.sum(oh1 * cum, axis=1, keepdims=True)
    rank2 = jnp.sum(oh2 * cum, axis=1, keepdims=True)
    new_carry = carry + jnp.sum(oh, axis=0, keepdims=True)
    carry_ref[...] = jnp.broadcast_to(new_carry, carry_ref.shape)
    cnt_ref[...] = jnp.broadcast_to(new_carry, cnt_ref.shape)

    slab = jnp.where(lane == 0, i1 - n_groups, 0.0)
    slab = jnp.where(lane == 1, i2 - n_groups, slab)
    slab = jnp.where(lane == 2, g1, slab)
    slab = jnp.where(lane == 3, g2, slab)
    slab = jnp.where(lane == 4, rank1, slab)
    slab = jnp.where(lane == 5, rank2, slab)
    slab_ref[...] = slab


def _route(h, mo_p, mo_s, g, whi, wlo, br, tm, n_groups, epg):
    n, d = h.shape
    npt = mo_p.shape[0] // tm
    const = lambda i: (0, 0)
    return pl.pallas_call(
        functools.partial(_route_kernel, npt=npt, tm=tm, n_groups=n_groups, epg=epg),
        grid=(n // tm,),
        in_specs=[
            pl.BlockSpec((tm, d), lambda i: (i, 0)),
            pl.BlockSpec((tm, d), lambda i: (jnp.minimum(i, npt - 1), 0)),
            pl.BlockSpec((tm, d), lambda i: (jnp.maximum(i - npt, 0), 0)),
            pl.BlockSpec((1, d), const),
            pl.BlockSpec((d, V7X_LANES), const), pl.BlockSpec((d, V7X_LANES), const),
            pl.BlockSpec((1, V7X_LANES), const),
        ],
        out_specs=[pl.BlockSpec((tm, d), lambda i: (i, 0)),
                   pl.BlockSpec((tm, d), lambda i: (i, 0)),
                   pl.BlockSpec((tm, V7X_LANES), lambda i: (i, 0)),
                   pl.BlockSpec((V7X_SUBLANES, V7X_LANES), const)],
        out_shape=[jax.ShapeDtypeStruct((n, d), F32), jax.ShapeDtypeStruct((n, d), F32),
                   jax.ShapeDtypeStruct((n, V7X_LANES), F32),
                   jax.ShapeDtypeStruct((V7X_SUBLANES, V7X_LANES), F32)],
        scratch_shapes=[pltpu.VMEM((V7X_SUBLANES, V7X_LANES), F32)],
        compiler_params=_cparams(("arbitrary",)),
        name="route",
    )(h, mo_p, mo_s, g, whi, wlo, br)


def _dispatch_tables(slab, cnt, n_experts, n_groups, tm, n_blk):
    counts = cnt[0, n_groups:n_groups + n_experts].astype(jnp.int32)
    padded = (counts + tm - 1) // tm * tm
    pad_ends = jnp.cumsum(padded)
    pad_starts = pad_ends - padded
    eids = jnp.arange(n_experts, dtype=jnp.int32)

    def dest(k):
        e = slab[:, k].astype(jnp.int32)
        start = jnp.sum(jnp.where(e[:, None] == eids[None, :], pad_starts[None, :], 0), axis=1)
        return start + slab[:, 4 + k].astype(jnp.int32)

    blk_start = jnp.arange(n_blk, dtype=jnp.int32) * tm
    blk_e = jnp.sum((blk_start[:, None] >= pad_ends[None, :]).astype(jnp.int32), axis=1)
    blk_e = jnp.minimum(blk_e, n_experts - 1)
    nblk = pad_ends[-1:] // tm
    return dest(0), dest(1), blk_e, nblk


def _dispatch_kernel(d0_ref, d1_ref, xn_ref, xs_in, xs_out, sem, *, tm):
    del xs_in
    i = pl.program_id(0)
    base = i * tm

    def body(r, c):
        src = xn_ref.at[pl.ds(r, 1)]
        pltpu.make_async_copy(src, xs_out.at[pl.ds(d0_ref[base + r], 1)], sem).start()
        pltpu.make_async_copy(src, xs_out.at[pl.ds(d1_ref[base + r], 1)], sem).start()
        return c

    lax.fori_loop(0, tm, body, 0, unroll=DMA_ISSUE_UNROLL)
    for _ in range(TOP_K):
        pltpu.make_async_copy(xn_ref, xs_out.at[pl.ds(0, tm)], sem).wait()


def _dispatch(d0, d1, xn, xs_zero, tm):
    n, d = xn.shape
    grid_spec = pltpu.PrefetchScalarGridSpec(
        num_scalar_prefetch=2,
        grid=(n // tm,),
        in_specs=[pl.BlockSpec((tm, d), lambda i, a, b: (i, 0)),
                  pl.BlockSpec(memory_space=pl.ANY)],
        out_specs=pl.BlockSpec(memory_space=pl.ANY),
        scratch_shapes=[pltpu.SemaphoreType.DMA],
    )
    return pl.pallas_call(
        functools.partial(_dispatch_kernel, tm=tm),
        grid_spec=grid_spec,
        out_shape=jax.ShapeDtypeStruct(xs_zero.shape, F32),
        input_output_aliases={3: 0},
        compiler_params=_cparams(("arbitrary",)),
        name="moe_dispatch",
    )(d0, d1, xn, xs_zero)


def _experts_kernel(be_ref, nb_ref, xs_ref, w1_ref, w3_ref, w2_ref, y_ref, w1b, w3b, w2b):
    i = pl.program_id(0)

    @pl.when(i < nb_ref[0])
    def _():
        e = be_ref[i]
        e_prev = be_ref[jnp.maximum(i - 1, 0)]

        @pl.when((i == 0) | (e != e_prev))
        def _():
            w1b[...] = w1_ref[...].astype(BF16)
            w3b[...] = w3_ref[...].astype(BF16)
            w2b[...] = w2_ref[...].astype(BF16)

        x = xs_ref[...].astype(BF16)
        a = _dot(x, w1b[...])
        hmid = (a * _sigmoid(a)) * _dot(x, w3b[...])
        y_ref[...] = _dot(hmid.astype(BF16), w2b[...])


def _experts(blk_e, nblk, xs, w1, w3, w2, layer, tm):
    n_rows, d = xs.shape
    f = w1.shape[-1]
    wmap = lambda i, be, nb: (layer, be[i], 0, 0)
    xmap = lambda i, be, nb: (jnp.minimum(i, nb[0] - 1), 0)
    grid_spec = pltpu.PrefetchScalarGridSpec(
        num_scalar_prefetch=2,
        grid=(n_rows // tm,),
        in_specs=[
            pl.BlockSpec((tm, d), xmap),
            pl.BlockSpec((None, None, d, f), wmap),
            pl.BlockSpec((None, None, d, f), wmap),
            pl.BlockSpec((None, None, f, d), wmap),
        ],
        out_specs=pl.BlockSpec((tm, d), xmap),
        scratch_shapes=[pltpu.VMEM((d, f), BF16), pltpu.VMEM((d, f), BF16), pltpu.VMEM((f, d), BF16)],
    )
    return pl.pallas_call(
        _experts_kernel,
        grid_spec=grid_spec,
        out_shape=jax.ShapeDtypeStruct((n_rows, d), F32),
        input_output_aliases={2: 0},
        compiler_params=_cparams(("arbitrary",)),
        name="moe_experts",
    )(blk_e, nblk, xs, w1, w3, w2)


def _combine_copies(d0_ref, d1_ref, y_hbm, ybuf, sem, tm, tile, slot):
    base = tile * tm

    def body(r, c):
        pltpu.make_async_copy(y_hbm.at[pl.ds(d0_ref[base + r], 1)], ybuf.at[slot, 0, pl.ds(r, 1)],
                              sem.at[slot]).start()
        pltpu.make_async_copy(y_hbm.at[pl.ds(d1_ref[base + r], 1)], ybuf.at[slot, 1, pl.ds(r, 1)],
                              sem.at[slot]).start()
        return c

    lax.fori_loop(0, tm, body, 0, unroll=DMA_ISSUE_UNROLL)


def _ple_kernel(d0_ref, d1_ref, h1_ref, y_hbm, slab_ref, pp_ref, ps_ref, gple_ref, wg_ref, wp_ref,
                gn_ref, *rest, npt, last, tm):
    ybuf, sem = rest[-2:]
    outs = rest[:-2]
    i = pl.program_id(0)
    nt = pl.num_programs(0)
    slot = i % 2
    fetch = functools.partial(_combine_copies, d0_ref, d1_ref, y_hbm, ybuf, sem, tm)

    @pl.when(i == 0)
    def _():
        fetch(0, 0)

    @pl.when(i + 1 < nt)
    def _():
        fetch(i + 1, 1 - slot)

    for k in range(TOP_K):
        pltpu.make_async_copy(y_hbm.at[pl.ds(0, tm)], ybuf.at[slot, k], sem.at[slot]).wait()

    slab = slab_ref[...]
    h2 = h1_ref[...] + slab[:, 2:3] * ybuf[slot, 0] + slab[:, 3:4] * ybuf[slot, 1]
    gate = _sigmoid(_dot(_rms(h2, gple_ref[...]).astype(BF16), wg_ref[...]))
    p = jnp.where(i < npt, pp_ref[...], ps_ref[...]).astype(BF16)
    h3 = h2 + gate * _dot(p, wp_ref[...])
    hn = _rms(h3, gn_ref[...])
    if last:
        yp_ref, ys_ref = outs

        @pl.when(i < npt)
        def _():
            yp_ref[...] = hn

        @pl.when(i >= npt)
        def _():
            ys_ref[...] = hn
    else:
        h_ref, hn_ref = outs
        h_ref[...] = h3
        hn_ref[...] = hn.astype(BF16)


def _ple(d0, d1, h1, y, slab, p_p, p_s, layer, gple, wg, wp, gn, tm, last):
    n, d = h1.shape
    np_ = p_p.shape[1]
    npt = np_ // tm
    pd = p_p.shape[2]
    const = lambda i, a, b: (0, 0)
    rowmap = lambda i, a, b: (i, 0)
    if last:
        out_specs = [pl.BlockSpec((tm, d), lambda i, a, b: (jnp.minimum(i, npt - 1), 0)),
                     pl.BlockSpec((tm, d), lambda i, a, b: (jnp.maximum(i - npt, 0), 0))]
        out_shape = [jax.ShapeDtypeStruct((np_, d), F32), jax.ShapeDtypeStruct((n - np_, d), F32)]
    else:
        out_specs = [pl.BlockSpec((tm, d), rowmap), pl.BlockSpec((tm, d), rowmap)]
        out_shape = [jax.ShapeDtypeStruct((n, d), F32), jax.ShapeDtypeStruct((n, d), BF16)]
    grid_spec = pltpu.PrefetchScalarGridSpec(
        num_scalar_prefetch=2,
        grid=(n // tm,),
        in_specs=[
            pl.BlockSpec((tm, d), rowmap),
            pl.BlockSpec(memory_space=pl.ANY),
            pl.BlockSpec((tm, V7X_LANES), rowmap),
            pl.BlockSpec((None, tm, pd), lambda i, a, b: (layer, jnp.minimum(i, npt - 1), 0)),
            pl.BlockSpec((None, tm, pd), lambda i, a, b: (layer, jnp.maximum(i - npt, 0), 0)),
            pl.BlockSpec((1, d), const), pl.BlockSpec((d, d), const), pl.BlockSpec((pd, d), const),
            pl.BlockSpec((1, d), const),
        ],
        out_specs=out_specs,
        scratch_shapes=[pltpu.VMEM((2, TOP_K, tm, d), F32), pltpu.SemaphoreType.DMA((2,))],
    )
    return pl.pallas_call(
        functools.partial(_ple_kernel, npt=npt, last=last, tm=tm),
        grid_spec=grid_spec,
        out_shape=out_shape,
        compiler_params=_cparams(("arbitrary",)),
        name="ple",
    )(d0, d1, h1, y, slab, p_p, p_s, gple, wg, wp, gn)


def _rope_tables(pos, rope):
    half = rope // 2
    inv = 1.0 / (ROPE_THETA ** (jnp.arange(half, dtype=F32) * (2.0 / rope)))
    ang = pos.astype(F32)[:, None] * inv[None, :]
    pad = ((0, 0), (0, V7X_LANES - rope))
    cos = jnp.pad(jnp.tile(jnp.cos(ang), (1, 2)), pad)
    sin = jnp.pad(jnp.tile(jnp.sin(ang), (1, 2)), pad)
    return cos, sin


def _rot_last(w):
    half = w.shape[-1] // 2
    return jnp.concatenate([-w[..., half:], w[..., :half]], axis=-1)


def _pad_last(w, width):
    return jnp.pad(w, [(0, 0)] * (w.ndim - 1) + [(0, width - w.shape[-1])])


def kernel(x_prompt, x_sample, state_pool, state_conv, cache_latent, cache_krope, page_table, p_prompt, p_sample, norm_mix, norm_ffn, norm_ple, norm_final, pool_w_in, pool_w_grp, pool_scale, conv_w_pw1, conv_b_pw1, conv_w_dw, conv_b_dw, conv_ln_g, conv_ln_b, conv_w_pw2, conv_b_pw2, mla_w_dqkv, mla_q_norm, mla_kv_norm, mla_w_uq, mla_w_uk, mla_w_uv, mla_w_o, moe_w_group, moe_b_group, moe_w_expert, moe_b_expert, moe_w1, moe_w3, moe_w2, ple_w_gate, ple_w_proj):
    b, t, d = x_prompt.shape
    bs, ts, _ = x_sample.shape
    depth = norm_mix.shape[0]
    np_, ns = b * t, bs * ts
    n = np_ + ns
    tm = min(TOKEN_TILE, ns)
    assert np_ % tm == 0 and ns % tm == 0
    past_len = page_table.shape[1] * cache_latent.shape[2]
    n_groups = moe_w_group.shape[-1]
    n_experts = moe_w_expert.shape[-1]
    epg = n_experts // n_groups
    assert n_groups + n_experts <= V7X_LANES
    n_rows = (n * TOP_K + n_experts * (MOE_TILE - 1) + MOE_TILE - 1) // MOE_TILE * MOE_TILE
    pd = p_prompt.shape[-1]
    row = lambda v: v.reshape(1, -1)

    nh, kv_lora, nope = mla_w_uk.shape[1:]
    rope = cache_krope.shape[-1]
    q_lora = mla_w_dqkv.shape[-1] - kv_lora - rope
    scale = float((nope + rope) ** -0.5)
    mla_dims = (q_lora, kv_lora, rope, nh, nope, scale)
    cache_kr_t = jnp.swapaxes(cache_krope, 2, 3)

    h, hn = _prep(x_prompt.reshape(np_, d), x_sample.reshape(ns, d), row(norm_mix[0]), tm)
    p_p = p_prompt.reshape(depth, np_, pd)
    p_s = p_sample.reshape(depth, ns, pd)

    new_pool_p, new_pool_s, new_conv_p, new_conv_s = [], [], [], []
    new_lat_p, new_kr_p, new_lat_s, new_kr_s = [], [], [], []
    y_p = y_s = None
    for i in range(depth):
        kind, j = i % N_MIXERS, i // N_MIXERS
        if kind == 0:
            w_in = pool_w_in[j].astype(BF16)
            w_grp = pool_w_grp[j].astype(BF16)
            sc = row(pool_scale[j])
            mo_p, st_p = _pool_prompt(hn, w_in, w_grp, sc, b, t, min(tm, t))
            mo_s, st_s = _pool_sample(hn, np_, state_pool[j], w_in, w_grp, sc, bs, ts, past_len)
            new_pool_p.append(st_p)
            new_pool_s.append(st_s)
        elif kind == 1:
            w1 = conv_w_pw1[j].astype(BF16)
            w2 = conv_w_pw2[j].astype(BF16)
            args = (w1, row(conv_b_pw1[j]), conv_w_dw[j], row(conv_b_dw[j]), row(conv_ln_g[j]),
                    row(conv_ln_b[j]), w2, row(conv_b_pw2[j]))
            mo_p, st_p = _conv_prompt(hn, *args, b, t, min(CONV_TILE, t))
            mo_s, st_s = _conv_sample(hn, np_, state_conv[j], *args, bs, ts)
            new_conv_p.append(st_p)
            new_conv_s.append(st_s)
        else:
            wd = mla_w_dqkv[j]
            o = q_lora + kv_lora
            wd_k = wd[:, o:]
            wd = jnp.concatenate([wd[:, :o], _pad_last(wd_k, V7X_LANES),
                                  _pad_last(_rot_last(wd_k), V7X_LANES)], axis=1).astype(BF16)
            wuq = mla_w_uq[j].reshape(q_lora, nh, nope + rope)
            wq_rope = wuq[:, :, nope:]
            wuq = jnp.concatenate([
                wuq[:, :, :nope].reshape(q_lora, nh * nope),
                _pad_last(wq_rope, V7X_LANES).reshape(q_lora, nh * V7X_LANES),
                _pad_last(_rot_last(wq_rope), V7X_LANES).reshape(q_lora, nh * V7X_LANES)],
                axis=1).astype(BF16)
            wuk = jnp.swapaxes(mla_w_uk[j], 1, 2).astype(BF16)
            wuv = mla_w_uv[j].astype(BF16)
            wo = mla_w_o[j].astype(BF16)
            qn, kvn = row(mla_q_norm[j]), row(mla_kv_norm[j])
            tmp = min(tm, t)
            ntj = t // tmp
            cos_p, sin_p = _rope_tables(jnp.arange(t, dtype=jnp.int32), rope)
            ckv_p, kpe_p, kcat_p, qcat_p = _mla_qkv(
                hn, 0, np_, wd, qn, kvn, wuq, wuk, cos_p, sin_p, lambda ii: (ii % ntj, 0), tmp,
                mla_dims, BF16)
            cos_s, sin_s = _rope_tables(past_len + jnp.arange(ts, dtype=jnp.int32), rope)
            cos_s, sin_s = jnp.tile(cos_s, (bs, 1)), jnp.tile(sin_s, (bs, 1))
            ckv_s, kpe_s, _, qcat_s = _mla_qkv(
                hn, np_ // tm, ns, wd, qn, kvn, wuq, wuk, cos_s, sin_s, lambda ii: (ii, 0), tm,
                mla_dims, F32)
            o_p = _attn_prompt(qcat_p, kcat_p, b, t, nh, kv_lora)
            o_s = _attn_sample(page_table, qcat_s, ckv_s, kpe_s, cache_latent, cache_kr_t, j, bs, ts, nh)
            mo_p = _mla_out(o_p, wuv, wo, tm)
            mo_s = _mla_out(o_s, wuv, wo, tm)
            new_lat_p.append(ckv_p.reshape(b, t, kv_lora))
            new_kr_p.append(kpe_p.reshape(b, t, rope))
            new_lat_s.append(ckv_s.reshape(bs, ts, kv_lora))
            new_kr_s.append(kpe_s.reshape(bs, ts, rope))

        wr = _pad_last(jnp.concatenate([moe_w_group[i], moe_w_expert[i]], axis=1), V7X_LANES)
        br = _pad_last(jnp.concatenate([moe_b_group[i], moe_b_expert[i]])[None, :], V7X_LANES)
        whi = wr.astype(BF16)
        wlo = (wr - whi.astype(F32)).astype(BF16)
        h1, xn, slab, cnt = _route(h, mo_p, mo_s, row(norm_ffn[i]), whi, wlo, br, tm, n_groups, epg)
        d0, d1, blk_e, nblk = _dispatch_tables(slab, cnt, n_experts, n_groups, MOE_TILE, n_rows // MOE_TILE)
        xs = _dispatch(d0, d1, xn, jnp.zeros((n_rows, d), F32), tm)
        y = _experts(blk_e, nblk, xs, moe_w1, moe_w3, moe_w2, i, MOE_TILE)
        last = i == depth - 1
        gn = row(norm_final) if last else row(norm_mix[i + 1])
        out = _ple(d0, d1, h1, y, slab, p_p, p_s, i, row(norm_ple[i]), ple_w_gate[i].astype(BF16),
                   ple_w_proj[i].astype(BF16), gn, tm, last)
        if last:
            y_p, y_s = out
        else:
            h, hn = out

    return (y_p.reshape(b, t, d), y_s.reshape(bs, ts, d),
            jnp.stack(new_pool_p), jnp.stack(new_pool_s),
            jnp.stack(new_conv_p), jnp.stack(new_conv_s),
            jnp.stack(new_lat_p), jnp.stack(new_kr_p),
            jnp.stack(new_lat_s), jnp.stack(new_kr_s))
```

```python
import functools

import jax
import jax.numpy as jnp
from jax import lax
from jax.experimental import pallas as pl
from jax.experimental.pallas import tpu as pltpu

F32 = jnp.float32
BF16 = jnp.bfloat16

POOL_WINDOWS = (2, 4, 8, 16)
TOP_K = 2
ROPE_THETA = 10000.0
RMS_EPS = 1e-6
LN_EPS = 1e-5
NEG_INF = -1e30
N_MIXERS = 3

V7X_LANES = 128
V7X_SUBLANES = 8
V7X_VMEM_LIMIT_BYTES = 48 * 1024 * 1024

TOKEN_TILE = 512
MOE_TILE = 256
CONV_TILE = 256
ATTN_Q_TILE = 256
ATTN_K_TILE = 512
DECODE_PAGE_CHUNK = 8
SAMPLE_SEQ_BLOCK = 32
DMA_ISSUE_UNROLL = 8


def _cparams(sem):
    return pltpu.CompilerParams(dimension_semantics=sem, vmem_limit_bytes=V7X_VMEM_LIMIT_BYTES)


def _rms(x, g):
    ms = jnp.mean(x * x, axis=-1, keepdims=True)
    return x * lax.rsqrt(ms + RMS_EPS) * g


def _dot(a, b):
    return jnp.dot(a, b, preferred_element_type=F32)


def _dot_t(a, b):
    return lax.dot_general(a, b, (((1,), (1,)), ((), ())), preferred_element_type=F32)


def _sigmoid(x):
    return 1.0 / (1.0 + jnp.exp(-x))


def _pack_bf16_pairs(x):
    w = x.shape[1] // 2
    hi = lax.bitcast_convert_type(x[:, :w].astype(BF16).astype(F32), jnp.uint32)
    lo = lax.bitcast_convert_type(x[:, w:].astype(BF16).astype(F32), jnp.uint32)
    return hi | (lo >> 16)


def _unpack_bf16_pairs(p):
    hi = lax.bitcast_convert_type(p & jnp.uint32(0xFFFF0000), F32)
    lo = lax.bitcast_convert_type(p << 16, F32)
    return hi, lo


def _issue_rows(n_rows, issue):
    def group(g, c):
        for u in range(DMA_ISSUE_UNROLL):
            issue(g * DMA_ISSUE_UNROLL + u)
        return c

    lax.fori_loop(0, n_rows // DMA_ISSUE_UNROLL, group, 0)


def _prep_kernel(xp_ref, xs_ref, g_ref, h_ref, hn_ref, *, npt):
    i = pl.program_id(0)
    x = jnp.where(i < npt, xp_ref[...], xs_ref[...])
    h_ref[...] = x
    hn_ref[...] = _rms(x, g_ref[...]).astype(BF16)


def _prep(xp, xs, g, tm):
    np_, d = xp.shape
    ns = xs.shape[0]
    npt, nst = np_ // tm, ns // tm
    n = np_ + ns
    return pl.pallas_call(
        functools.partial(_prep_kernel, npt=npt),
        grid=(npt + nst,),
        in_specs=[
            pl.BlockSpec((tm, d), lambda i: (jnp.minimum(i, npt - 1), 0)),
            pl.BlockSpec((tm, d), lambda i: (jnp.maximum(i - npt, 0), 0)),
            pl.BlockSpec((1, d), lambda i: (0, 0)),
        ],
        out_specs=[pl.BlockSpec((tm, d), lambda i: (i, 0)),
                   pl.BlockSpec((tm, d), lambda i: (i, 0))],
        out_shape=[jax.ShapeDtypeStruct((n, d), F32), jax.ShapeDtypeStruct((n, d), BF16)],
        compiler_params=_cparams(("arbitrary",)),
        name="prep",
    )(xp, xs, g)


def _pool_prompt_kernel(hn_ref, win_ref, wg_ref, sc_ref, mo_ref, st_ref, ext_ref, *, tm, carry, state):
    j = pl.program_id(1)
    nj = pl.num_programs(1)
    d = hn_ref.shape[1]
    gd = d // len(POOL_WINDOWS)
    u = _dot(hn_ref[...], win_ref[...])

    @pl.when(j == 0)
    def _():
        ext_ref[0:carry, :] = jnp.zeros((carry, d), F32)

    @pl.when(j > 0)
    def _():
        ext_ref[0:carry, :] = ext_ref[tm:tm + carry, :]

    ext_ref[carry:carry + tm, :] = u
    pos = j * tm + lax.broadcasted_iota(jnp.int32, (tm, 1), 0)
    for g, w in enumerate(POOL_WINDOWS):
        c0, c1 = g * gd, (g + 1) * gd
        ug = u[:, c0:c1]
        s = ug
        for k in range(1, w):
            s = s + ext_ref[carry - k:carry - k + tm, c0:c1]
        cnt = jnp.minimum(pos + 1, w).astype(F32)
        z = s / cnt - ug
        y = _dot(z.astype(BF16), wg_ref[g])
        mo_ref[:, c0:c1] = y * sc_ref[:, c0:c1]

    @pl.when(j == nj - 1)
    def _():
        st_ref[0] = ext_ref[carry + tm - state:carry + tm, :]


def _pool_prompt(hn, w_in, w_grp, scale, b, t, tm):
    d = hn.shape[1]
    state = max(POOL_WINDOWS) - 1
    carry = 16
    ntj = t // tm
    return pl.pallas_call(
        functools.partial(_pool_prompt_kernel, tm=tm, carry=carry, state=state),
        grid=(b, ntj),
        in_specs=[
            pl.BlockSpec((tm, d), lambda bi, j: (bi * ntj + j, 0)),
            pl.BlockSpec((d, d), lambda bi, j: (0, 0)),
            pl.BlockSpec(w_grp.shape, lambda bi, j: (0, 0, 0)),
            pl.BlockSpec((1, d), lambda bi, j: (0, 0)),
        ],
        out_specs=[pl.BlockSpec((tm, d), lambda bi, j: (bi * ntj + j, 0)),
                   pl.BlockSpec((1, state, d), lambda bi, j: (bi, 0, 0))],
        out_shape=[jax.ShapeDtypeStruct((b * t, d), F32), jax.ShapeDtypeStruct((b, state, d), F32)],
        scratch_shapes=[pltpu.VMEM((carry + tm, d), F32)],
        compiler_params=_cparams(("arbitrary", "arbitrary")),
        name="pool_prompt",
    )(hn, w_in, w_grp, scale)


def _pool_sample_kernel(hn_ref, prev_ref, win_ref, wg_ref, sc_ref, mo_ref, st_ref, u_ref, z_ref,
                        *, bs, t, state, pos0):
    d = hn_ref.shape[1]
    gd = d // len(POOL_WINDOWS)
    u = _dot(hn_ref[...], win_ref[...])
    u_ref[...] = u.reshape(bs, t, d)

    def row(i, c0, c1):
        if i >= 0:
            return u_ref[:, i, c0:c1]
        return prev_ref[:, state + i, c0:c1]

    for g, w in enumerate(POOL_WINDOWS):
        c0, c1 = g * gd, (g + 1) * gd
        for ti in range(t):
            s = row(ti, c0, c1)
            for k in range(1, w):
                s = s + row(ti - k, c0, c1)
            cnt = float(min(pos0 + ti + 1, w))
            z_ref[:, ti, c0:c1] = s / cnt - row(ti, c0, c1)
    z = z_ref[...].reshape(bs * t, d)
    for g in range(len(POOL_WINDOWS)):
        c0, c1 = g * gd, (g + 1) * gd
        y = _dot(z[:, c0:c1].astype(BF16), wg_ref[g])
        mo_ref[:, c0:c1] = y * sc_ref[:, c0:c1]
    if t >= state:
        st_ref[...] = u_ref[:, t - state:t, :]
    else:
        st_ref[:, 0:state - t, :] = prev_ref[:, t:state, :]
        st_ref[:, state - t:state, :] = u_ref[...]


def _pool_sample(hn, row0, prev, w_in, w_grp, scale, b, t, pos0):
    d = hn.shape[1]
    state = prev.shape[1]
    bs = SAMPLE_SEQ_BLOCK
    blk0 = row0 // (bs * t)
    return pl.pallas_call(
        functools.partial(_pool_sample_kernel, bs=bs, t=t, state=state, pos0=pos0),
        grid=(b // bs,),
        in_specs=[
            pl.BlockSpec((bs * t, d), lambda i: (blk0 + i, 0)),
            pl.BlockSpec((bs, state, d), lambda i: (i, 0, 0)),
            pl.BlockSpec((d, d), lambda i: (0, 0)),
            pl.BlockSpec(w_grp.shape, lambda i: (0, 0, 0)),
            pl.BlockSpec((1, d), lambda i: (0, 0)),
        ],
        out_specs=[pl.BlockSpec((bs * t, d), lambda i: (i, 0)),
                   pl.BlockSpec((bs, state, d), lambda i: (i, 0, 0))],
        out_shape=[jax.ShapeDtypeStruct((b * t, d), F32), jax.ShapeDtypeStruct((b, state, d), F32)],
        scratch_shapes=[pltpu.VMEM((bs, t, d), F32), pltpu.VMEM((bs, t, d), F32)],
        compiler_params=_cparams(("arbitrary",)),
        name="pool_sample",
    )(hn, prev, w_in, w_grp, scale)


def _layernorm_silu(c, g, b):
    mu = jnp.mean(c, axis=-1, keepdims=True)
    xc = c - mu
    var = jnp.mean(xc * xc, axis=-1, keepdims=True)
    n = xc * lax.rsqrt(var + LN_EPS) * g + b
    return n * _sigmoid(n)


def _conv_prompt_kernel(hn_ref, w1_ref, b1_ref, wdw_ref, bdw_ref, lg_ref, lb_ref, w2_ref, b2_ref,
                        mo_ref, st_ref, ext_ref, c_ref, tap_ref, *, tm, carry, width, rc):
    j = pl.program_id(1)
    nj = pl.num_programs(1)
    d = hn_ref.shape[1]
    sub = V7X_SUBLANES

    @pl.when((pl.program_id(0) == 0) & (j == 0))
    def _():
        for k in range(width):
            tap_ref[k] = jnp.broadcast_to(wdw_ref[k:k + 1, :], (sub, d))

    a = _dot(hn_ref[...], w1_ref[...]) + b1_ref[...]
    glu = a[:, :d] * _sigmoid(a[:, d:])

    length = carry + tm

    @pl.when(j == 0)
    def _():
        ext_ref[0, 0:carry, :] = jnp.zeros((carry, d), F32)

    @pl.when(j > 0)
    def _():
        ext_ref[0, 0:carry, :] = ext_ref[0, tm:tm + carry, :]

    ext_ref[0, carry:carry + tm, :] = glu
    for m in range(1, V7X_SUBLANES):
        ext_ref[m, 0:length - V7X_SUBLANES, :] = ext_ref[0, m:length - V7X_SUBLANES + m, :]
    off = carry - (width - 1)

    def chunk(ci, _):
        r0 = pl.multiple_of(ci * rc, rc)
        acc = jnp.zeros((rc // sub, sub, d), F32) + bdw_ref[...]
        for k in range(width):
            q, m = divmod(off + k, sub)
            win = ext_ref[m, pl.ds(r0 + q * sub, rc), :].reshape(rc // sub, sub, d)
            acc = acc + win * tap_ref[k]
        c_ref[pl.ds(r0, rc), :] = acc.reshape(rc, d)
        return 0

    lax.fori_loop(0, tm // rc, chunk, 0)
    s = _layernorm_silu(c_ref[...], lg_ref[...], lb_ref[...])
    mo_ref[...] = _dot(s.astype(BF16), w2_ref[...]) + b2_ref[...]

    @pl.when(j == nj - 1)
    def _():
        st_ref[0] = ext_ref[0, carry + tm - (width - 1):carry + tm, :]


def _conv_prompt(hn, w1, b1, wdw, bdw, lg, lb, w2, b2, b, t, tm):
    d = hn.shape[1]
    width = wdw.shape[0]
    carry = 32
    ntj = t // tm
    const = lambda bi, j: (0, 0)
    return pl.pallas_call(
        functools.partial(_conv_prompt_kernel, tm=tm, carry=carry, width=width, rc=16),
        grid=(b, ntj),
        in_specs=[
            pl.BlockSpec((tm, d), lambda bi, j: (bi * ntj + j, 0)),
            pl.BlockSpec((d, 2 * d), const), pl.BlockSpec((1, 2 * d), const),
            pl.BlockSpec((width, d), const), pl.BlockSpec((1, d), const),
            pl.BlockSpec((1, d), const), pl.BlockSpec((1, d), const),
            pl.BlockSpec((d, d), const), pl.BlockSpec((1, d), const),
        ],
        out_specs=[pl.BlockSpec((tm, d), lambda bi, j: (bi * ntj + j, 0)),
                   pl.BlockSpec((1, width - 1, d), lambda bi, j: (bi, 0, 0))],
        out_shape=[jax.ShapeDtypeStruct((b * t, d), F32),
                   jax.ShapeDtypeStruct((b, width - 1, d), F32)],
        scratch_shapes=[pltpu.VMEM((V7X_SUBLANES, carry + tm, d), F32), pltpu.VMEM((tm, d), F32),
                        pltpu.VMEM((width, V7X_SUBLANES, d), F32)],
        compiler_params=_cparams(("arbitrary", "arbitrary")),
        name="conv_prompt",
    )(hn, w1, b1, wdw, bdw, lg, lb, w2, b2)


def _conv_sample_kernel(hn_ref, prev_ref, w1_ref, b1_ref, wdw_ref, bdw_ref, lg_ref, lb_ref, w2_ref,
                        b2_ref, mo_ref, st_ref, g_ref, c_ref, *, bs, t, width):
    d = hn_ref.shape[1]
    state = width - 1
    a = _dot(hn_ref[...], w1_ref[...]) + b1_ref[...]
    glu = a[:, :d] * _sigmoid(a[:, d:])
    g_ref[...] = glu.reshape(bs, t, d)

    def row(i):
        if i >= 0:
            return g_ref[:, i, :]
        return prev_ref[:, state + i, :]

    for ti in range(t):
        acc = jnp.zeros((bs, d), F32) + bdw_ref[...]
        for k in range(width):
            acc = acc + row(ti + k - state) * wdw_ref[k:k + 1, :]
        c_ref[:, ti, :] = acc
    s = _layernorm_silu(c_ref[...].reshape(bs * t, d), lg_ref[...], lb_ref[...])
    mo_ref[...] = _dot(s.astype(BF16), w2_ref[...]) + b2_ref[...]
    if t >= state:
        st_ref[...] = g_ref[:, t - state:t, :]
    else:
        st_ref[:, 0:state - t, :] = prev_ref[:, t:state, :]
        st_ref[:, state - t:state, :] = g_ref[...]


def _conv_sample(hn, row0, prev, w1, b1, wdw, bdw, lg, lb, w2, b2, b, t):
    d = hn.shape[1]
    width = wdw.shape[0]
    bs = SAMPLE_SEQ_BLOCK
    blk0 = row0 // (bs * t)
    const = lambda i: (0, 0)
    return pl.pallas_call(
        functools.partial(_conv_sample_kernel, bs=bs, t=t, width=width),
        grid=(b // bs,),
        in_specs=[
            pl.BlockSpec((bs * t, d), lambda i: (blk0 + i, 0)),
            pl.BlockSpec((bs, width - 1, d), lambda i: (i, 0, 0)),
            pl.BlockSpec((d, 2 * d), const), pl.BlockSpec((1, 2 * d), const),
            pl.BlockSpec((width, d), const), pl.BlockSpec((1, d), const),
            pl.BlockSpec((1, d), const), pl.BlockSpec((1, d), const),
            pl.BlockSpec((d, d), const), pl.BlockSpec((1, d), const),
        ],
        out_specs=[pl.BlockSpec((bs * t, d), lambda i: (i, 0)),
                   pl.BlockSpec((bs, width - 1, d), lambda i: (i, 0, 0))],
        out_shape=[jax.ShapeDtypeStruct((b * t, d), F32),
                   jax.ShapeDtypeStruct((b, width - 1, d), F32)],
        scratch_shapes=[pltpu.VMEM((bs, t, d), F32), pltpu.VMEM((bs, t, d), F32)],
        compiler_params=_cparams(("arbitrary",)),
        name="conv_sample",
    )(hn, prev, w1, b1, wdw, bdw, lg, lb, w2, b2)


def _mla_qkv_kernel(hn_ref, wd_ref, qn_ref, kvn_ref, wuq_ref, wuk_ref, cos_ref, sin_ref,
                    ckv_ref, kpe_ref, kcat_ref, qcat_ref, *, q_lora, kv_lora, rope, nh, nope, scale):
    lanes = V7X_LANES
    qk = kv_lora + lanes
    dq = _dot(hn_ref[...], wd_ref[...])
    cq = _rms(dq[:, :q_lora], qn_ref[...]).astype(BF16)
    ckv = _rms(dq[:, q_lora:q_lora + kv_lora], kvn_ref[...])
    o = q_lora + kv_lora
    cos = cos_ref[...]
    sin = sin_ref[...]
    kblk = dq[:, o:o + lanes] * cos + dq[:, o + lanes:o + 2 * lanes] * sin
    ckv_ref[...] = ckv
    kpe_ref[...] = kblk[:, :rope]
    kcat_ref[:, :kv_lora] = ckv.astype(BF16)
    kcat_ref[:, kv_lora:] = kblk.astype(BF16)
    q = _dot(cq, wuq_ref[...])
    qn = nh * nope
    for h in range(nh):
        ql = _dot(q[:, h * nope:(h + 1) * nope].astype(BF16), wuk_ref[h])
        r0 = qn + h * lanes
        r1 = qn + (nh + h) * lanes
        qpe = q[:, r0:r0 + lanes] * cos + q[:, r1:r1 + lanes] * sin
        qcat_ref[:, h * qk:h * qk + kv_lora] = (ql * scale).astype(qcat_ref.dtype)
        qcat_ref[:, h * qk + kv_lora:(h + 1) * qk] = (qpe * scale).astype(qcat_ref.dtype)


def _mla_qkv(hn, blk0, nrows, wd, qn, kvn, wuq, wuk, cos, sin, cos_map, tm, dims, q_dtype):
    q_lora, kv_lora, rope, nh, nope, scale = dims
    d = hn.shape[1]
    qk = kv_lora + V7X_LANES
    const2 = lambda i: (0, 0)
    return pl.pallas_call(
        functools.partial(_mla_qkv_kernel, q_lora=q_lora, kv_lora=kv_lora, rope=rope, nh=nh,
                          nope=nope, scale=scale),
        grid=(nrows // tm,),
        in_specs=[
            pl.BlockSpec((tm, d), lambda i: (blk0 + i, 0)),
            pl.BlockSpec(wd.shape, const2), pl.BlockSpec((1, q_lora), const2),
            pl.BlockSpec((1, kv_lora), const2), pl.BlockSpec(wuq.shape, const2),
            pl.BlockSpec(wuk.shape, lambda i: (0, 0, 0)),
            pl.BlockSpec((tm, V7X_LANES), cos_map), pl.BlockSpec((tm, V7X_LANES), cos_map),
        ],
        out_specs=[pl.BlockSpec((tm, kv_lora), lambda i: (i, 0)),
                   pl.BlockSpec((tm, rope), lambda i: (i, 0)),
                   pl.BlockSpec((tm, qk), lambda i: (i, 0)),
                   pl.BlockSpec((tm, nh * qk), lambda i: (i, 0))],
        out_shape=[jax.ShapeDtypeStruct((nrows, kv_lora), F32),
                   jax.ShapeDtypeStruct((nrows, rope), F32),
                   jax.ShapeDtypeStruct((nrows, qk), BF16),
                   jax.ShapeDtypeStruct((nrows, nh * qk), q_dtype)],
        compiler_params=_cparams(("arbitrary",)),
        name="mla_qkv",
    )(hn, wd, qn, kvn, wuq, wuk, cos, sin)


def _softmax_update(s, v, m_ref, l_ref, acc_ref):
    m_old = m_ref[...]
    m_new = jnp.maximum(m_old, jnp.max(s, axis=1, keepdims=True))
    alpha = jnp.exp(m_old - m_new)
    p = jnp.exp(s - m_new)
    l_ref[...] = alpha * l_ref[...] + jnp.sum(p, axis=1, keepdims=True)
    acc_ref[...] = alpha * acc_ref[...] + _dot(p.astype(BF16), v)
    m_ref[...] = m_new


def _attn_prompt_kernel(q_ref, k_ref, o_ref, m_ref, l_ref, acc_ref, *, tq, tk, nh, qk, kv):
    i = pl.program_id(1)
    m_ref[...] = jnp.full(m_ref.shape, NEG_INF, F32)
    l_ref[...] = jnp.zeros(l_ref.shape, F32)
    acc_ref[...] = jnp.zeros(acc_ref.shape, F32)
    qpos = i * tq + lax.broadcasted_iota(jnp.int32, (tq, 1), 0)
    n_full = (i * tq + 1) // tk
    n_chunks = (i * tq + tq - 1) // tk + 1

    def make_body(masked):
        def body(kk, c):
            ks = pl.multiple_of(kk * tk, tk)
            kc = k_ref[pl.ds(ks, tk), :]
            v = kc[:, :kv]
            if masked:
                vis = ks + lax.broadcasted_iota(jnp.int32, (tq, tk), 1) <= qpos
            for h in range(nh):
                s = _dot_t(q_ref[:, h * qk:(h + 1) * qk], kc)
                if masked:
                    s = jnp.where(vis, s, NEG_INF)
                _softmax_update(s, v, m_ref.at[h], l_ref.at[h], acc_ref.at[h])
            return c
        return body

    lax.fori_loop(0, n_full, make_body(False), 0)
    lax.fori_loop(n_full, n_chunks, make_body(True), 0)
    for h in range(nh):
        o_ref[:, h * kv:(h + 1) * kv] = (acc_ref[h] / l_ref[h]).astype(BF16)


def _attn_prompt(qcat, kcat, b, t, nh, kv):
    qk = kcat.shape[1]
    tq = min(ATTN_Q_TILE, t)
    tk = min(ATTN_K_TILE, t)
    nq = t // tq
    return pl.pallas_call(
        functools.partial(_attn_prompt_kernel, tq=tq, tk=tk, nh=nh, qk=qk, kv=kv),
        grid=(b, nq),
        in_specs=[
            pl.BlockSpec((tq, nh * qk), lambda bi, i: (bi * nq + i, 0)),
            pl.BlockSpec((t, qk), lambda bi, i: (bi, 0)),
        ],
        out_specs=pl.BlockSpec((tq, nh * kv), lambda bi, i: (bi * nq + i, 0)),
        out_shape=jax.ShapeDtypeStruct((b * t, nh * kv), BF16),
        scratch_shapes=[pltpu.VMEM((nh, tq, 1), F32), pltpu.VMEM((nh, tq, 1), F32),
                        pltpu.VMEM((nh, tq, kv), F32)],
        compiler_params=_cparams(("arbitrary", "arbitrary")),
        name="mla_attn_prompt",
    )(qcat, kcat)


def _page_copies(pt_ref, lat_hbm, kr_hbm, lat_buf, kr_buf, sem, layer, npages, seq, slot):
    def body(p, c):
        idx = pt_ref[seq * npages + p]
        pltpu.make_async_copy(lat_hbm.at[layer, idx], lat_buf.at[slot, p], sem.at[0, slot]).start()
        pltpu.make_async_copy(kr_hbm.at[layer, idx], kr_buf.at[slot, p], sem.at[1, slot]).start()
        return c

    lax.fori_loop(0, npages, body, 0, unroll=DMA_ISSUE_UNROLL)


def _attn_sample_kernel(pt_ref, q_ref, cn_ref, krn_ref, lat_hbm, kr_hbm, o_ref,
                        lat_buf, kr_buf, sem, s_ref, cb_ref,
                        *, layer, npages, chunk, t, nh, qk, kv, rope, page):
    bi = pl.program_id(0)
    nb = pl.num_programs(0)
    slot = bi % 2
    fetch = functools.partial(_page_copies, pt_ref, lat_hbm, kr_hbm, lat_buf, kr_buf, sem, layer, npages)

    @pl.when(bi == 0)
    def _():
        fetch(0, 0)

    @pl.when(bi + 1 < nb)
    def _():
        fetch(bi + 1, 1 - slot)

    q = jnp.concatenate([q_ref[:, h * qk:(h + 1) * qk] for h in range(nh)], axis=0).astype(BF16)
    ql = q[:, :kv]
    qp = q[:, kv:kv + rope]

    pltpu.make_async_copy(lat_hbm.at[layer, pl.ds(0, npages)], lat_buf.at[slot], sem.at[0, slot]).wait()
    pltpu.make_async_copy(kr_hbm.at[layer, pl.ds(0, npages)], kr_buf.at[slot], sem.at[1, slot]).wait()

    width = chunk * page
    n_chunks = npages // chunk
    for ch in range(n_chunks):
        c = lat_buf[slot, ch * chunk:(ch + 1) * chunk].reshape(width, kv).astype(BF16)
        cb_ref[ch * width:(ch + 1) * width, :] = c
        krt = jnp.concatenate([kr_buf[slot, ch * chunk + p] for p in range(chunk)], axis=1).astype(BF16)
        s_ref[:, ch * width:(ch + 1) * width] = _dot_t(ql, c) + _dot(qp, krt)

    rows = nh * t
    cn = jnp.concatenate([cn_ref[...], jnp.zeros((page - t, kv), F32)], axis=0).astype(BF16)
    krn = jnp.concatenate([krn_ref[...], jnp.zeros((page - t, rope), F32)], axis=0).astype(BF16)
    sn = _dot_t(ql, cn) + _dot_t(qp, krn)
    qstep = lax.broadcasted_iota(jnp.int32, (rows, 1), 0) & (t - 1)
    kstep = lax.broadcasted_iota(jnp.int32, (rows, page), 1)
    sn = jnp.where(kstep <= qstep, sn, NEG_INF)

    m = jnp.maximum(jnp.max(s_ref[...], axis=1, keepdims=True), jnp.max(sn, axis=1, keepdims=True))
    pn = jnp.exp(sn - m)
    l = jnp.sum(pn, axis=1, keepdims=True)
    acc = _dot(pn.astype(BF16), cn)
    for ch in range(n_chunks):
        p = jnp.exp(s_ref[:, ch * width:(ch + 1) * width] - m)
        l = l + jnp.sum(p, axis=1, keepdims=True)
        acc = acc + _dot(p.astype(BF16), cb_ref[ch * width:(ch + 1) * width, :])
    o = acc / l
    for h in range(nh):
        o_ref[:, h * kv:(h + 1) * kv] = o[h * t:(h + 1) * t, :]


def _attn_sample(page_table, qcat, cn, krn, cache_lat, cache_kr_t, layer, b, t, nh):
    kv = cn.shape[1]
    rope = krn.shape[1]
    qk = qcat.shape[1] // nh
    page = cache_lat.shape[2]
    npages = page_table.shape[1]
    chunk = min(DECODE_PAGE_CHUNK, npages)
    assert npages % chunk == 0 and t % V7X_SUBLANES == 0 and t <= page and t & (t - 1) == 0
    rows = t * nh
    grid_spec = pltpu.PrefetchScalarGridSpec(
        num_scalar_prefetch=1,
        grid=(b,),
        in_specs=[
            pl.BlockSpec((t, nh * qk), lambda bi, ptr: (bi, 0)),
            pl.BlockSpec((t, kv), lambda bi, ptr: (bi, 0)),
            pl.BlockSpec((t, rope), lambda bi, ptr: (bi, 0)),
            pl.BlockSpec(memory_space=pl.ANY),
            pl.BlockSpec(memory_space=pl.ANY),
        ],
        out_specs=pl.BlockSpec((t, nh * kv), lambda bi, ptr: (bi, 0)),
        scratch_shapes=[
            pltpu.VMEM((2, npages, page, kv), F32),
            pltpu.VMEM((2, npages, rope, page), F32),
            pltpu.SemaphoreType.DMA((2, 2)),
            pltpu.VMEM((rows, npages * page), F32), pltpu.VMEM((npages * page, kv), BF16),
        ],
    )
    return pl.pallas_call(
        functools.partial(_attn_sample_kernel, layer=layer, npages=npages, chunk=chunk, t=t, nh=nh,
                          qk=qk, kv=kv, rope=rope, page=page),
        grid_spec=grid_spec,
        out_shape=jax.ShapeDtypeStruct((b * t, nh * kv), F32),
        compiler_params=_cparams(("arbitrary",)),
        name="mla_attn_sample",
    )(page_table.reshape(-1), qcat, cn, krn, cache_lat, cache_kr_t)


def _mla_out_kernel(o_ref, wuv_ref, wo_ref, mo_ref, *, nh, kv):
    vs = [_dot(o_ref[:, h * kv:(h + 1) * kv].astype(BF16), wuv_ref[h]).astype(BF16) for h in range(nh)]
    mo_ref[...] = _dot(jnp.concatenate(vs, axis=1), wo_ref[...])


def _mla_out(o, wuv, wo, tm):
    n = o.shape[0]
    nh, kv, vd = wuv.shape
    d = wo.shape[1]
    return pl.pallas_call(
        functools.partial(_mla_out_kernel, nh=nh, kv=kv),
        grid=(n // tm,),
        in_specs=[pl.BlockSpec((tm, nh * kv), lambda i: (i, 0)),
                  pl.BlockSpec(wuv.shape, lambda i: (0, 0, 0)),
                  pl.BlockSpec(wo.shape, lambda i: (0, 0))],
        out_specs=pl.BlockSpec((tm, d), lambda i: (i, 0)),
        out_shape=jax.ShapeDtypeStruct((n, d), F32),
        compiler_params=_cparams(("arbitrary",)),
        name="mla_out",
    )(o, wuv, wo)


def _route_kernel(h_ref, mop_ref, mos_ref, g_ref, whi_ref, wlo_ref, br_ref,
                  h1_ref, xn_ref, slab_ref, cnt_ref, carry_ref, *, npt, tm, n_groups, epg):
    i = pl.program_id(0)

    @pl.when(i == 0)
    def _():
        carry_ref[...] = jnp.zeros(carry_ref.shape, F32)

    h1 = h_ref[...] + jnp.where(i < npt, mop_ref[...], mos_ref[...])
    h1_ref[...] = h1
    xn = _rms(h1, g_ref[...])
    xn_ref[...] = _pack_bf16_pairs(xn)
    x_hi = xn.astype(BF16)
    x_lo = (xn - x_hi.astype(F32)).astype(BF16)
    whi = whi_ref[...]
    logits = _dot(x_hi, whi) + _dot(x_lo, whi) + _dot(x_hi, wlo_ref[...]) + br_ref[...]

    lane = lax.broadcasted_iota(jnp.int32, (tm, V7X_LANES), 1).astype(F32)
    big = float(V7X_LANES)
    gl = jnp.where(lane < n_groups, logits, NEG_INF)
    gmax = jnp.max(gl, axis=1, keepdims=True)
    gsel = jnp.min(jnp.where(gl == gmax, lane, big), axis=1, keepdims=True)
    gprob = 1.0 / jnp.sum(jnp.exp(gl - gmax), axis=1, keepdims=True)
    lo = n_groups + gsel * epg
    el = jnp.where(lane >= lo, jnp.where(lane < lo + epg, logits, NEG_INF), NEG_INF)
    m1 = jnp.max(el, axis=1, keepdims=True)
    i1 = jnp.min(jnp.where(el == m1, lane, big), axis=1, keepdims=True)
    el2 = jnp.where(lane == i1, NEG_INF, el)
    m2 = jnp.max(el2, axis=1, keepdims=True)
    i2 = jnp.min(jnp.where(el2 == m2, lane, big), axis=1, keepdims=True)
    r = jnp.exp(m2 - m1)
    g1 = gprob / (1.0 + r)
    g2 = gprob * r / (1.0 + r)

    oh1 = jnp.where(lane == i1, 1.0, 0.0)
    oh2 = jnp.where(lane == i2, 1.0, 0.0)
    oh = oh1 + oh2
    rr = lax.broadcasted_iota(jnp.int32, (tm, tm), 0)
    cc = lax.broadcasted_iota(jnp.int32, (tm, tm), 1)
    tri = jnp.where(rr > cc, 1.0, 0.0).astype(BF16)
    carry = carry_ref[0:1, :]
    cum = _dot(tri, oh.astype(BF16)) + carry
    rank1 = jnp.sum(oh1 * cum, axis=1, keepdims=True)
    rank2 = jnp.sum(oh2 * cum, axis=1, keepdims=True)
    new_carry = carry + jnp.sum(oh, axis=0, keepdims=True)
    carry_ref[...] = jnp.broadcast_to(new_carry, carry_ref.shape)
    cnt_ref[...] = jnp.broadcast_to(new_carry, cnt_ref.shape)

    slab = jnp.where(lane == 0, i1 - n_groups, 0.0)
    slab = jnp.where(lane == 1, i2 - n_groups, slab)
    slab = jnp.where(lane == 2, g1, slab)
    slab = jnp.where(lane == 3, g2, slab)
    slab = jnp.where(lane == 4, rank1, slab)
    slab = jnp.where(lane == 5, rank2, slab)
    slab_ref[...] = slab


def _route(h, mo_p, mo_s, g, whi, wlo, br, tm, n_groups, epg):
    n, d = h.shape
    npt = mo_p.shape[0] // tm
    const = lambda i: (0, 0)
    return pl.pallas_call(
        functools.partial(_route_kernel, npt=npt, tm=tm, n_groups=n_groups, epg=epg),
        grid=(n // tm,),
        in_specs=[
            pl.BlockSpec((tm, d), lambda i: (i, 0)),
            pl.BlockSpec((tm, d), lambda i: (jnp.minimum(i, npt - 1), 0)),
            pl.BlockSpec((tm, d), lambda i: (jnp.maximum(i - npt, 0), 0)),
            pl.BlockSpec((1, d), const),
            pl.BlockSpec((d, V7X_LANES), const), pl.BlockSpec((d, V7X_LANES), const),
            pl.BlockSpec((1, V7X_LANES), const),
        ],
        out_specs=[pl.BlockSpec((tm, d), lambda i: (i, 0)),
                   pl.BlockSpec((tm, d // 2), lambda i: (i, 0)),
                   pl.BlockSpec((tm, V7X_LANES), lambda i: (i, 0)),
                   pl.BlockSpec((V7X_SUBLANES, V7X_LANES), const)],
        out_shape=[jax.ShapeDtypeStruct((n, d), F32), jax.ShapeDtypeStruct((n, d // 2), jnp.uint32),
                   jax.ShapeDtypeStruct((n, V7X_LANES), F32),
                   jax.ShapeDtypeStruct((V7X_SUBLANES, V7X_LANES), F32)],
        scratch_shapes=[pltpu.VMEM((V7X_SUBLANES, V7X_LANES), F32)],
        compiler_params=_cparams(("arbitrary",)),
        name="route",
    )(h, mo_p, mo_s, g, whi, wlo, br)


def _dispatch_tables(slab, cnt, n_experts, n_groups, tm, n_blk):
    counts = cnt[0, n_groups:n_groups + n_experts].astype(jnp.int32)
    padded = (counts + tm - 1) // tm * tm
    pad_ends = jnp.cumsum(padded)
    pad_starts = pad_ends - padded
    eids = jnp.arange(n_experts, dtype=jnp.int32)

    def dest(k):
        e = slab[:, k].astype(jnp.int32)
        start = jnp.sum(jnp.where(e[:, None] == eids[None, :], pad_starts[None, :], 0), axis=1)
        return start + slab[:, 4 + k].astype(jnp.int32)

    blk_start = jnp.arange(n_blk, dtype=jnp.int32) * tm
    blk_e = jnp.sum((blk_start[:, None] >= pad_ends[None, :]).astype(jnp.int32), axis=1)
    blk_e = jnp.minimum(blk_e, n_experts - 1)
    nblk = pad_ends[-1:] // tm
    return dest(0), dest(1), blk_e, nblk


def _dispatch_kernel(d0_ref, d1_ref, xn_ref, xs_in, xs_out, sem, *, tm):
    del xs_in
    i = pl.program_id(0)
    base = i * tm

    def issue(r):
        src = xn_ref.at[pl.ds(r, 1)]
        pltpu.make_async_copy(src, xs_out.at[pl.ds(d0_ref[base + r], 1)], sem).start()
        pltpu.make_async_copy(src, xs_out.at[pl.ds(d1_ref[base + r], 1)], sem).start()

    _issue_rows(tm, issue)
    for _ in range(TOP_K):
        pltpu.make_async_copy(xn_ref, xs_out.at[pl.ds(0, tm)], sem).wait()


def _dispatch(d0, d1, xn, xs_zero, tm):
    n, d = xn.shape
    grid_spec = pltpu.PrefetchScalarGridSpec(
        num_scalar_prefetch=2,
        grid=(n // tm,),
        in_specs=[pl.BlockSpec((tm, d), lambda i, a, b: (i, 0)),
                  pl.BlockSpec(memory_space=pl.ANY)],
        out_specs=pl.BlockSpec(memory_space=pl.ANY),
        scratch_shapes=[pltpu.SemaphoreType.DMA],
    )
    return pl.pallas_call(
        functools.partial(_dispatch_kernel, tm=tm),
        grid_spec=grid_spec,
        out_shape=jax.ShapeDtypeStruct(xs_zero.shape, xs_zero.dtype),
        input_output_aliases={3: 0},
        compiler_params=_cparams(("arbitrary",)),
        name="moe_dispatch",
    )(d0, d1, xn, xs_zero)


def _experts_kernel(be_ref, nb_ref, xs_ref, w1_ref, w3_ref, w2_ref, y_ref, w1b, w3b, w2b):
    i = pl.program_id(0)

    @pl.when(i < nb_ref[0])
    def _():
        e = be_ref[i]
        e_prev = be_ref[jnp.maximum(i - 1, 0)]

        @pl.when((i == 0) | (e != e_prev))
        def _():
            w1b[...] = w1_ref[...].astype(BF16)
            w3b[...] = w3_ref[...].astype(BF16)
            w2b[...] = w2_ref[...].astype(BF16)

        xa, xb = _unpack_bf16_pairs(xs_ref[...])
        xa = xa.astype(BF16)
        xb = xb.astype(BF16)
        half = xa.shape[1]
        a = _dot(xa, w1b[0:half, :]) + _dot(xb, w1b[half:, :])
        g = _dot(xa, w3b[0:half, :]) + _dot(xb, w3b[half:, :])
        hmid = (a * _sigmoid(a)) * g
        y_ref[...] = _pack_bf16_pairs(_dot(hmid.astype(BF16), w2b[...]))


def _experts(blk_e, nblk, xs, w1, w3, w2, layer, tm):
    n_rows = xs.shape[0]
    d = w1.shape[-2]
    f = w1.shape[-1]
    wmap = lambda i, be, nb: (layer, be[i], 0, 0)
    xmap = lambda i, be, nb: (jnp.minimum(i, nb[0] - 1), 0)
    grid_spec = pltpu.PrefetchScalarGridSpec(
        num_scalar_prefetch=2,
        grid=(n_rows // tm,),
        in_specs=[
            pl.BlockSpec((tm, d // 2), xmap),
            pl.BlockSpec((None, None, d, f), wmap),
            pl.BlockSpec((None, None, d, f), wmap),
            pl.BlockSpec((None, None, f, d), wmap),
        ],
        out_specs=pl.BlockSpec((tm, d // 2), xmap),
        scratch_shapes=[pltpu.VMEM((d, f), BF16), pltpu.VMEM((d, f), BF16), pltpu.VMEM((f, d), BF16)],
    )
    return pl.pallas_call(
        _experts_kernel,
        grid_spec=grid_spec,
        out_shape=jax.ShapeDtypeStruct(xs.shape, xs.dtype),
        input_output_aliases={2: 0},
        compiler_params=_cparams(("arbitrary",)),
        name="moe_experts",
    )(blk_e, nblk, xs, w1, w3, w2)


def _combine_copies(d0_ref, d1_ref, y_hbm, ybuf, sem, tm, tile, slot):
    base = tile * tm

    def issue(r):
        pltpu.make_async_copy(y_hbm.at[pl.ds(d0_ref[base + r], 1)], ybuf.at[slot, 0, pl.ds(r, 1)],
                              sem.at[slot]).start()
        pltpu.make_async_copy(y_hbm.at[pl.ds(d1_ref[base + r], 1)], ybuf.at[slot, 1, pl.ds(r, 1)],
                              sem.at[slot]).start()

    _issue_rows(tm, issue)


def _ple_kernel(d0_ref, d1_ref, h1_ref, y_hbm, slab_ref, pp_ref, ps_ref, gple_ref, wg_ref, wp_ref,
                gn_ref, *rest, npt, last, tm):
    ybuf, sem = rest[-2:]
    outs = rest[:-2]
    i = pl.program_id(0)
    nt = pl.num_programs(0)
    slot = i % 2
    fetch = functools.partial(_combine_copies, d0_ref, d1_ref, y_hbm, ybuf, sem, tm)

    @pl.when(i == 0)
    def _():
        fetch(0, 0)

    @pl.when(i + 1 < nt)
    def _():
        fetch(i + 1, 1 - slot)

    for k in range(TOP_K):
        pltpu.make_async_copy(y_hbm.at[pl.ds(0, tm)], ybuf.at[slot, k], sem.at[slot]).wait()

    slab = slab_ref[...]
    y0 = jnp.concatenate(_unpack_bf16_pairs(ybuf[slot, 0]), axis=1)
    y1 = jnp.concatenate(_unpack_bf16_pairs(ybuf[slot, 1]), axis=1)
    h2 = h1_ref[...] + slab[:, 2:3] * y0 + slab[:, 3:4] * y1
    gate = _sigmoid(_dot(_rms(h2, gple_ref[...]).astype(BF16), wg_ref[...]))
    p = jnp.where(i < npt, pp_ref[...], ps_ref[...]).astype(BF16)
    h3 = h2 + gate * _dot(p, wp_ref[...])
    hn = _rms(h3, gn_ref[...])
    if last:
        yp_ref, ys_ref = outs

        @pl.when(i < npt)
        def _():
            yp_ref[...] = hn

        @pl.when(i >= npt)
        def _():
            ys_ref[...] = hn
    else:
        h_ref, hn_ref = outs
        h_ref[...] = h3
        hn_ref[...] = hn.astype(BF16)


def _ple(d0, d1, h1, y, slab, p_p, p_s, layer, gple, wg, wp, gn, tm, last):
    n, d = h1.shape
    np_ = p_p.shape[1]
    npt = np_ // tm
    pd = p_p.shape[2]
    const = lambda i, a, b: (0, 0)
    rowmap = lambda i, a, b: (i, 0)
    if last:
        out_specs = [pl.BlockSpec((tm, d), lambda i, a, b: (jnp.minimum(i, npt - 1), 0)),
                     pl.BlockSpec((tm, d), lambda i, a, b: (jnp.maximum(i - npt, 0), 0))]
        out_shape = [jax.ShapeDtypeStruct((np_, d), F32), jax.ShapeDtypeStruct((n - np_, d), F32)]
    else:
        out_specs = [pl.BlockSpec((tm, d), rowmap), pl.BlockSpec((tm, d), rowmap)]
        out_shape = [jax.ShapeDtypeStruct((n, d), F32), jax.ShapeDtypeStruct((n, d), BF16)]
    grid_spec = pltpu.PrefetchScalarGridSpec(
        num_scalar_prefetch=2,
        grid=(n // tm,),
        in_specs=[
            pl.BlockSpec((tm, d), rowmap),
            pl.BlockSpec(memory_space=pl.ANY),
            pl.BlockSpec((tm, V7X_LANES), rowmap),
            pl.BlockSpec((None, tm, pd), lambda i, a, b: (layer, jnp.minimum(i, npt - 1), 0)),
            pl.BlockSpec((None, tm, pd), lambda i, a, b: (layer, jnp.maximum(i - npt, 0), 0)),
            pl.BlockSpec((1, d), const), pl.BlockSpec((d, d), const), pl.BlockSpec((pd, d), const),
            pl.BlockSpec((1, d), const),
        ],
        out_specs=out_specs,
        scratch_shapes=[pltpu.VMEM((2, TOP_K, tm, d // 2), jnp.uint32), pltpu.SemaphoreType.DMA((2,))],
    )
    return pl.pallas_call(
        functools.partial(_ple_kernel, npt=npt, last=last, tm=tm),
        grid_spec=grid_spec,
        out_shape=out_shape,
        compiler_params=_cparams(("arbitrary",)),
        name="ple",
    )(d0, d1, h1, y, slab, p_p, p_s, gple, wg, wp, gn)


def _rope_tables(pos, rope):
    half = rope // 2
    inv = 1.0 / (ROPE_THETA ** (jnp.arange(half, dtype=F32) * (2.0 / rope)))
    ang = pos.astype(F32)[:, None] * inv[None, :]
    pad = ((0, 0), (0, V7X_LANES - rope))
    cos = jnp.pad(jnp.tile(jnp.cos(ang), (1, 2)), pad)
    sin = jnp.pad(jnp.tile(jnp.sin(ang), (1, 2)), pad)
    return cos, sin


def _rot_last(w):
    half = w.shape[-1] // 2
    return jnp.concatenate([-w[..., half:], w[..., :half]], axis=-1)


def _pad_last(w, width):
    return jnp.pad(w, [(0, 0)] * (w.ndim - 1) + [(0, width - w.shape[-1])])


def kernel(x_prompt, x_sample, state_pool, state_conv, cache_latent, cache_krope, page_table, p_prompt, p_sample, norm_mix, norm_ffn, norm_ple, norm_final, pool_w_in, pool_w_grp, pool_scale, conv_w_pw1, conv_b_pw1, conv_w_dw, conv_b_dw, conv_ln_g, conv_ln_b, conv_w_pw2, conv_b_pw2, mla_w_dqkv, mla_q_norm, mla_kv_norm, mla_w_uq, mla_w_uk, mla_w_uv, mla_w_o, moe_w_group, moe_b_group, moe_w_expert, moe_b_expert, moe_w1, moe_w3, moe_w2, ple_w_gate, ple_w_proj):
    b, t, d = x_prompt.shape
    bs, ts, _ = x_sample.shape
    depth = norm_mix.shape[0]
    np_, ns = b * t, bs * ts
    n = np_ + ns
    tm = min(TOKEN_TILE, ns)
    assert np_ % tm == 0 and ns % tm == 0
    past_len = page_table.shape[1] * cache_latent.shape[2]
    n_groups = moe_w_group.shape[-1]
    n_experts = moe_w_expert.shape[-1]
    epg = n_experts // n_groups
    assert n_groups + n_experts <= V7X_LANES
    n_rows = (n * TOP_K + n_experts * (MOE_TILE - 1) + MOE_TILE - 1) // MOE_TILE * MOE_TILE
    pd = p_prompt.shape[-1]
    row = lambda v: v.reshape(1, -1)

    nh, kv_lora, nope = mla_w_uk.shape[1:]
    rope = cache_krope.shape[-1]
    q_lora = mla_w_dqkv.shape[-1] - kv_lora - rope
    scale = float((nope + rope) ** -0.5)
    mla_dims = (q_lora, kv_lora, rope, nh, nope, scale)
    cache_kr_t = jnp.swapaxes(cache_krope, 2, 3)

    h, hn = _prep(x_prompt.reshape(np_, d), x_sample.reshape(ns, d), row(norm_mix[0]), tm)
    p_p = p_prompt.reshape(depth, np_, pd)
    p_s = p_sample.reshape(depth, ns, pd)

    new_pool_p, new_pool_s, new_conv_p, new_conv_s = [], [], [], []
    new_lat_p, new_kr_p, new_lat_s, new_kr_s = [], [], [], []
    y_p = y_s = None
    for i in range(depth):
        kind, j = i % N_MIXERS, i // N_MIXERS
        if kind == 0:
            w_in = pool_w_in[j].astype(BF16)
            w_grp = pool_w_grp[j].astype(BF16)
            sc = row(pool_scale[j])
            mo_p, st_p = _pool_prompt(hn, w_in, w_grp, sc, b, t, min(tm, t))
            mo_s, st_s = _pool_sample(hn, np_, state_pool[j], w_in, w_grp, sc, bs, ts, past_len)
            new_pool_p.append(st_p)
            new_pool_s.append(st_s)
        elif kind == 1:
            w1 = conv_w_pw1[j].astype(BF16)
            w2 = conv_w_pw2[j].astype(BF16)
            args = (w1, row(conv_b_pw1[j]), conv_w_dw[j], row(conv_b_dw[j]), row(conv_ln_g[j]),
                    row(conv_ln_b[j]), w2, row(conv_b_pw2[j]))
            mo_p, st_p = _conv_prompt(hn, *args, b, t, min(CONV_TILE, t))
            mo_s, st_s = _conv_sample(hn, np_, state_conv[j], *args, bs, ts)
            new_conv_p.append(st_p)
            new_conv_s.append(st_s)
        else:
            wd = mla_w_dqkv[j]
            o = q_lora + kv_lora
            wd_k = wd[:, o:]
            wd = jnp.concatenate([wd[:, :o], _pad_last(wd_k, V7X_LANES),
                                  _pad_last(_rot_last(wd_k), V7X_LANES)], axis=1).astype(BF16)
            wuq = mla_w_uq[j].reshape(q_lora, nh, nope + rope)
            wq_rope = wuq[:, :, nope:]
            wuq = jnp.concatenate([
                wuq[:, :, :nope].reshape(q_lora, nh * nope),
                _pad_last(wq_rope, V7X_LANES).reshape(q_lora, nh * V7X_LANES),
                _pad_last(_rot_last(wq_rope), V7X_LANES).reshape(q_lora, nh * V7X_LANES)],
                axis=1).astype(BF16)
            wuk = jnp.swapaxes(mla_w_uk[j], 1, 2).astype(BF16)
            wuv = mla_w_uv[j].astype(BF16)
            wo = mla_w_o[j].astype(BF16)
            qn, kvn = row(mla_q_norm[j]), row(mla_kv_norm[j])
            tmp = min(tm, t)
            ntj = t // tmp
            cos_p, sin_p = _rope_tables(jnp.arange(t, dtype=jnp.int32), rope)
            ckv_p, kpe_p, kcat_p, qcat_p = _mla_qkv(
                hn, 0, np_, wd, qn, kvn, wuq, wuk, cos_p, sin_p, lambda ii: (ii % ntj, 0), tmp,
                mla_dims, BF16)
            cos_s, sin_s = _rope_tables(past_len + jnp.arange(ts, dtype=jnp.int32), rope)
            cos_s, sin_s = jnp.tile(cos_s, (bs, 1)), jnp.tile(sin_s, (bs, 1))
            ckv_s, kpe_s, _, qcat_s = _mla_qkv(
                hn, np_ // tm, ns, wd, qn, kvn, wuq, wuk, cos_s, sin_s, lambda ii: (ii, 0), tm,
                mla_dims, F32)
            o_p = _attn_prompt(qcat_p, kcat_p, b, t, nh, kv_lora)
            o_s = _attn_sample(page_table, qcat_s, ckv_s, kpe_s, cache_latent, cache_kr_t, j, bs, ts, nh)
            mo_p = _mla_out(o_p, wuv, wo, tm)
            mo_s = _mla_out(o_s, wuv, wo, tm)
            new_lat_p.append(ckv_p.reshape(b, t, kv_lora))
            new_kr_p.append(kpe_p.reshape(b, t, rope))
            new_lat_s.append(ckv_s.reshape(bs, ts, kv_lora))
            new_kr_s.append(kpe_s.reshape(bs, ts, rope))

        wr = _pad_last(jnp.concatenate([moe_w_group[i], moe_w_expert[i]], axis=1), V7X_LANES)
        br = _pad_last(jnp.concatenate([moe_b_group[i], moe_b_expert[i]])[None, :], V7X_LANES)
        whi = wr.astype(BF16)
        wlo = (wr - whi.astype(F32)).astype(BF16)
        h1, xn, slab, cnt = _route(h, mo_p, mo_s, row(norm_ffn[i]), whi, wlo, br, tm, n_groups, epg)
        d0, d1, blk_e, nblk = _dispatch_tables(slab, cnt, n_experts, n_groups, MOE_TILE, n_rows // MOE_TILE)
        xs = _dispatch(d0, d1, xn, jnp.zeros((n_rows, d // 2), jnp.uint32), tm)
        y = _experts(blk_e, nblk, xs, moe_w1, moe_w3, moe_w2, i, MOE_TILE)
        last = i == depth - 1
        gn = row(norm_final) if last else row(norm_mix[i + 1])
        out = _ple(d0, d1, h1, y, slab, p_p, p_s, i, row(norm_ple[i]), ple_w_gate[i].astype(BF16),
                   ple_w_proj[i].astype(BF16), gn, tm, last)
        if last:
            y_p, y_s = out
        else:
            h, hn = out

    return (y_p.reshape(b, t, d), y_s.reshape(bs, ts, d),
            jnp.stack(new_pool_p), jnp.stack(new_pool_s),
            jnp.stack(new_conv_p), jnp.stack(new_conv_s),
            jnp.stack(new_lat_p), jnp.stack(new_kr_p),
            jnp.stack(new_lat_s), jnp.stack(new_kr_s))
```

```python
import functools

import jax
import jax.numpy as jnp
from jax import lax
from jax.experimental import pallas as pl
from jax.experimental.pallas import tpu as pltpu

F32 = jnp.float32
BF16 = jnp.bfloat16

POOL_WINDOWS = (2, 4, 8, 16)
TOP_K = 2
ROPE_THETA = 10000.0
RMS_EPS = 1e-6
LN_EPS = 1e-5
NEG_INF = -1e30
N_MIXERS = 3

V7X_LANES = 128
V7X_SUBLANES = 8
V7X_VMEM_LIMIT_BYTES = 48 * 1024 * 1024

TOKEN_TILE = 512
MOE_TILE = 512
CONV_TILE = 256
ATTN_Q_TILE = 256
ATTN_K_TILE = 512
DECODE_PAGE_CHUNK = 8
SAMPLE_SEQ_BLOCK = 32
DMA_ISSUE_UNROLL = 8


def _cparams(sem):
    return pltpu.CompilerParams(dimension_semantics=sem, vmem_limit_bytes=V7X_VMEM_LIMIT_BYTES)


def _rms(x, g):
    ms = jnp.mean(x * x, axis=-1, keepdims=True)
    return x * lax.rsqrt(ms + RMS_EPS) * g


def _dot(a, b):
    return jnp.dot(a, b, preferred_element_type=F32)


def _dot_t(a, b):
    return lax.dot_general(a, b, (((1,), (1,)), ((), ())), preferred_element_type=F32)


def _sigmoid(x):
    return 1.0 / (1.0 + jnp.exp(-x))


def _pack_bf16_pairs(x):
    w = x.shape[1] // 2
    hi = lax.bitcast_convert_type(x[:, :w].astype(BF16).astype(F32), jnp.uint32)
    lo = lax.bitcast_convert_type(x[:, w:].astype(BF16).astype(F32), jnp.uint32)
    return hi | (lo >> 16)


def _unpack_bf16_pairs(p):
    hi = lax.bitcast_convert_type(p & jnp.uint32(0xFFFF0000), F32)
    lo = lax.bitcast_convert_type(p << 16, F32)
    return hi, lo


def _issue_rows(n_rows, issue):
    def group(g, c):
        for u in range(DMA_ISSUE_UNROLL):
            issue(g * DMA_ISSUE_UNROLL + u)
        return c

    lax.fori_loop(0, n_rows // DMA_ISSUE_UNROLL, group, 0)


def _prep_kernel(xp_ref, xs_ref, g_ref, h_ref, hn_ref, *, npt):
    i = pl.program_id(0)
    x = jnp.where(i < npt, xp_ref[...], xs_ref[...])
    h_ref[...] = x
    hn_ref[...] = _rms(x, g_ref[...]).astype(BF16)


def _prep(xp, xs, g, tm):
    np_, d = xp.shape
    ns = xs.shape[0]
    npt, nst = np_ // tm, ns // tm
    n = np_ + ns
    return pl.pallas_call(
        functools.partial(_prep_kernel, npt=npt),
        grid=(npt + nst,),
        in_specs=[
            pl.BlockSpec((tm, d), lambda i: (jnp.minimum(i, npt - 1), 0)),
            pl.BlockSpec((tm, d), lambda i: (jnp.maximum(i - npt, 0), 0)),
            pl.BlockSpec((1, d), lambda i: (0, 0)),
        ],
        out_specs=[pl.BlockSpec((tm, d), lambda i: (i, 0)),
                   pl.BlockSpec((tm, d), lambda i: (i, 0))],
        out_shape=[jax.ShapeDtypeStruct((n, d), F32), jax.ShapeDtypeStruct((n, d), BF16)],
        compiler_params=_cparams(("arbitrary",)),
        name="prep",
    )(xp, xs, g)


def _pool_prompt_kernel(hn_ref, win_ref, wg_ref, sc_ref, mo_ref, st_ref, ext_ref, *, tm, carry, state):
    j = pl.program_id(1)
    nj = pl.num_programs(1)
    d = hn_ref.shape[1]
    gd = d // len(POOL_WINDOWS)
    u = _dot(hn_ref[...], win_ref[...])

    @pl.when(j == 0)
    def _():
        ext_ref[0:carry, :] = jnp.zeros((carry, d), F32)

    @pl.when(j > 0)
    def _():
        ext_ref[0:carry, :] = ext_ref[tm:tm + carry, :]

    ext_ref[carry:carry + tm, :] = u
    pos = j * tm + lax.broadcasted_iota(jnp.int32, (tm, 1), 0)
    for g, w in enumerate(POOL_WINDOWS):
        c0, c1 = g * gd, (g + 1) * gd
        ug = u[:, c0:c1]
        s = ug
        for k in range(1, w):
            s = s + ext_ref[carry - k:carry - k + tm, c0:c1]
        cnt = jnp.minimum(pos + 1, w).astype(F32)
        z = s / cnt - ug
        y = _dot(z.astype(BF16), wg_ref[g])
        mo_ref[:, c0:c1] = y * sc_ref[:, c0:c1]

    @pl.when(j == nj - 1)
    def _():
        st_ref[0] = ext_ref[carry + tm - state:carry + tm, :]


def _pool_prompt(hn, w_in, w_grp, scale, b, t, tm):
    d = hn.shape[1]
    state = max(POOL_WINDOWS) - 1
    carry = 16
    ntj = t // tm
    return pl.pallas_call(
        functools.partial(_pool_prompt_kernel, tm=tm, carry=carry, state=state),
        grid=(b, ntj),
        in_specs=[
            pl.BlockSpec((tm, d), lambda bi, j: (bi * ntj + j, 0)),
            pl.BlockSpec((d, d), lambda bi, j: (0, 0)),
            pl.BlockSpec(w_grp.shape, lambda bi, j: (0, 0, 0)),
            pl.BlockSpec((1, d), lambda bi, j: (0, 0)),
        ],
        out_specs=[pl.BlockSpec((tm, d), lambda bi, j: (bi * ntj + j, 0)),
                   pl.BlockSpec((1, state, d), lambda bi, j: (bi, 0, 0))],
        out_shape=[jax.ShapeDtypeStruct((b * t, d), F32), jax.ShapeDtypeStruct((b, state, d), F32)],
        scratch_shapes=[pltpu.VMEM((carry + tm, d), F32)],
        compiler_params=_cparams(("arbitrary", "arbitrary")),
        name="pool_prompt",
    )(hn, w_in, w_grp, scale)


def _pool_sample_kernel(hn_ref, prev_ref, win_ref, wg_ref, sc_ref, mo_ref, st_ref, u_ref, z_ref,
                        *, bs, t, state, pos0):
    d = hn_ref.shape[1]
    gd = d // len(POOL_WINDOWS)
    u = _dot(hn_ref[...], win_ref[...])
    u_ref[...] = u.reshape(bs, t, d)

    def row(i, c0, c1):
        if i >= 0:
            return u_ref[:, i, c0:c1]
        return prev_ref[:, state + i, c0:c1]

    for g, w in enumerate(POOL_WINDOWS):
        c0, c1 = g * gd, (g + 1) * gd
        for ti in range(t):
            s = row(ti, c0, c1)
            for k in range(1, w):
                s = s + row(ti - k, c0, c1)
            cnt = float(min(pos0 + ti + 1, w))
            z_ref[:, ti, c0:c1] = s / cnt - row(ti, c0, c1)
    z = z_ref[...].reshape(bs * t, d)
    for g in range(len(POOL_WINDOWS)):
        c0, c1 = g * gd, (g + 1) * gd
        y = _dot(z[:, c0:c1].astype(BF16), wg_ref[g])
        mo_ref[:, c0:c1] = y * sc_ref[:, c0:c1]
    if t >= state:
        st_ref[...] = u_ref[:, t - state:t, :]
    else:
        st_ref[:, 0:state - t, :] = prev_ref[:, t:state, :]
        st_ref[:, state - t:state, :] = u_ref[...]


def _pool_sample(hn, row0, prev, w_in, w_grp, scale, b, t, pos0):
    d = hn.shape[1]
    state = prev.shape[1]
    bs = SAMPLE_SEQ_BLOCK
    blk0 = row0 // (bs * t)
    return pl.pallas_call(
        functools.partial(_pool_sample_kernel, bs=bs, t=t, state=state, pos0=pos0),
        grid=(b // bs,),
        in_specs=[
            pl.BlockSpec((bs * t, d), lambda i: (blk0 + i, 0)),
            pl.BlockSpec((bs, state, d), lambda i: (i, 0, 0)),
            pl.BlockSpec((d, d), lambda i: (0, 0)),
            pl.BlockSpec(w_grp.shape, lambda i: (0, 0, 0)),
            pl.BlockSpec((1, d), lambda i: (0, 0)),
        ],
        out_specs=[pl.BlockSpec((bs * t, d), lambda i: (i, 0)),
                   pl.BlockSpec((bs, state, d), lambda i: (i, 0, 0))],
        out_shape=[jax.ShapeDtypeStruct((b * t, d), F32), jax.ShapeDtypeStruct((b, state, d), F32)],
        scratch_shapes=[pltpu.VMEM((bs, t, d), F32), pltpu.VMEM((bs, t, d), F32)],
        compiler_params=_cparams(("arbitrary",)),
        name="pool_sample",
    )(hn, prev, w_in, w_grp, scale)


def _layernorm_silu(c, g, b):
    mu = jnp.mean(c, axis=-1, keepdims=True)
    xc = c - mu
    var = jnp.mean(xc * xc, axis=-1, keepdims=True)
    n = xc * lax.rsqrt(var + LN_EPS) * g + b
    return n * _sigmoid(n)


def _conv_prompt_kernel(hn_ref, w1_ref, b1_ref, wdw_ref, bdw_ref, lg_ref, lb_ref, w2_ref, b2_ref,
                        mo_ref, st_ref, ext_ref, c_ref, tap_ref, *, tm, carry, width, rc):
    j = pl.program_id(1)
    nj = pl.num_programs(1)
    d = hn_ref.shape[1]
    sub = V7X_SUBLANES

    @pl.when((pl.program_id(0) == 0) & (j == 0))
    def _():
        for k in range(width):
            tap_ref[k] = jnp.broadcast_to(wdw_ref[k:k + 1, :], (sub, d))

    a = _dot(hn_ref[...], w1_ref[...]) + b1_ref[...]
    glu = a[:, :d] * _sigmoid(a[:, d:])

    length = carry + tm

    @pl.when(j == 0)
    def _():
        ext_ref[0, 0:carry, :] = jnp.zeros((carry, d), F32)

    @pl.when(j > 0)
    def _():
        ext_ref[0, 0:carry, :] = ext_ref[0, tm:tm + carry, :]

    ext_ref[0, carry:carry + tm, :] = glu
    for m in range(1, V7X_SUBLANES):
        ext_ref[m, 0:length - V7X_SUBLANES, :] = ext_ref[0, m:length - V7X_SUBLANES + m, :]
    off = carry - (width - 1)

    def chunk(ci, _):
        r0 = pl.multiple_of(ci * rc, rc)
        acc = jnp.zeros((rc // sub, sub, d), F32) + bdw_ref[...]
        for k in range(width):
            q, m = divmod(off + k, sub)
            win = ext_ref[m, pl.ds(r0 + q * sub, rc), :].reshape(rc // sub, sub, d)
            acc = acc + win * tap_ref[k]
        c_ref[pl.ds(r0, rc), :] = acc.reshape(rc, d)
        return 0

    lax.fori_loop(0, tm // rc, chunk, 0)
    s = _layernorm_silu(c_ref[...], lg_ref[...], lb_ref[...])
    mo_ref[...] = _dot(s.astype(BF16), w2_ref[...]) + b2_ref[...]

    @pl.when(j == nj - 1)
    def _():
        st_ref[0] = ext_ref[0, carry + tm - (width - 1):carry + tm, :]


def _conv_prompt(hn, w1, b1, wdw, bdw, lg, lb, w2, b2, b, t, tm):
    d = hn.shape[1]
    width = wdw.shape[0]
    carry = 32
    ntj = t // tm
    const = lambda bi, j: (0, 0)
    return pl.pallas_call(
        functools.partial(_conv_prompt_kernel, tm=tm, carry=carry, width=width, rc=16),
        grid=(b, ntj),
        in_specs=[
            pl.BlockSpec((tm, d), lambda bi, j: (bi * ntj + j, 0)),
            pl.BlockSpec((d, 2 * d), const), pl.BlockSpec((1, 2 * d), const),
            pl.BlockSpec((width, d), const), pl.BlockSpec((1, d), const),
            pl.BlockSpec((1, d), const), pl.BlockSpec((1, d), const),
            pl.BlockSpec((d, d), const), pl.BlockSpec((1, d), const),
        ],
        out_specs=[pl.BlockSpec((tm, d), lambda bi, j: (bi * ntj + j, 0)),
                   pl.BlockSpec((1, width - 1, d), lambda bi, j: (bi, 0, 0))],
        out_shape=[jax.ShapeDtypeStruct((b * t, d), F32),
                   jax.ShapeDtypeStruct((b, width - 1, d), F32)],
        scratch_shapes=[pltpu.VMEM((V7X_SUBLANES, carry + tm, d), F32), pltpu.VMEM((tm, d), F32),
                        pltpu.VMEM((width, V7X_SUBLANES, d), F32)],
        compiler_params=_cparams(("arbitrary", "arbitrary")),
        name="conv_prompt",
    )(hn, w1, b1, wdw, bdw, lg, lb, w2, b2)


def _conv_sample_kernel(hn_ref, prev_ref, w1_ref, b1_ref, wdw_ref, bdw_ref, lg_ref, lb_ref, w2_ref,
                        b2_ref, mo_ref, st_ref, g_ref, c_ref, *, bs, t, width):
    d = hn_ref.shape[1]
    state = width - 1
    a = _dot(hn_ref[...], w1_ref[...]) + b1_ref[...]
    glu = a[:, :d] * _sigmoid(a[:, d:])
    g_ref[...] = glu.reshape(bs, t, d)

    def row(i):
        if i >= 0:
            return g_ref[:, i, :]
        return prev_ref[:, state + i, :]

    for ti in range(t):
        acc = jnp.zeros((bs, d), F32) + bdw_ref[...]
        for k in range(width):
            acc = acc + row(ti + k - state) * wdw_ref[k:k + 1, :]
        c_ref[:, ti, :] = acc
    s = _layernorm_silu(c_ref[...].reshape(bs * t, d), lg_ref[...], lb_ref[...])
    mo_ref[...] = _dot(s.astype(BF16), w2_ref[...]) + b2_ref[...]
    if t >= state:
        st_ref[...] = g_ref[:, t - state:t, :]
    else:
        st_ref[:, 0:state - t, :] = prev_ref[:, t:state, :]
        st_ref[:, state - t:state, :] = g_ref[...]


def _conv_sample(hn, row0, prev, w1, b1, wdw, bdw, lg, lb, w2, b2, b, t):
    d = hn.shape[1]
    width = wdw.shape[0]
    bs = SAMPLE_SEQ_BLOCK
    blk0 = row0 // (bs * t)
    const = lambda i: (0, 0)
    return pl.pallas_call(
        functools.partial(_conv_sample_kernel, bs=bs, t=t, width=width),
        grid=(b // bs,),
        in_specs=[
            pl.BlockSpec((bs * t, d), lambda i: (blk0 + i, 0)),
            pl.BlockSpec((bs, width - 1, d), lambda i: (i, 0, 0)),
            pl.BlockSpec((d, 2 * d), const), pl.BlockSpec((1, 2 * d), const),
            pl.BlockSpec((width, d), const), pl.BlockSpec((1, d), const),
            pl.BlockSpec((1, d), const), pl.BlockSpec((1, d), const),
            pl.BlockSpec((d, d), const), pl.BlockSpec((1, d), const),
        ],
        out_specs=[pl.BlockSpec((bs * t, d), lambda i: (i, 0)),
                   pl.BlockSpec((bs, width - 1, d), lambda i: (i, 0, 0))],
        out_shape=[jax.ShapeDtypeStruct((b * t, d), F32),
                   jax.ShapeDtypeStruct((b, width - 1, d), F32)],
        scratch_shapes=[pltpu.VMEM((bs, t, d), F32), pltpu.VMEM((bs, t, d), F32)],
        compiler_params=_cparams(("arbitrary",)),
        name="conv_sample",
    )(hn, prev, w1, b1, wdw, bdw, lg, lb, w2, b2)


def _mla_qkv_kernel(hn_ref, wd_ref, qn_ref, kvn_ref, wuq_ref, wuk_ref, cos_ref, sin_ref,
                    ckv_ref, kpe_ref, kcat_ref, qcat_ref, *, q_lora, kv_lora, rope, nh, nope, scale):
    lanes = V7X_LANES
    qk = kv_lora + lanes
    dq = _dot(hn_ref[...], wd_ref[...])
    cq = _rms(dq[:, :q_lora], qn_ref[...]).astype(BF16)
    ckv = _rms(dq[:, q_lora:q_lora + kv_lora], kvn_ref[...])
    o = q_lora + kv_lora
    cos = cos_ref[...]
    sin = sin_ref[...]
    kblk = dq[:, o:o + lanes] * cos + dq[:, o + lanes:o + 2 * lanes] * sin
    ckv_ref[...] = ckv
    kpe_ref[...] = kblk[:, :rope]
    kcat_ref[:, :kv_lora] = ckv.astype(BF16)
    kcat_ref[:, kv_lora:] = kblk.astype(BF16)
    q = _dot(cq, wuq_ref[...])
    qn = nh * nope
    for h in range(nh):
        ql = _dot(q[:, h * nope:(h + 1) * nope].astype(BF16), wuk_ref[h])
        r0 = qn + h * lanes
        r1 = qn + (nh + h) * lanes
        qpe = q[:, r0:r0 + lanes] * cos + q[:, r1:r1 + lanes] * sin
        qcat_ref[:, h * qk:h * qk + kv_lora] = (ql * scale).astype(qcat_ref.dtype)
        qcat_ref[:, h * qk + kv_lora:(h + 1) * qk] = (qpe * scale).astype(qcat_ref.dtype)


def _mla_qkv(hn, blk0, nrows, wd, qn, kvn, wuq, wuk, cos, sin, cos_map, tm, dims, q_dtype):
    q_lora, kv_lora, rope, nh, nope, scale = dims
    d = hn.shape[1]
    qk = kv_lora + V7X_LANES
    const2 = lambda i: (0, 0)
    return pl.pallas_call(
        functools.partial(_mla_qkv_kernel, q_lora=q_lora, kv_lora=kv_lora, rope=rope, nh=nh,
                          nope=nope, scale=scale),
        grid=(nrows // tm,),
        in_specs=[
            pl.BlockSpec((tm, d), lambda i: (blk0 + i, 0)),
            pl.BlockSpec(wd.shape, const2), pl.BlockSpec((1, q_lora), const2),
            pl.BlockSpec((1, kv_lora), const2), pl.BlockSpec(wuq.shape, const2),
            pl.BlockSpec(wuk.shape, lambda i: (0, 0, 0)),
            pl.BlockSpec((tm, V7X_LANES), cos_map), pl.BlockSpec((tm, V7X_LANES), cos_map),
        ],
        out_specs=[pl.BlockSpec((tm, kv_lora), lambda i: (i, 0)),
                   pl.BlockSpec((tm, rope), lambda i: (i, 0)),
                   pl.BlockSpec((tm, qk), lambda i: (i, 0)),
                   pl.BlockSpec((tm, nh * qk), lambda i: (i, 0))],
        out_shape=[jax.ShapeDtypeStruct((nrows, kv_lora), F32),
                   jax.ShapeDtypeStruct((nrows, rope), F32),
                   jax.ShapeDtypeStruct((nrows, qk), BF16),
                   jax.ShapeDtypeStruct((nrows, nh * qk), q_dtype)],
        compiler_params=_cparams(("arbitrary",)),
        name="mla_qkv",
    )(hn, wd, qn, kvn, wuq, wuk, cos, sin)


def _softmax_update(s, v, m_ref, l_ref, acc_ref):
    m_old = m_ref[...]
    m_new = jnp.maximum(m_old, jnp.max(s, axis=1, keepdims=True))
    alpha = jnp.exp(m_old - m_new)
    p = jnp.exp(s - m_new)
    l_ref[...] = alpha * l_ref[...] + jnp.sum(p, axis=1, keepdims=True)
    acc_ref[...] = alpha * acc_ref[...] + _dot(p.astype(BF16), v)
    m_ref[...] = m_new


def _attn_prompt_kernel(q_ref, k_ref, o_ref, m_ref, l_ref, acc_ref, *, tq, tk, nh, qk, kv):
    i = pl.program_id(1)
    m_ref[...] = jnp.full(m_ref.shape, NEG_INF, F32)
    l_ref[...] = jnp.zeros(l_ref.shape, F32)
    acc_ref[...] = jnp.zeros(acc_ref.shape, F32)
    qpos = i * tq + lax.broadcasted_iota(jnp.int32, (tq, 1), 0)
    n_full = (i * tq + 1) // tk
    n_chunks = (i * tq + tq - 1) // tk + 1

    def make_body(masked):
        def body(kk, c):
            ks = pl.multiple_of(kk * tk, tk)
            kc = k_ref[pl.ds(ks, tk), :]
            v = kc[:, :kv]
            if masked:
                vis = ks + lax.broadcasted_iota(jnp.int32, (tq, tk), 1) <= qpos
            scores = [_dot_t(q_ref[:, h * qk:(h + 1) * qk], kc) for h in range(nh)]
            for h in range(nh):
                s = jnp.where(vis, scores[h], NEG_INF) if masked else scores[h]
                _softmax_update(s, v, m_ref.at[h], l_ref.at[h], acc_ref.at[h])
            return c
        return body

    lax.fori_loop(0, n_full, make_body(False), 0)
    lax.fori_loop(n_full, n_chunks, make_body(True), 0)
    for h in range(nh):
        o_ref[:, h * kv:(h + 1) * kv] = (acc_ref[h] / l_ref[h]).astype(BF16)


def _attn_prompt(qcat, kcat, b, t, nh, kv):
    qk = kcat.shape[1]
    tq = min(ATTN_Q_TILE, t)
    tk = min(ATTN_K_TILE, t)
    nq = t // tq
    return pl.pallas_call(
        functools.partial(_attn_prompt_kernel, tq=tq, tk=tk, nh=nh, qk=qk, kv=kv),
        grid=(b, nq),
        in_specs=[
            pl.BlockSpec((tq, nh * qk), lambda bi, i: (bi * nq + i, 0)),
            pl.BlockSpec((t, qk), lambda bi, i: (bi, 0)),
        ],
        out_specs=pl.BlockSpec((tq, nh * kv), lambda bi, i: (bi * nq + i, 0)),
        out_shape=jax.ShapeDtypeStruct((b * t, nh * kv), BF16),
        scratch_shapes=[pltpu.VMEM((nh, tq, 1), F32), pltpu.VMEM((nh, tq, 1), F32),
                        pltpu.VMEM((nh, tq, kv), F32)],
        compiler_params=_cparams(("arbitrary", "arbitrary")),
        name="mla_attn_prompt",
    )(qcat, kcat)


def _page_copies(pt_ref, lat_hbm, kr_hbm, lat_buf, kr_buf, sem, layer, npages, seq, slot):
    def body(p, c):
        idx = pt_ref[seq * npages + p]
        pltpu.make_async_copy(lat_hbm.at[layer, idx], lat_buf.at[slot, p], sem.at[0, slot]).start()
        pltpu.make_async_copy(kr_hbm.at[layer, idx], kr_buf.at[slot, p], sem.at[1, slot]).start()
        return c

    lax.fori_loop(0, npages, body, 0, unroll=DMA_ISSUE_UNROLL)


def _attn_sample_kernel(pt_ref, q_ref, cn_ref, krn_ref, lat_hbm, kr_hbm, o_ref,
                        lat_buf, kr_buf, sem, s_ref, cb_ref,
                        *, layer, npages, chunk, t, nh, qk, kv, rope, page):
    bi = pl.program_id(0)
    nb = pl.num_programs(0)
    slot = bi % 2
    fetch = functools.partial(_page_copies, pt_ref, lat_hbm, kr_hbm, lat_buf, kr_buf, sem, layer, npages)

    @pl.when(bi == 0)
    def _():
        fetch(0, 0)

    @pl.when(bi + 1 < nb)
    def _():
        fetch(bi + 1, 1 - slot)

    q = jnp.concatenate([q_ref[:, h * qk:(h + 1) * qk] for h in range(nh)], axis=0).astype(BF16)
    ql = q[:, :kv]
    qp = q[:, kv:kv + rope]

    pltpu.make_async_copy(lat_hbm.at[layer, pl.ds(0, npages)], lat_buf.at[slot], sem.at[0, slot]).wait()
    pltpu.make_async_copy(kr_hbm.at[layer, pl.ds(0, npages)], kr_buf.at[slot], sem.at[1, slot]).wait()

    width = chunk * page
    n_chunks = npages // chunk
    for ch in range(n_chunks):
        c = lat_buf[slot, ch * chunk:(ch + 1) * chunk].reshape(width, kv).astype(BF16)
        cb_ref[ch * width:(ch + 1) * width, :] = c
        krt = jnp.concatenate([kr_buf[slot, ch * chunk + p] for p in range(chunk)], axis=1).astype(BF16)
        s_ref[:, ch * width:(ch + 1) * width] = _dot_t(ql, c) + _dot(qp, krt)

    rows = nh * t
    cn = jnp.concatenate([cn_ref[...], jnp.zeros((page - t, kv), F32)], axis=0).astype(BF16)
    krn = jnp.concatenate([krn_ref[...], jnp.zeros((page - t, rope), F32)], axis=0).astype(BF16)
    sn = _dot_t(ql, cn) + _dot_t(qp, krn)
    qstep = lax.broadcasted_iota(jnp.int32, (rows, 1), 0) & (t - 1)
    kstep = lax.broadcasted_iota(jnp.int32, (rows, page), 1)
    sn = jnp.where(kstep <= qstep, sn, NEG_INF)

    m = jnp.maximum(jnp.max(s_ref[...], axis=1, keepdims=True), jnp.max(sn, axis=1, keepdims=True))
    pn = jnp.exp(sn - m)
    l = jnp.sum(pn, axis=1, keepdims=True)
    acc = _dot(pn.astype(BF16), cn)
    for ch in range(n_chunks):
        p = jnp.exp(s_ref[:, ch * width:(ch + 1) * width] - m)
        l = l + jnp.sum(p, axis=1, keepdims=True)
        acc = acc + _dot(p.astype(BF16), cb_ref[ch * width:(ch + 1) * width, :])
    o = acc / l
    for h in range(nh):
        o_ref[:, h * kv:(h + 1) * kv] = o[h * t:(h + 1) * t, :]


def _attn_sample(page_table, qcat, cn, krn, cache_lat, cache_kr_t, layer, b, t, nh):
    kv = cn.shape[1]
    rope = krn.shape[1]
    qk = qcat.shape[1] // nh
    page = cache_lat.shape[2]
    npages = page_table.shape[1]
    chunk = min(DECODE_PAGE_CHUNK, npages)
    assert npages % chunk == 0 and t % V7X_SUBLANES == 0 and t <= page and t & (t - 1) == 0
    rows = t * nh
    grid_spec = pltpu.PrefetchScalarGridSpec(
        num_scalar_prefetch=1,
        grid=(b,),
        in_specs=[
            pl.BlockSpec((t, nh * qk), lambda bi, ptr: (bi, 0)),
            pl.BlockSpec((t, kv), lambda bi, ptr: (bi, 0)),
            pl.BlockSpec((t, rope), lambda bi, ptr: (bi, 0)),
            pl.BlockSpec(memory_space=pl.ANY),
            pl.BlockSpec(memory_space=pl.ANY),
        ],
        out_specs=pl.BlockSpec((t, nh * kv), lambda bi, ptr: (bi, 0)),
        scratch_shapes=[
            pltpu.VMEM((2, npages, page, kv), F32),
            pltpu.VMEM((2, npages, rope, page), F32),
            pltpu.SemaphoreType.DMA((2, 2)),
            pltpu.VMEM((rows, npages * page), F32), pltpu.VMEM((npages * page, kv), BF16),
        ],
    )
    return pl.pallas_call(
        functools.partial(_attn_sample_kernel, layer=layer, npages=npages, chunk=chunk, t=t, nh=nh,
                          qk=qk, kv=kv, rope=rope, page=page),
        grid_spec=grid_spec,
        out_shape=jax.ShapeDtypeStruct((b * t, nh * kv), F32),
        compiler_params=_cparams(("arbitrary",)),
        name="mla_attn_sample",
    )(page_table.reshape(-1), qcat, cn, krn, cache_lat, cache_kr_t)


def _mla_out_kernel(o_ref, wuv_ref, wo_ref, mo_ref, *, nh, kv):
    vs = [_dot(o_ref[:, h * kv:(h + 1) * kv].astype(BF16), wuv_ref[h]).astype(BF16) for h in range(nh)]
    mo_ref[...] = _dot(jnp.concatenate(vs, axis=1), wo_ref[...])


def _mla_out(o, wuv, wo, tm):
    n = o.shape[0]
    nh, kv, vd = wuv.shape
    d = wo.shape[1]
    return pl.pallas_call(
        functools.partial(_mla_out_kernel, nh=nh, kv=kv),
        grid=(n // tm,),
        in_specs=[pl.BlockSpec((tm, nh * kv), lambda i: (i, 0)),
                  pl.BlockSpec(wuv.shape, lambda i: (0, 0, 0)),
                  pl.BlockSpec(wo.shape, lambda i: (0, 0))],
        out_specs=pl.BlockSpec((tm, d), lambda i: (i, 0)),
        out_shape=jax.ShapeDtypeStruct((n, d), F32),
        compiler_params=_cparams(("arbitrary",)),
        name="mla_out",
    )(o, wuv, wo)


def _route_kernel(h_ref, mop_ref, mos_ref, g_ref, whi_ref, wlo_ref, br_ref,
                  h1_ref, xn_ref, slab_ref, cnt_ref, carry_ref, *, npt, tm, n_groups, epg):
    i = pl.program_id(0)

    @pl.when(i == 0)
    def _():
        carry_ref[...] = jnp.zeros(carry_ref.shape, F32)

    h1 = h_ref[...] + jnp.where(i < npt, mop_ref[...], mos_ref[...])
    h1_ref[...] = h1
    xn = _rms(h1, g_ref[...])
    xn_ref[...] = _pack_bf16_pairs(xn)
    x_hi = xn.astype(BF16)
    x_lo = (xn - x_hi.astype(F32)).astype(BF16)
    whi = whi_ref[...]
    logits = _dot(x_hi, whi) + _dot(x_lo, whi) + _dot(x_hi, wlo_ref[...]) + br_ref[...]

    lane = lax.broadcasted_iota(jnp.int32, (tm, V7X_LANES), 1).astype(F32)
    big = float(V7X_LANES)
    gl = jnp.where(lane < n_groups, logits, NEG_INF)
    gmax = jnp.max(gl, axis=1, keepdims=True)
    gsel = jnp.min(jnp.where(gl == gmax, lane, big), axis=1, keepdims=True)
    gprob = 1.0 / jnp.sum(jnp.exp(gl - gmax), axis=1, keepdims=True)
    lo = n_groups + gsel * epg
    el = jnp.where(lane >= lo, jnp.where(lane < lo + epg, logits, NEG_INF), NEG_INF)
    m1 = jnp.max(el, axis=1, keepdims=True)
    i1 = jnp.min(jnp.where(el == m1, lane, big), axis=1, keepdims=True)
    el2 = jnp.where(lane == i1, NEG_INF, el)
    m2 = jnp.max(el2, axis=1, keepdims=True)
    i2 = jnp.min(jnp.where(el2 == m2, lane, big), axis=1, keepdims=True)
    r = jnp.exp(m2 - m1)
    g1 = gprob / (1.0 + r)
    g2 = gprob * r / (1.0 + r)

    oh1 = jnp.where(lane == i1, 1.0, 0.0)
    oh2 = jnp.where(lane == i2, 1.0, 0.0)
    oh = oh1 + oh2
    rr = lax.broadcasted_iota(jnp.int32, (tm, tm), 0)
    cc = lax.broadcasted_iota(jnp.int32, (tm, tm), 1)
    tri = jnp.where(rr > cc, 1.0, 0.0).astype(BF16)
    carry = carry_ref[0:1, :]
    cum = _dot(tri, oh.astype(BF16)) + carry
    rank1 = jnp.sum(oh1 * cum, axis=1, keepdims=True)
    rank2 = jnp.sum(oh2 * cum, axis=1, keepdims=True)
    new_carry = carry + jnp.sum(oh, axis=0, keepdims=True)
    carry_ref[...] = jnp.broadcast_to(new_carry, carry_ref.shape)
    cnt_ref[...] = jnp.broadcast_to(new_carry, cnt_ref.shape)

    slab = jnp.where(lane == 0, i1 - n_groups, 0.0)
    slab = jnp.where(lane == 1, i2 - n_groups, slab)
    slab = jnp.where(lane == 2, g1, slab)
    slab = jnp.where(lane == 3, g2, slab)
    slab = jnp.where(lane == 4, rank1, slab)
    slab = jnp.where(lane == 5, rank2, slab)
    slab_ref[...] = slab


def _route(h, mo_p, mo_s, g, whi, wlo, br, tm, n_groups, epg):
    n, d = h.shape
    npt = mo_p.shape[0] // tm
    const = lambda i: (0, 0)
    return pl.pallas_call(
        functools.partial(_route_kernel, npt=npt, tm=tm, n_groups=n_groups, epg=epg),
        grid=(n // tm,),
        in_specs=[
            pl.BlockSpec((tm, d), lambda i: (i, 0)),
            pl.BlockSpec((tm, d), lambda i: (jnp.minimum(i, npt - 1), 0)),
            pl.BlockSpec((tm, d), lambda i: (jnp.maximum(i - npt, 0), 0)),
            pl.BlockSpec((1, d), const),
            pl.BlockSpec((d, V7X_LANES), const), pl.BlockSpec((d, V7X_LANES), const),
            pl.BlockSpec((1, V7X_LANES), const),
        ],
        out_specs=[pl.BlockSpec((tm, d), lambda i: (i, 0)),
                   pl.BlockSpec((tm, d // 2), lambda i: (i, 0)),
                   pl.BlockSpec((tm, V7X_LANES), lambda i: (i, 0)),
                   pl.BlockSpec((V7X_SUBLANES, V7X_LANES), const)],
        out_shape=[jax.ShapeDtypeStruct((n, d), F32), jax.ShapeDtypeStruct((n, d // 2), jnp.uint32),
                   jax.ShapeDtypeStruct((n, V7X_LANES), F32),
                   jax.ShapeDtypeStruct((V7X_SUBLANES, V7X_LANES), F32)],
        scratch_shapes=[pltpu.VMEM((V7X_SUBLANES, V7X_LANES), F32)],
        compiler_params=_cparams(("arbitrary",)),
        name="route",
    )(h, mo_p, mo_s, g, whi, wlo, br)


def _dispatch_tables(slab, cnt, n_experts, n_groups, tm, n_blk):
    counts = cnt[0, n_groups:n_groups + n_experts].astype(jnp.int32)
    padded = (counts + tm - 1) // tm * tm
    pad_ends = jnp.cumsum(padded)
    pad_starts = pad_ends - padded
    eids = jnp.arange(n_experts, dtype=jnp.int32)

    def dest(k):
        e = slab[:, k].astype(jnp.int32)
        start = jnp.sum(jnp.where(e[:, None] == eids[None, :], pad_starts[None, :], 0), axis=1)
        return start + slab[:, 4 + k].astype(jnp.int32)

    blk_start = jnp.arange(n_blk, dtype=jnp.int32) * tm
    blk_e = jnp.sum((blk_start[:, None] >= pad_ends[None, :]).astype(jnp.int32), axis=1)
    blk_e = jnp.minimum(blk_e, n_experts - 1)
    nblk = pad_ends[-1:] // tm
    return dest(0), dest(1), blk_e, nblk


def _dispatch_kernel(d0_ref, d1_ref, xn_ref, xs_in, xs_out, sem, *, tm):
    del xs_in
    i = pl.program_id(0)
    base = i * tm

    def issue(r):
        src = xn_ref.at[pl.ds(r, 1)]
        pltpu.make_async_copy(src, xs_out.at[pl.ds(d0_ref[base + r], 1)], sem).start()
        pltpu.make_async_copy(src, xs_out.at[pl.ds(d1_ref[base + r], 1)], sem).start()

    _issue_rows(tm, issue)
    for _ in range(TOP_K):
        pltpu.make_async_copy(xn_ref, xs_out.at[pl.ds(0, tm)], sem).wait()


def _dispatch(d0, d1, xn, xs_zero, tm):
    n, d = xn.shape
    grid_spec = pltpu.PrefetchScalarGridSpec(
        num_scalar_prefetch=2,
        grid=(n // tm,),
        in_specs=[pl.BlockSpec((tm, d), lambda i, a, b: (i, 0)),
                  pl.BlockSpec(memory_space=pl.ANY)],
        out_specs=pl.BlockSpec(memory_space=pl.ANY),
        scratch_shapes=[pltpu.SemaphoreType.DMA],
    )
    return pl.pallas_call(
        functools.partial(_dispatch_kernel, tm=tm),
        grid_spec=grid_spec,
        out_shape=jax.ShapeDtypeStruct(xs_zero.shape, xs_zero.dtype),
        input_output_aliases={3: 0},
        compiler_params=_cparams(("arbitrary",)),
        name="moe_dispatch",
    )(d0, d1, xn, xs_zero)


def _experts_kernel(be_ref, nb_ref, xs_ref, w1_ref, w3_ref, w2_ref, y_ref, w1b, w3b, w2b):
    i = pl.program_id(0)

    @pl.when(i < nb_ref[0])
    def _():
        e = be_ref[i]
        e_prev = be_ref[jnp.maximum(i - 1, 0)]

        @pl.when((i == 0) | (e != e_prev))
        def _():
            w1b[...] = w1_ref[...].astype(BF16)
            w3b[...] = w3_ref[...].astype(BF16)
            w2b[...] = w2_ref[...].astype(BF16)

        half = xs_ref.shape[1]
        rows = xs_ref.shape[0] // 2
        ups = []
        for r in range(2):
            xa, xb = _unpack_bf16_pairs(xs_ref[r * rows:(r + 1) * rows, :])
            xa = xa.astype(BF16)
            xb = xb.astype(BF16)
            a = _dot(xa, w1b[0:half, :]) + _dot(xb, w1b[half:, :])
            g = _dot(xa, w3b[0:half, :]) + _dot(xb, w3b[half:, :])
            ups.append((a, g))
        for r, (a, g) in enumerate(ups):
            hmid = (a * _sigmoid(a)) * g
            y_ref[r * rows:(r + 1) * rows, :] = _pack_bf16_pairs(_dot(hmid.astype(BF16), w2b[...]))


def _experts(blk_e, nblk, xs, w1, w3, w2, layer, tm):
    n_rows = xs.shape[0]
    d = w1.shape[-2]
    f = w1.shape[-1]
    wmap = lambda i, be, nb: (layer, be[i], 0, 0)
    xmap = lambda i, be, nb: (jnp.minimum(i, nb[0] - 1), 0)
    grid_spec = pltpu.PrefetchScalarGridSpec(
        num_scalar_prefetch=2,
        grid=(n_rows // tm,),
        in_specs=[
            pl.BlockSpec((tm, d // 2), xmap),
            pl.BlockSpec((None, None, d, f), wmap),
            pl.BlockSpec((None, None, d, f), wmap),
            pl.BlockSpec((None, None, f, d), wmap),
        ],
        out_specs=pl.BlockSpec((tm, d // 2), xmap),
        scratch_shapes=[pltpu.VMEM((d, f), BF16), pltpu.VMEM((d, f), BF16), pltpu.VMEM((f, d), BF16)],
    )
    return pl.pallas_call(
        _experts_kernel,
        grid_spec=grid_spec,
        out_shape=jax.ShapeDtypeStruct(xs.shape, xs.dtype),
        input_output_aliases={2: 0},
        compiler_params=_cparams(("arbitrary",)),
        name="moe_experts",
    )(blk_e, nblk, xs, w1, w3, w2)


def _combine_copies(d0_ref, d1_ref, y_hbm, ybuf, sem, tm, tile, slot):
    base = tile * tm

    def issue(r):
        pltpu.make_async_copy(y_hbm.at[pl.ds(d0_ref[base + r], 1)], ybuf.at[slot, 0, pl.ds(r, 1)],
                              sem.at[slot]).start()
        pltpu.make_async_copy(y_hbm.at[pl.ds(d1_ref[base + r], 1)], ybuf.at[slot, 1, pl.ds(r, 1)],
                              sem.at[slot]).start()

    _issue_rows(tm, issue)


def _ple_kernel(d0_ref, d1_ref, h1_ref, y_hbm, slab_ref, pp_ref, ps_ref, gple_ref, wg_ref, wp_ref,
                gn_ref, *rest, npt, last, tm):
    ybuf, sem = rest[-2:]
    outs = rest[:-2]
    i = pl.program_id(0)
    nt = pl.num_programs(0)
    slot = i % 2
    fetch = functools.partial(_combine_copies, d0_ref, d1_ref, y_hbm, ybuf, sem, tm)

    @pl.when(i == 0)
    def _():
        fetch(0, 0)

    @pl.when(i + 1 < nt)
    def _():
        fetch(i + 1, 1 - slot)

    for k in range(TOP_K):
        pltpu.make_async_copy(y_hbm.at[pl.ds(0, tm)], ybuf.at[slot, k], sem.at[slot]).wait()

    p = jnp.where(i < npt, pp_ref[...], ps_ref[...]).astype(BF16)
    proj = _dot(p, wp_ref[...])
    slab = slab_ref[...]
    y0 = jnp.concatenate(_unpack_bf16_pairs(ybuf[slot, 0]), axis=1)
    y1 = jnp.concatenate(_unpack_bf16_pairs(ybuf[slot, 1]), axis=1)
    h2 = h1_ref[...] + slab[:, 2:3] * y0 + slab[:, 3:4] * y1
    gate = _sigmoid(_dot(_rms(h2, gple_ref[...]).astype(BF16), wg_ref[...]))
    h3 = h2 + gate * proj
    hn = _rms(h3, gn_ref[...])
    if last:
        yp_ref, ys_ref = outs

        @pl.when(i < npt)
        def _():
            yp_ref[...] = hn

        @pl.when(i >= npt)
        def _():
            ys_ref[...] = hn
    else:
        h_ref, hn_ref = outs
        h_ref[...] = h3
        hn_ref[...] = hn.astype(BF16)


def _ple(d0, d1, h1, y, slab, p_p, p_s, layer, gple, wg, wp, gn, tm, last):
    n, d = h1.shape
    np_ = p_p.shape[1]
    npt = np_ // tm
    pd = p_p.shape[2]
    const = lambda i, a, b: (0, 0)
    rowmap = lambda i, a, b: (i, 0)
    if last:
        out_specs = [pl.BlockSpec((tm, d), lambda i, a, b: (jnp.minimum(i, npt - 1), 0)),
                     pl.BlockSpec((tm, d), lambda i, a, b: (jnp.maximum(i - npt, 0), 0))]
        out_shape = [jax.ShapeDtypeStruct((np_, d), F32), jax.ShapeDtypeStruct((n - np_, d), F32)]
    else:
        out_specs = [pl.BlockSpec((tm, d), rowmap), pl.BlockSpec((tm, d), rowmap)]
        out_shape = [jax.ShapeDtypeStruct((n, d), F32), jax.ShapeDtypeStruct((n, d), BF16)]
    grid_spec = pltpu.PrefetchScalarGridSpec(
        num_scalar_prefetch=2,
        grid=(n // tm,),
        in_specs=[
            pl.BlockSpec((tm, d), rowmap),
            pl.BlockSpec(memory_space=pl.ANY),
            pl.BlockSpec((tm, V7X_LANES), rowmap),
            pl.BlockSpec((None, tm, pd), lambda i, a, b: (layer, jnp.minimum(i, npt - 1), 0)),
            pl.BlockSpec((None, tm, pd), lambda i, a, b: (layer, jnp.maximum(i - npt, 0), 0)),
            pl.BlockSpec((1, d), const), pl.BlockSpec((d, d), const), pl.BlockSpec((pd, d), const),
            pl.BlockSpec((1, d), const),
        ],
        out_specs=out_specs,
        scratch_shapes=[pltpu.VMEM((2, TOP_K, tm, d // 2), jnp.uint32), pltpu.SemaphoreType.DMA((2,))],
    )
    return pl.pallas_call(
        functools.partial(_ple_kernel, npt=npt, last=last, tm=tm),
        grid_spec=grid_spec,
        out_shape=out_shape,
        compiler_params=_cparams(("arbitrary",)),
        name="ple",
    )(d0, d1, h1, y, slab, p_p, p_s, gple, wg, wp, gn)


def _rope_tables(pos, rope):
    half = rope // 2
    inv = 1.0 / (ROPE_THETA ** (jnp.arange(half, dtype=F32) * (2.0 / rope)))
    ang = pos.astype(F32)[:, None] * inv[None, :]
    pad = ((0, 0), (0, V7X_LANES - rope))
    cos = jnp.pad(jnp.tile(jnp.cos(ang), (1, 2)), pad)
    sin = jnp.pad(jnp.tile(jnp.sin(ang), (1, 2)), pad)
    return cos, sin


def _rot_last(w):
    half = w.shape[-1] // 2
    return jnp.concatenate([-w[..., half:], w[..., :half]], axis=-1)


def _pad_last(w, width):
    return jnp.pad(w, [(0, 0)] * (w.ndim - 1) + [(0, width - w.shape[-1])])


def kernel(x_prompt, x_sample, state_pool, state_conv, cache_latent, cache_krope, page_table, p_prompt, p_sample, norm_mix, norm_ffn, norm_ple, norm_final, pool_w_in, pool_w_grp, pool_scale, conv_w_pw1, conv_b_pw1, conv_w_dw, conv_b_dw, conv_ln_g, conv_ln_b, conv_w_pw2, conv_b_pw2, mla_w_dqkv, mla_q_norm, mla_kv_norm, mla_w_uq, mla_w_uk, mla_w_uv, mla_w_o, moe_w_group, moe_b_group, moe_w_expert, moe_b_expert, moe_w1, moe_w3, moe_w2, ple_w_gate, ple_w_proj):
    b, t, d = x_prompt.shape
    bs, ts, _ = x_sample.shape
    depth = norm_mix.shape[0]
    np_, ns = b * t, bs * ts
    n = np_ + ns
    tm = min(TOKEN_TILE, ns)
    assert np_ % tm == 0 and ns % tm == 0
    past_len = page_table.shape[1] * cache_latent.shape[2]
    n_groups = moe_w_group.shape[-1]
    n_experts = moe_w_expert.shape[-1]
    epg = n_experts // n_groups
    assert n_groups + n_experts <= V7X_LANES
    n_rows = (n * TOP_K + n_experts * (MOE_TILE - 1) + MOE_TILE - 1) // MOE_TILE * MOE_TILE
    pd = p_prompt.shape[-1]
    row = lambda v: v.reshape(1, -1)

    nh, kv_lora, nope = mla_w_uk.shape[1:]
    rope = cache_krope.shape[-1]
    q_lora = mla_w_dqkv.shape[-1] - kv_lora - rope
    scale = float((nope + rope) ** -0.5)
    mla_dims = (q_lora, kv_lora, rope, nh, nope, scale)
    cache_kr_t = jnp.swapaxes(cache_krope, 2, 3)

    h, hn = _prep(x_prompt.reshape(np_, d), x_sample.reshape(ns, d), row(norm_mix[0]), tm)
    p_p = p_prompt.reshape(depth, np_, pd)
    p_s = p_sample.reshape(depth, ns, pd)

    new_pool_p, new_pool_s, new_conv_p, new_conv_s = [], [], [], []
    new_lat_p, new_kr_p, new_lat_s, new_kr_s = [], [], [], []
    y_p = y_s = None
    for i in range(depth):
        kind, j = i % N_MIXERS, i // N_MIXERS
        if kind == 0:
            w_in = pool_w_in[j].astype(BF16)
            w_grp = pool_w_grp[j].astype(BF16)
            sc = row(pool_scale[j])
            mo_p, st_p = _pool_prompt(hn, w_in, w_grp, sc, b, t, min(tm, t))
            mo_s, st_s = _pool_sample(hn, np_, state_pool[j], w_in, w_grp, sc, bs, ts, past_len)
            new_pool_p.append(st_p)
            new_pool_s.append(st_s)
        elif kind == 1:
            w1 = conv_w_pw1[j].astype(BF16)
            w2 = conv_w_pw2[j].astype(BF16)
            args = (w1, row(conv_b_pw1[j]), conv_w_dw[j], row(conv_b_dw[j]), row(conv_ln_g[j]),
                    row(conv_ln_b[j]), w2, row(conv_b_pw2[j]))
            mo_p, st_p = _conv_prompt(hn, *args, b, t, min(CONV_TILE, t))
            mo_s, st_s = _conv_sample(hn, np_, state_conv[j], *args, bs, ts)
            new_conv_p.append(st_p)
            new_conv_s.append(st_s)
        else:
            wd = mla_w_dqkv[j]
            o = q_lora + kv_lora
            wd_k = wd[:, o:]
            wd = jnp.concatenate([wd[:, :o], _pad_last(wd_k, V7X_LANES),
                                  _pad_last(_rot_last(wd_k), V7X_LANES)], axis=1).astype(BF16)
            wuq = mla_w_uq[j].reshape(q_lora, nh, nope + rope)
            wq_rope = wuq[:, :, nope:]
            wuq = jnp.concatenate([
                wuq[:, :, :nope].reshape(q_lora, nh * nope),
                _pad_last(wq_rope, V7X_LANES).reshape(q_lora, nh * V7X_LANES),
                _pad_last(_rot_last(wq_rope), V7X_LANES).reshape(q_lora, nh * V7X_LANES)],
                axis=1).astype(BF16)
            wuk = jnp.swapaxes(mla_w_uk[j], 1, 2).astype(BF16)
            wuv = mla_w_uv[j].astype(BF16)
            wo = mla_w_o[j].astype(BF16)
            qn, kvn = row(mla_q_norm[j]), row(mla_kv_norm[j])
            tmp = min(tm, t)
            ntj = t // tmp
            cos_p, sin_p = _rope_tables(jnp.arange(t, dtype=jnp.int32), rope)
            ckv_p, kpe_p, kcat_p, qcat_p = _mla_qkv(
                hn, 0, np_, wd, qn, kvn, wuq, wuk, cos_p, sin_p, lambda ii: (ii % ntj, 0), tmp,
                mla_dims, BF16)
            cos_s, sin_s = _rope_tables(past_len + jnp.arange(ts, dtype=jnp.int32), rope)
            cos_s, sin_s = jnp.tile(cos_s, (bs, 1)), jnp.tile(sin_s, (bs, 1))
            ckv_s, kpe_s, _, qcat_s = _mla_qkv(
                hn, np_ // tm, ns, wd, qn, kvn, wuq, wuk, cos_s, sin_s, lambda ii: (ii, 0), tm,
                mla_dims, F32)
            o_p = _attn_prompt(qcat_p, kcat_p, b, t, nh, kv_lora)
            o_s = _attn_sample(page_table, qcat_s, ckv_s, kpe_s, cache_latent, cache_kr_t, j, bs, ts, nh)
            mo_p = _mla_out(o_p, wuv, wo, tm)
            mo_s = _mla_out(o_s, wuv, wo, tm)
            new_lat_p.append(ckv_p.reshape(b, t, kv_lora))
            new_kr_p.append(kpe_p.reshape(b, t, rope))
            new_lat_s.append(ckv_s.reshape(bs, ts, kv_lora))
            new_kr_s.append(kpe_s.reshape(bs, ts, rope))

        wr = _pad_last(jnp.concatenate([moe_w_group[i], moe_w_expert[i]], axis=1), V7X_LANES)
        br = _pad_last(jnp.concatenate([moe_b_group[i], moe_b_expert[i]])[None, :], V7X_LANES)
        whi = wr.astype(BF16)
        wlo = (wr - whi.astype(F32)).astype(BF16)
        h1, xn, slab, cnt = _route(h, mo_p, mo_s, row(norm_ffn[i]), whi, wlo, br, tm, n_groups, epg)
        d0, d1, blk_e, nblk = _dispatch_tables(slab, cnt, n_experts, n_groups, MOE_TILE, n_rows // MOE_TILE)
        xs = _dispatch(d0, d1, xn, jnp.zeros((n_rows, d // 2), jnp.uint32), tm)
        y = _experts(blk_e, nblk, xs, moe_w1, moe_w3, moe_w2, i, MOE_TILE)
        last = i == depth - 1
        gn = row(norm_final) if last else row(norm_mix[i + 1])
        out = _ple(d0, d1, h1, y, slab, p_p, p_s, i, row(norm_ple[i]), ple_w_gate[i].astype(BF16),
                   ple_w_proj[i].astype(BF16), gn, tm, last)
        if last:
            y_p, y_s = out
        else:
            h, hn = out

    return (y_p.reshape(b, t, d), y_s.reshape(bs, ts, d),
            jnp.stack(new_pool_p), jnp.stack(new_pool_s),
            jnp.stack(new_conv_p), jnp.stack(new_conv_s),
            jnp.stack(new_lat_p), jnp.stack(new_kr_p),
            jnp.stack(new_lat_s), jnp.stack(new_kr_s))
```

```python
import functools

import jax
import jax.numpy as jnp
from jax import lax
from jax.experimental import pallas as pl
from jax.experimental.pallas import tpu as pltpu

F32 = jnp.float32
BF16 = jnp.bfloat16

POOL_WINDOWS = (2, 4, 8, 16)
TOP_K = 2
ROPE_THETA = 10000.0
RMS_EPS = 1e-6
LN_EPS = 1e-5
NEG_INF = -1e30
N_MIXERS = 3

V7X_LANES = 128
V7X_SUBLANES = 8
V7X_VMEM_LIMIT_BYTES = 48 * 1024 * 1024

TOKEN_TILE = 512
MOE_TILE = 512
CONV_TILE = 256
ATTN_Q_TILE = 256
ATTN_K_TILE = 512
DECODE_PAGE_CHUNK = 8
SAMPLE_SEQ_BLOCK = 32
DMA_ISSUE_UNROLL = 8


def _cparams(sem):
    return pltpu.CompilerParams(dimension_semantics=sem, vmem_limit_bytes=V7X_VMEM_LIMIT_BYTES)


def _rms(x, g):
    ms = jnp.mean(x * x, axis=-1, keepdims=True)
    return x * lax.rsqrt(ms + RMS_EPS) * g


def _dot(a, b):
    return jnp.dot(a, b, preferred_element_type=F32)


def _dot_t(a, b):
    return lax.dot_general(a, b, (((1,), (1,)), ((), ())), preferred_element_type=F32)


def _sigmoid(x):
    return 1.0 / (1.0 + jnp.exp(-x))


def _pack_bf16_pairs(x):
    w = x.shape[1] // 2
    hi = lax.bitcast_convert_type(x[:, :w].astype(BF16).astype(F32), jnp.uint32)
    lo = lax.bitcast_convert_type(x[:, w:].astype(BF16).astype(F32), jnp.uint32)
    return hi | (lo >> 16)


def _unpack_bf16_pairs(p):
    hi = lax.bitcast_convert_type(p & jnp.uint32(0xFFFF0000), F32)
    lo = lax.bitcast_convert_type(p << 16, F32)
    return hi, lo


def _issue_rows(n_rows, issue):
    def group(g, c):
        for u in range(DMA_ISSUE_UNROLL):
            issue(g * DMA_ISSUE_UNROLL + u, u % 2)
        return c

    lax.fori_loop(0, n_rows // DMA_ISSUE_UNROLL, group, 0)


def _to_row_tiles(x):
    return [x[:, j * V7X_LANES:(j + 1) * V7X_LANES] for j in range(x.shape[1] // V7X_LANES)]


def _store_row_tiles(ref, r0, r1, x):
    for j, blk in enumerate(_to_row_tiles(x)):
        ref[r0:r1, j, :] = blk


def _load_row_tiles(ref, r0, r1):
    return jnp.concatenate([ref[r0:r1, j, :] for j in range(ref.shape[-2])], axis=1)


def _prep_kernel(xp_ref, xs_ref, g_ref, h_ref, hn_ref, *, npt):
    i = pl.program_id(0)
    x = jnp.where(i < npt, xp_ref[...], xs_ref[...])
    h_ref[...] = x
    hn_ref[...] = _rms(x, g_ref[...]).astype(BF16)


def _prep(xp, xs, g, tm):
    np_, d = xp.shape
    ns = xs.shape[0]
    npt, nst = np_ // tm, ns // tm
    n = np_ + ns
    return pl.pallas_call(
        functools.partial(_prep_kernel, npt=npt),
        grid=(npt + nst,),
        in_specs=[
            pl.BlockSpec((tm, d), lambda i: (jnp.minimum(i, npt - 1), 0)),
            pl.BlockSpec((tm, d), lambda i: (jnp.maximum(i - npt, 0), 0)),
            pl.BlockSpec((1, d), lambda i: (0, 0)),
        ],
        out_specs=[pl.BlockSpec((tm, d), lambda i: (i, 0)),
                   pl.BlockSpec((tm, d), lambda i: (i, 0))],
        out_shape=[jax.ShapeDtypeStruct((n, d), F32), jax.ShapeDtypeStruct((n, d), BF16)],
        compiler_params=_cparams(("arbitrary",)),
        name="prep",
    )(xp, xs, g)


def _pool_prompt_kernel(hn_ref, win_ref, wg_ref, sc_ref, mo_ref, st_ref, ext_ref, *, tm, carry, state):
    j = pl.program_id(1)
    nj = pl.num_programs(1)
    d = hn_ref.shape[1]
    gd = d // len(POOL_WINDOWS)
    u = _dot(hn_ref[...], win_ref[...])

    @pl.when(j == 0)
    def _():
        ext_ref[0:carry, :] = jnp.zeros((carry, d), F32)

    @pl.when(j > 0)
    def _():
        ext_ref[0:carry, :] = ext_ref[tm:tm + carry, :]

    ext_ref[carry:carry + tm, :] = u
    pos = j * tm + lax.broadcasted_iota(jnp.int32, (tm, 1), 0)
    for g, w in enumerate(POOL_WINDOWS):
        c0, c1 = g * gd, (g + 1) * gd
        ug = u[:, c0:c1]
        s = ug
        for k in range(1, w):
            s = s + ext_ref[carry - k:carry - k + tm, c0:c1]
        cnt = jnp.minimum(pos + 1, w).astype(F32)
        z = s / cnt - ug
        y = _dot(z.astype(BF16), wg_ref[g])
        mo_ref[:, c0:c1] = y * sc_ref[:, c0:c1]

    @pl.when(j == nj - 1)
    def _():
        st_ref[0] = ext_ref[carry + tm - state:carry + tm, :]


def _pool_prompt(hn, w_in, w_grp, scale, b, t, tm):
    d = hn.shape[1]
    state = max(POOL_WINDOWS) - 1
    carry = 16
    ntj = t // tm
    return pl.pallas_call(
        functools.partial(_pool_prompt_kernel, tm=tm, carry=carry, state=state),
        grid=(b, ntj),
        in_specs=[
            pl.BlockSpec((tm, d), lambda bi, j: (bi * ntj + j, 0)),
            pl.BlockSpec((d, d), lambda bi, j: (0, 0)),
            pl.BlockSpec(w_grp.shape, lambda bi, j: (0, 0, 0)),
            pl.BlockSpec((1, d), lambda bi, j: (0, 0)),
        ],
        out_specs=[pl.BlockSpec((tm, d), lambda bi, j: (bi * ntj + j, 0)),
                   pl.BlockSpec((1, state, d), lambda bi, j: (bi, 0, 0))],
        out_shape=[jax.ShapeDtypeStruct((b * t, d), F32), jax.ShapeDtypeStruct((b, state, d), F32)],
        scratch_shapes=[pltpu.VMEM((carry + tm, d), F32)],
        compiler_params=_cparams(("arbitrary", "arbitrary")),
        name="pool_prompt",
    )(hn, w_in, w_grp, scale)


def _pool_sample_kernel(hn_ref, prev_ref, win_ref, wg_ref, sc_ref, mo_ref, st_ref, u_ref, z_ref,
                        *, bs, t, state, pos0):
    d = hn_ref.shape[1]
    gd = d // len(POOL_WINDOWS)
    u = _dot(hn_ref[...], win_ref[...])
    u_ref[...] = u.reshape(bs, t, d)

    def row(i, c0, c1):
        if i >= 0:
            return u_ref[:, i, c0:c1]
        return prev_ref[:, state + i, c0:c1]

    for g, w in enumerate(POOL_WINDOWS):
        c0, c1 = g * gd, (g + 1) * gd
        for ti in range(t):
            s = row(ti, c0, c1)
            for k in range(1, w):
                s = s + row(ti - k, c0, c1)
            cnt = float(min(pos0 + ti + 1, w))
            z_ref[:, ti, c0:c1] = s / cnt - row(ti, c0, c1)
    z = z_ref[...].reshape(bs * t, d)
    for g in range(len(POOL_WINDOWS)):
        c0, c1 = g * gd, (g + 1) * gd
        y = _dot(z[:, c0:c1].astype(BF16), wg_ref[g])
        mo_ref[:, c0:c1] = y * sc_ref[:, c0:c1]
    if t >= state:
        st_ref[...] = u_ref[:, t - state:t, :]
    else:
        st_ref[:, 0:state - t, :] = prev_ref[:, t:state, :]
        st_ref[:, state - t:state, :] = u_ref[...]


def _pool_sample(hn, row0, prev, w_in, w_grp, scale, b, t, pos0):
    d = hn.shape[1]
    state = prev.shape[1]
    bs = SAMPLE_SEQ_BLOCK
    blk0 = row0 // (bs * t)
    return pl.pallas_call(
        functools.partial(_pool_sample_kernel, bs=bs, t=t, state=state, pos0=pos0),
        grid=(b // bs,),
        in_specs=[
            pl.BlockSpec((bs * t, d), lambda i: (blk0 + i, 0)),
            pl.BlockSpec((bs, state, d), lambda i: (i, 0, 0)),
            pl.BlockSpec((d, d), lambda i: (0, 0)),
            pl.BlockSpec(w_grp.shape, lambda i: (0, 0, 0)),
            pl.BlockSpec((1, d), lambda i: (0, 0)),
        ],
        out_specs=[pl.BlockSpec((bs * t, d), lambda i: (i, 0)),
                   pl.BlockSpec((bs, state, d), lambda i: (i, 0, 0))],
        out_shape=[jax.ShapeDtypeStruct((b * t, d), F32), jax.ShapeDtypeStruct((b, state, d), F32)],
        scratch_shapes=[pltpu.VMEM((bs, t, d), F32), pltpu.VMEM((bs, t, d), F32)],
        compiler_params=_cparams(("arbitrary",)),
        name="pool_sample",
    )(hn, prev, w_in, w_grp, scale)


def _layernorm_silu(c, g, b):
    mu = jnp.mean(c, axis=-1, keepdims=True)
    xc = c - mu
    var = jnp.mean(xc * xc, axis=-1, keepdims=True)
    n = xc * lax.rsqrt(var + LN_EPS) * g + b
    return n * _sigmoid(n)


def _conv_prompt_kernel(hn_ref, w1_ref, b1_ref, wdw_ref, bdw_ref, lg_ref, lb_ref, w2_ref, b2_ref,
                        mo_ref, st_ref, ext_ref, c_ref, tap_ref, *, tm, carry, width, rc):
    j = pl.program_id(1)
    nj = pl.num_programs(1)
    d = hn_ref.shape[1]
    sub = V7X_SUBLANES

    @pl.when((pl.program_id(0) == 0) & (j == 0))
    def _():
        for k in range(width):
            tap_ref[k] = jnp.broadcast_to(wdw_ref[k:k + 1, :], (sub, d))

    a = _dot(hn_ref[...], w1_ref[...]) + b1_ref[...]
    glu = a[:, :d] * _sigmoid(a[:, d:])

    length = carry + tm

    @pl.when(j == 0)
    def _():
        ext_ref[0, 0:carry, :] = jnp.zeros((carry, d), F32)

    @pl.when(j > 0)
    def _():
        ext_ref[0, 0:carry, :] = ext_ref[0, tm:tm + carry, :]

    ext_ref[0, carry:carry + tm, :] = glu
    for m in range(1, V7X_SUBLANES):
        ext_ref[m, 0:length - V7X_SUBLANES, :] = ext_ref[0, m:length - V7X_SUBLANES + m, :]
    off = carry - (width - 1)

    def chunk(ci, _):
        r0 = pl.multiple_of(ci * rc, rc)
        acc = jnp.zeros((rc // sub, sub, d), F32) + bdw_ref[...]
        for k in range(width):
            q, m = divmod(off + k, sub)
            win = ext_ref[m, pl.ds(r0 + q * sub, rc), :].reshape(rc // sub, sub, d)
            acc = acc + win * tap_ref[k]
        c_ref[pl.ds(r0, rc), :] = acc.reshape(rc, d)
        return 0

    lax.fori_loop(0, tm // rc, chunk, 0)
    s = _layernorm_silu(c_ref[...], lg_ref[...], lb_ref[...])
    mo_ref[...] = _dot(s.astype(BF16), w2_ref[...]) + b2_ref[...]

    @pl.when(j == nj - 1)
    def _():
        st_ref[0] = ext_ref[0, carry + tm - (width - 1):carry + tm, :]


def _conv_prompt(hn, w1, b1, wdw, bdw, lg, lb, w2, b2, b, t, tm):
    d = hn.shape[1]
    width = wdw.shape[0]
    carry = 32
    ntj = t // tm
    const = lambda bi, j: (0, 0)
    return pl.pallas_call(
        functools.partial(_conv_prompt_kernel, tm=tm, carry=carry, width=width, rc=16),
        grid=(b, ntj),
        in_specs=[
            pl.BlockSpec((tm, d), lambda bi, j: (bi * ntj + j, 0)),
            pl.BlockSpec((d, 2 * d), const), pl.BlockSpec((1, 2 * d), const),
            pl.BlockSpec((width, d), const), pl.BlockSpec((1, d), const),
            pl.BlockSpec((1, d), const), pl.BlockSpec((1, d), const),
            pl.BlockSpec((d, d), const), pl.BlockSpec((1, d), const),
        ],
        out_specs=[pl.BlockSpec((tm, d), lambda bi, j: (bi * ntj + j, 0)),
                   pl.BlockSpec((1, width - 1, d), lambda bi, j: (bi, 0, 0))],
        out_shape=[jax.ShapeDtypeStruct((b * t, d), F32),
                   jax.ShapeDtypeStruct((b, width - 1, d), F32)],
        scratch_shapes=[pltpu.VMEM((V7X_SUBLANES, carry + tm, d), F32), pltpu.VMEM((tm, d), F32),
                        pltpu.VMEM((width, V7X_SUBLANES, d), F32)],
        compiler_params=_cparams(("arbitrary", "arbitrary")),
        name="conv_prompt",
    )(hn, w1, b1, wdw, bdw, lg, lb, w2, b2)


def _conv_sample_kernel(hn_ref, prev_ref, w1_ref, b1_ref, wdw_ref, bdw_ref, lg_ref, lb_ref, w2_ref,
                        b2_ref, mo_ref, st_ref, g_ref, c_ref, *, bs, t, width):
    d = hn_ref.shape[1]
    state = width - 1
    a = _dot(hn_ref[...], w1_ref[...]) + b1_ref[...]
    glu = a[:, :d] * _sigmoid(a[:, d:])
    g_ref[...] = glu.reshape(bs, t, d)

    def row(i):
        if i >= 0:
            return g_ref[:, i, :]
        return prev_ref[:, state + i, :]

    for ti in range(t):
        acc = jnp.zeros((bs, d), F32) + bdw_ref[...]
        for k in range(width):
            acc = acc + row(ti + k - state) * wdw_ref[k:k + 1, :]
        c_ref[:, ti, :] = acc
    s = _layernorm_silu(c_ref[...].reshape(bs * t, d), lg_ref[...], lb_ref[...])
    mo_ref[...] = _dot(s.astype(BF16), w2_ref[...]) + b2_ref[...]
    if t >= state:
        st_ref[...] = g_ref[:, t - state:t, :]
    else:
        st_ref[:, 0:state - t, :] = prev_ref[:, t:state, :]
        st_ref[:, state - t:state, :] = g_ref[...]


def _conv_sample(hn, row0, prev, w1, b1, wdw, bdw, lg, lb, w2, b2, b, t):
    d = hn.shape[1]
    width = wdw.shape[0]
    bs = SAMPLE_SEQ_BLOCK
    blk0 = row0 // (bs * t)
    const = lambda i: (0, 0)
    return pl.pallas_call(
        functools.partial(_conv_sample_kernel, bs=bs, t=t, width=width),
        grid=(b // bs,),
        in_specs=[
            pl.BlockSpec((bs * t, d), lambda i: (blk0 + i, 0)),
            pl.BlockSpec((bs, width - 1, d), lambda i: (i, 0, 0)),
            pl.BlockSpec((d, 2 * d), const), pl.BlockSpec((1, 2 * d), const),
            pl.BlockSpec((width, d), const), pl.BlockSpec((1, d), const),
            pl.BlockSpec((1, d), const), pl.BlockSpec((1, d), const),
            pl.BlockSpec((d, d), const), pl.BlockSpec((1, d), const),
        ],
        out_specs=[pl.BlockSpec((bs * t, d), lambda i: (i, 0)),
                   pl.BlockSpec((bs, width - 1, d), lambda i: (i, 0, 0))],
        out_shape=[jax.ShapeDtypeStruct((b * t, d), F32),
                   jax.ShapeDtypeStruct((b, width - 1, d), F32)],
        scratch_shapes=[pltpu.VMEM((bs, t, d), F32), pltpu.VMEM((bs, t, d), F32)],
        compiler_params=_cparams(("arbitrary",)),
        name="conv_sample",
    )(hn, prev, w1, b1, wdw, bdw, lg, lb, w2, b2)


def _mla_qkv_kernel(hn_ref, wd_ref, qn_ref, kvn_ref, wuq_ref, wuk_ref, cos_ref, sin_ref,
                    ckv_ref, kpe_ref, kcat_ref, qcat_ref, *, q_lora, kv_lora, rope, nh, nope, scale):
    lanes = V7X_LANES
    qk = kv_lora + lanes
    dq = _dot(hn_ref[...], wd_ref[...])
    cq = _rms(dq[:, :q_lora], qn_ref[...]).astype(BF16)
    ckv = _rms(dq[:, q_lora:q_lora + kv_lora], kvn_ref[...])
    o = q_lora + kv_lora
    cos = cos_ref[...]
    sin = sin_ref[...]
    kblk = dq[:, o:o + lanes] * cos + dq[:, o + lanes:o + 2 * lanes] * sin
    ckv_ref[...] = ckv
    kpe_ref[...] = kblk[:, :rope]
    kcat_ref[:, :kv_lora] = ckv.astype(BF16)
    kcat_ref[:, kv_lora:] = kblk.astype(BF16)
    q = _dot(cq, wuq_ref[...])
    qn = nh * nope
    for h in range(nh):
        ql = _dot(q[:, h * nope:(h + 1) * nope].astype(BF16), wuk_ref[h])
        r0 = qn + h * lanes
        r1 = qn + (nh + h) * lanes
        qpe = q[:, r0:r0 + lanes] * cos + q[:, r1:r1 + lanes] * sin
        qcat_ref[:, h * qk:h * qk + kv_lora] = (ql * scale).astype(qcat_ref.dtype)
        qcat_ref[:, h * qk + kv_lora:(h + 1) * qk] = (qpe * scale).astype(qcat_ref.dtype)


def _mla_qkv(hn, blk0, nrows, wd, qn, kvn, wuq, wuk, cos, sin, cos_map, tm, dims, q_dtype):
    q_lora, kv_lora, rope, nh, nope, scale = dims
    d = hn.shape[1]
    qk = kv_lora + V7X_LANES
    const2 = lambda i: (0, 0)
    return pl.pallas_call(
        functools.partial(_mla_qkv_kernel, q_lora=q_lora, kv_lora=kv_lora, rope=rope, nh=nh,
                          nope=nope, scale=scale),
        grid=(nrows // tm,),
        in_specs=[
            pl.BlockSpec((tm, d), lambda i: (blk0 + i, 0)),
            pl.BlockSpec(wd.shape, const2), pl.BlockSpec((1, q_lora), const2),
            pl.BlockSpec((1, kv_lora), const2), pl.BlockSpec(wuq.shape, const2),
            pl.BlockSpec(wuk.shape, lambda i: (0, 0, 0)),
            pl.BlockSpec((tm, V7X_LANES), cos_map), pl.BlockSpec((tm, V7X_LANES), cos_map),
        ],
        out_specs=[pl.BlockSpec((tm, kv_lora), lambda i: (i, 0)),
                   pl.BlockSpec((tm, rope), lambda i: (i, 0)),
                   pl.BlockSpec((tm, qk), lambda i: (i, 0)),
                   pl.BlockSpec((tm, nh * qk), lambda i: (i, 0))],
        out_shape=[jax.ShapeDtypeStruct((nrows, kv_lora), F32),
                   jax.ShapeDtypeStruct((nrows, rope), F32),
                   jax.ShapeDtypeStruct((nrows, qk), BF16),
                   jax.ShapeDtypeStruct((nrows, nh * qk), q_dtype)],
        compiler_params=_cparams(("arbitrary",)),
        name="mla_qkv",
    )(hn, wd, qn, kvn, wuq, wuk, cos, sin)


def _softmax_update(s, v, m_ref, l_ref, acc_ref):
    m_old = m_ref[...]
    m_new = jnp.maximum(m_old, jnp.max(s, axis=1, keepdims=True))
    alpha = jnp.exp(m_old - m_new)
    p = jnp.exp(s - m_new)
    l_ref[...] = alpha * l_ref[...] + jnp.sum(p, axis=1, keepdims=True)
    acc_ref[...] = alpha * acc_ref[...] + _dot(p.astype(BF16), v)
    m_ref[...] = m_new


def _attn_prompt_kernel(q_ref, k_ref, o_ref, m_ref, l_ref, acc_ref, *, tq, tk, nh, qk, kv):
    i = pl.program_id(1)
    m_ref[...] = jnp.full(m_ref.shape, NEG_INF, F32)
    l_ref[...] = jnp.zeros(l_ref.shape, F32)
    acc_ref[...] = jnp.zeros(acc_ref.shape, F32)
    qpos = i * tq + lax.broadcasted_iota(jnp.int32, (tq, 1), 0)
    n_full = (i * tq + 1) // tk
    n_chunks = (i * tq + tq - 1) // tk + 1

    def make_body(masked):
        def body(kk, c):
            ks = pl.multiple_of(kk * tk, tk)
            kc = k_ref[pl.ds(ks, tk), :]
            v = kc[:, :kv]
            if masked:
                vis = ks + lax.broadcasted_iota(jnp.int32, (tq, tk), 1) <= qpos
            scores = [_dot_t(q_ref[:, h * qk:(h + 1) * qk], kc) for h in range(nh)]
            for h in range(nh):
                s = jnp.where(vis, scores[h], NEG_INF) if masked else scores[h]
                _softmax_update(s, v, m_ref.at[h], l_ref.at[h], acc_ref.at[h])
            return c
        return body

    lax.fori_loop(0, n_full, make_body(False), 0)
    lax.fori_loop(n_full, n_chunks, make_body(True), 0)
    for h in range(nh):
        o_ref[:, h * kv:(h + 1) * kv] = (acc_ref[h] / l_ref[h]).astype(BF16)


def _attn_prompt(qcat, kcat, b, t, nh, kv):
    qk = kcat.shape[1]
    tq = min(ATTN_Q_TILE, t)
    tk = min(ATTN_K_TILE, t)
    nq = t // tq
    return pl.pallas_call(
        functools.partial(_attn_prompt_kernel, tq=tq, tk=tk, nh=nh, qk=qk, kv=kv),
        grid=(b, nq),
        in_specs=[
            pl.BlockSpec((tq, nh * qk), lambda bi, i: (bi * nq + i, 0)),
            pl.BlockSpec((t, qk), lambda bi, i: (bi, 0)),
        ],
        out_specs=pl.BlockSpec((tq, nh * kv), lambda bi, i: (bi * nq + i, 0)),
        out_shape=jax.ShapeDtypeStruct((b * t, nh * kv), BF16),
        scratch_shapes=[pltpu.VMEM((nh, tq, 1), F32), pltpu.VMEM((nh, tq, 1), F32),
                        pltpu.VMEM((nh, tq, kv), F32)],
        compiler_params=_cparams(("arbitrary", "arbitrary")),
        name="mla_attn_prompt",
    )(qcat, kcat)


def _page_copies(pt_ref, lat_hbm, kr_hbm, lat_buf, kr_buf, sem, layer, npages, seq, slot):
    def body(p, c):
        idx = pt_ref[seq * npages + p]
        pltpu.make_async_copy(lat_hbm.at[layer, idx], lat_buf.at[slot, p], sem.at[0, slot]).start()
        pltpu.make_async_copy(kr_hbm.at[layer, idx], kr_buf.at[slot, p], sem.at[1, slot]).start()
        return c

    lax.fori_loop(0, npages, body, 0, unroll=DMA_ISSUE_UNROLL)


def _attn_sample_kernel(pt_ref, q_ref, cn_ref, krn_ref, lat_hbm, kr_hbm, o_ref,
                        lat_buf, kr_buf, sem, s_ref, cb_ref,
                        *, layer, npages, chunk, t, nh, qk, kv, rope, page):
    bi = pl.program_id(0)
    nb = pl.num_programs(0)
    slot = bi % 2
    fetch = functools.partial(_page_copies, pt_ref, lat_hbm, kr_hbm, lat_buf, kr_buf, sem, layer, npages)

    @pl.when(bi == 0)
    def _():
        fetch(0, 0)

    @pl.when(bi + 1 < nb)
    def _():
        fetch(bi + 1, 1 - slot)

    q = jnp.concatenate([q_ref[:, h * qk:(h + 1) * qk] for h in range(nh)], axis=0).astype(BF16)
    ql = q[:, :kv]
    qp = q[:, kv:kv + rope]

    pltpu.make_async_copy(lat_hbm.at[layer, pl.ds(0, npages)], lat_buf.at[slot], sem.at[0, slot]).wait()
    pltpu.make_async_copy(kr_hbm.at[layer, pl.ds(0, npages)], kr_buf.at[slot], sem.at[1, slot]).wait()

    width = chunk * page
    n_chunks = npages // chunk
    for ch in range(n_chunks):
        c = lat_buf[slot, ch * chunk:(ch + 1) * chunk].reshape(width, kv).astype(BF16)
        cb_ref[ch * width:(ch + 1) * width, :] = c
        krt = jnp.concatenate([kr_buf[slot, ch * chunk + p] for p in range(chunk)], axis=1).astype(BF16)
        s_ref[:, ch * width:(ch + 1) * width] = _dot_t(ql, c) + _dot(qp, krt)

    rows = nh * t
    cn = jnp.concatenate([cn_ref[...], jnp.zeros((page - t, kv), F32)], axis=0).astype(BF16)
    krn = jnp.concatenate([krn_ref[...], jnp.zeros((page - t, rope), F32)], axis=0).astype(BF16)
    sn = _dot_t(ql, cn) + _dot_t(qp, krn)
    qstep = lax.broadcasted_iota(jnp.int32, (rows, 1), 0) & (t - 1)
    kstep = lax.broadcasted_iota(jnp.int32, (rows, page), 1)
    sn = jnp.where(kstep <= qstep, sn, NEG_INF)

    m = jnp.maximum(jnp.max(s_ref[...], axis=1, keepdims=True), jnp.max(sn, axis=1, keepdims=True))
    pn = jnp.exp(sn - m)
    l = jnp.sum(pn, axis=1, keepdims=True)
    acc = _dot(pn.astype(BF16), cn)
    for ch in range(n_chunks):
        p = jnp.exp(s_ref[:, ch * width:(ch + 1) * width] - m)
        l = l + jnp.sum(p, axis=1, keepdims=True)
        acc = acc + _dot(p.astype(BF16), cb_ref[ch * width:(ch + 1) * width, :])
    o = acc / l
    for h in range(nh):
        o_ref[:, h * kv:(h + 1) * kv] = o[h * t:(h + 1) * t, :]


def _attn_sample(page_table, qcat, cn, krn, cache_lat, cache_kr_t, layer, b, t, nh):
    kv = cn.shape[1]
    rope = krn.shape[1]
    qk = qcat.shape[1] // nh
    page = cache_lat.shape[2]
    npages = page_table.shape[1]
    chunk = min(DECODE_PAGE_CHUNK, npages)
    assert npages % chunk == 0 and t % V7X_SUBLANES == 0 and t <= page and t & (t - 1) == 0
    rows = t * nh
    grid_spec = pltpu.PrefetchScalarGridSpec(
        num_scalar_prefetch=1,
        grid=(b,),
        in_specs=[
            pl.BlockSpec((t, nh * qk), lambda bi, ptr: (bi, 0)),
            pl.BlockSpec((t, kv), lambda bi, ptr: (bi, 0)),
            pl.BlockSpec((t, rope), lambda bi, ptr: (bi, 0)),
            pl.BlockSpec(memory_space=pl.ANY),
            pl.BlockSpec(memory_space=pl.ANY),
        ],
        out_specs=pl.BlockSpec((t, nh * kv), lambda bi, ptr: (bi, 0)),
        scratch_shapes=[
            pltpu.VMEM((2, npages, page, kv), F32),
            pltpu.VMEM((2, npages, rope, page), F32),
            pltpu.SemaphoreType.DMA((2, 2)),
            pltpu.VMEM((rows, npages * page), F32), pltpu.VMEM((npages * page, kv), BF16),
        ],
    )
    return pl.pallas_call(
        functools.partial(_attn_sample_kernel, layer=layer, npages=npages, chunk=chunk, t=t, nh=nh,
                          qk=qk, kv=kv, rope=rope, page=page),
        grid_spec=grid_spec,
        out_shape=jax.ShapeDtypeStruct((b * t, nh * kv), F32),
        compiler_params=_cparams(("arbitrary",)),
        name="mla_attn_sample",
    )(page_table.reshape(-1), qcat, cn, krn, cache_lat, cache_kr_t)


def _mla_out_kernel(o_ref, wuv_ref, wo_ref, mo_ref, *, nh, kv):
    vs = [_dot(o_ref[:, h * kv:(h + 1) * kv].astype(BF16), wuv_ref[h]).astype(BF16) for h in range(nh)]
    mo_ref[...] = _dot(jnp.concatenate(vs, axis=1), wo_ref[...])


def _mla_out(o, wuv, wo, tm):
    n = o.shape[0]
    nh, kv, vd = wuv.shape
    d = wo.shape[1]
    return pl.pallas_call(
        functools.partial(_mla_out_kernel, nh=nh, kv=kv),
        grid=(n // tm,),
        in_specs=[pl.BlockSpec((tm, nh * kv), lambda i: (i, 0)),
                  pl.BlockSpec(wuv.shape, lambda i: (0, 0, 0)),
                  pl.BlockSpec(wo.shape, lambda i: (0, 0))],
        out_specs=pl.BlockSpec((tm, d), lambda i: (i, 0)),
        out_shape=jax.ShapeDtypeStruct((n, d), F32),
        compiler_params=_cparams(("arbitrary",)),
        name="mla_out",
    )(o, wuv, wo)


def _route_kernel(h_ref, mop_ref, mos_ref, g_ref, whi_ref, wlo_ref, br_ref,
                  h1_ref, xn_ref, slab_ref, cnt_ref, carry_ref, *, npt, tm, n_groups, epg):
    i = pl.program_id(0)

    @pl.when(i == 0)
    def _():
        carry_ref[...] = jnp.zeros(carry_ref.shape, F32)

    h1 = h_ref[...] + jnp.where(i < npt, mop_ref[...], mos_ref[...])
    h1_ref[...] = h1
    xn = _rms(h1, g_ref[...])
    _store_row_tiles(xn_ref, 0, tm, _pack_bf16_pairs(xn))
    x_hi = xn.astype(BF16)
    x_lo = (xn - x_hi.astype(F32)).astype(BF16)
    whi = whi_ref[...]
    logits = _dot(x_hi, whi) + _dot(x_lo, whi) + _dot(x_hi, wlo_ref[...]) + br_ref[...]

    lane = lax.broadcasted_iota(jnp.int32, (tm, V7X_LANES), 1).astype(F32)
    big = float(V7X_LANES)
    gl = jnp.where(lane < n_groups, logits, NEG_INF)
    gmax = jnp.max(gl, axis=1, keepdims=True)
    gsel = jnp.min(jnp.where(gl == gmax, lane, big), axis=1, keepdims=True)
    gprob = 1.0 / jnp.sum(jnp.exp(gl - gmax), axis=1, keepdims=True)
    lo = n_groups + gsel * epg
    el = jnp.where(lane >= lo, jnp.where(lane < lo + epg, logits, NEG_INF), NEG_INF)
    m1 = jnp.max(el, axis=1, keepdims=True)
    i1 = jnp.min(jnp.where(el == m1, lane, big), axis=1, keepdims=True)
    el2 = jnp.where(lane == i1, NEG_INF, el)
    m2 = jnp.max(el2, axis=1, keepdims=True)
    i2 = jnp.min(jnp.where(el2 == m2, lane, big), axis=1, keepdims=True)
    r = jnp.exp(m2 - m1)
    g1 = gprob / (1.0 + r)
    g2 = gprob * r / (1.0 + r)

    oh1 = jnp.where(lane == i1, 1.0, 0.0)
    oh2 = jnp.where(lane == i2, 1.0, 0.0)
    oh = oh1 + oh2
    rr = lax.broadcasted_iota(jnp.int32, (tm, tm), 0)
    cc = lax.broadcasted_iota(jnp.int32, (tm, tm), 1)
    tri = jnp.where(rr > cc, 1.0, 0.0).astype(BF16)
    carry = carry_ref[0:1, :]
    cum = _dot(tri, oh.astype(BF16)) + carry
    rank1 = jnp.sum(oh1 * cum, axis=1, keepdims=True)
    rank2 = jnp.sum(oh2 * cum, axis=1, keepdims=True)
    new_carry = carry + jnp.sum(oh, axis=0, keepdims=True)
    carry_ref[...] = jnp.broadcast_to(new_carry, carry_ref.shape)
    cnt_ref[...] = jnp.broadcast_to(new_carry, cnt_ref.shape)

    slab = jnp.where(lane == 0, i1 - n_groups, 0.0)
    slab = jnp.where(lane == 1, i2 - n_groups, slab)
    slab = jnp.where(lane == 2, g1, slab)
    slab = jnp.where(lane == 3, g2, slab)
    slab = jnp.where(lane == 4, rank1, slab)
    slab = jnp.where(lane == 5, rank2, slab)
    slab_ref[...] = slab


def _route(h, mo_p, mo_s, g, whi, wlo, br, tm, n_groups, epg):
    n, d = h.shape
    npt = mo_p.shape[0] // tm
    const = lambda i: (0, 0)
    return pl.pallas_call(
        functools.partial(_route_kernel, npt=npt, tm=tm, n_groups=n_groups, epg=epg),
        grid=(n // tm,),
        in_specs=[
            pl.BlockSpec((tm, d), lambda i: (i, 0)),
            pl.BlockSpec((tm, d), lambda i: (jnp.minimum(i, npt - 1), 0)),
            pl.BlockSpec((tm, d), lambda i: (jnp.maximum(i - npt, 0), 0)),
            pl.BlockSpec((1, d), const),
            pl.BlockSpec((d, V7X_LANES), const), pl.BlockSpec((d, V7X_LANES), const),
            pl.BlockSpec((1, V7X_LANES), const),
        ],
        out_specs=[pl.BlockSpec((tm, d), lambda i: (i, 0)),
                   pl.BlockSpec((tm, d // 2 // V7X_LANES, V7X_LANES), lambda i: (i, 0, 0)),
                   pl.BlockSpec((tm, V7X_LANES), lambda i: (i, 0)),
                   pl.BlockSpec((V7X_SUBLANES, V7X_LANES), const)],
        out_shape=[jax.ShapeDtypeStruct((n, d), F32),
                   jax.ShapeDtypeStruct((n, d // 2 // V7X_LANES, V7X_LANES), jnp.uint32),
                   jax.ShapeDtypeStruct((n, V7X_LANES), F32),
                   jax.ShapeDtypeStruct((V7X_SUBLANES, V7X_LANES), F32)],
        scratch_shapes=[pltpu.VMEM((V7X_SUBLANES, V7X_LANES), F32)],
        compiler_params=_cparams(("arbitrary",)),
        name="route",
    )(h, mo_p, mo_s, g, whi, wlo, br)


def _dispatch_tables(slab, cnt, n_experts, n_groups, tm, n_blk):
    counts = cnt[0, n_groups:n_groups + n_experts].astype(jnp.int32)
    padded = (counts + tm - 1) // tm * tm
    pad_ends = jnp.cumsum(padded)
    pad_starts = pad_ends - padded
    eids = jnp.arange(n_experts, dtype=jnp.int32)

    def dest(k):
        e = slab[:, k].astype(jnp.int32)
        start = jnp.sum(jnp.where(e[:, None] == eids[None, :], pad_starts[None, :], 0), axis=1)
        return start + slab[:, 4 + k].astype(jnp.int32)

    blk_start = jnp.arange(n_blk, dtype=jnp.int32) * tm
    blk_e = jnp.sum((blk_start[:, None] >= pad_ends[None, :]).astype(jnp.int32), axis=1)
    blk_e = jnp.minimum(blk_e, n_experts - 1)
    nblk = pad_ends[-1:] // tm
    return dest(0), dest(1), blk_e, nblk


def _dispatch_kernel(d0_ref, d1_ref, xn_ref, xs_in, xs_out, sem, *, tm):
    del xs_in
    i = pl.program_id(0)
    base = i * tm

    def issue(r, priority):
        src = xn_ref.at[r]
        pltpu.make_async_copy(src, xs_out.at[d0_ref[base + r]], sem).start(priority)
        pltpu.make_async_copy(src, xs_out.at[d1_ref[base + r]], sem).start(1 - priority)

    _issue_rows(tm, issue)
    for _ in range(TOP_K):
        pltpu.make_async_copy(xn_ref, xs_out.at[pl.ds(0, tm)], sem).wait()


def _dispatch(d0, d1, xn, xs_base, tm):
    n = xn.shape[0]
    grid_spec = pltpu.PrefetchScalarGridSpec(
        num_scalar_prefetch=2,
        grid=(n // tm,),
        in_specs=[pl.BlockSpec((tm,) + xn.shape[1:], lambda i, a, b: (i, 0, 0)),
                  pl.BlockSpec(memory_space=pl.ANY)],
        out_specs=pl.BlockSpec(memory_space=pl.ANY),
        scratch_shapes=[pltpu.SemaphoreType.DMA],
    )
    return pl.pallas_call(
        functools.partial(_dispatch_kernel, tm=tm),
        grid_spec=grid_spec,
        out_shape=jax.ShapeDtypeStruct(xs_base.shape, xs_base.dtype),
        input_output_aliases={3: 0},
        compiler_params=_cparams(("arbitrary",)),
        name="moe_dispatch",
    )(d0, d1, xn, xs_base)


def _experts_kernel(be_ref, nb_ref, xs_ref, w1_ref, w3_ref, w2_ref, y_ref, w1b, w3b, w2b):
    i = pl.program_id(0)

    @pl.when(i < nb_ref[0])
    def _():
        e = be_ref[i]
        e_prev = be_ref[jnp.maximum(i - 1, 0)]

        @pl.when((i == 0) | (e != e_prev))
        def _():
            w1b[...] = w1_ref[...].astype(BF16)
            w3b[...] = w3_ref[...].astype(BF16)
            w2b[...] = w2_ref[...].astype(BF16)

        half = xs_ref.shape[1] * xs_ref.shape[2]
        rows = xs_ref.shape[0] // 2
        ups = []
        for r in range(2):
            xa, xb = _unpack_bf16_pairs(_load_row_tiles(xs_ref, r * rows, (r + 1) * rows))
            xa = xa.astype(BF16)
            xb = xb.astype(BF16)
            a = _dot(xa, w1b[0:half, :]) + _dot(xb, w1b[half:, :])
            g = _dot(xa, w3b[0:half, :]) + _dot(xb, w3b[half:, :])
            ups.append((a, g))
        for r, (a, g) in enumerate(ups):
            hmid = (a * _sigmoid(a)) * g
            _store_row_tiles(y_ref, r * rows, (r + 1) * rows,
                             _pack_bf16_pairs(_dot(hmid.astype(BF16), w2b[...])))


def _experts(blk_e, nblk, xs, w1, w3, w2, layer, tm):
    n_rows = xs.shape[0]
    d = w1.shape[-2]
    f = w1.shape[-1]
    wmap = lambda i, be, nb: (layer, be[i], 0, 0)
    xmap = lambda i, be, nb: (jnp.minimum(i, nb[0] - 1), 0, 0)
    grid_spec = pltpu.PrefetchScalarGridSpec(
        num_scalar_prefetch=2,
        grid=(n_rows // tm,),
        in_specs=[
            pl.BlockSpec((tm,) + xs.shape[1:], xmap),
            pl.BlockSpec((None, None, d, f), wmap),
            pl.BlockSpec((None, None, d, f), wmap),
            pl.BlockSpec((None, None, f, d), wmap),
        ],
        out_specs=pl.BlockSpec((tm,) + xs.shape[1:], xmap),
        scratch_shapes=[pltpu.VMEM((d, f), BF16), pltpu.VMEM((d, f), BF16), pltpu.VMEM((f, d), BF16)],
    )
    return pl.pallas_call(
        _experts_kernel,
        grid_spec=grid_spec,
        out_shape=jax.ShapeDtypeStruct(xs.shape, xs.dtype),
        input_output_aliases={2: 0},
        compiler_params=_cparams(("arbitrary",)),
        name="moe_experts",
    )(blk_e, nblk, xs, w1, w3, w2)


def _combine_copies(d0_ref, d1_ref, y_hbm, ybuf, sem, tm, tile, slot):
    base = tile * tm

    def issue(r, priority):
        pltpu.make_async_copy(y_hbm.at[d0_ref[base + r]], ybuf.at[slot, 0, r], sem.at[slot]).start(priority)
        pltpu.make_async_copy(y_hbm.at[d1_ref[base + r]], ybuf.at[slot, 1, r], sem.at[slot]).start(1 - priority)

    _issue_rows(tm, issue)


def _ple_kernel(d0_ref, d1_ref, h1_ref, y_hbm, slab_ref, pp_ref, ps_ref, gple_ref, wg_ref, wp_ref,
                gn_ref, *rest, npt, last, tm):
    ybuf, sem = rest[-2:]
    outs = rest[:-2]
    i = pl.program_id(0)
    nt = pl.num_programs(0)
    slot = i % 2
    fetch = functools.partial(_combine_copies, d0_ref, d1_ref, y_hbm, ybuf, sem, tm)

    @pl.when(i == 0)
    def _():
        fetch(0, 0)

    @pl.when(i + 1 < nt)
    def _():
        fetch(i + 1, 1 - slot)

    for k in range(TOP_K):
        pltpu.make_async_copy(y_hbm.at[pl.ds(0, tm)], ybuf.at[slot, k], sem.at[slot]).wait()

    p = jnp.where(i < npt, pp_ref[...], ps_ref[...]).astype(BF16)
    proj = _dot(p, wp_ref[...])
    slab = slab_ref[...]
    y0 = jnp.concatenate(_unpack_bf16_pairs(_load_row_tiles(ybuf.at[slot, 0], 0, tm)), axis=1)
    y1 = jnp.concatenate(_unpack_bf16_pairs(_load_row_tiles(ybuf.at[slot, 1], 0, tm)), axis=1)
    h2 = h1_ref[...] + slab[:, 2:3] * y0 + slab[:, 3:4] * y1
    gate = _sigmoid(_dot(_rms(h2, gple_ref[...]).astype(BF16), wg_ref[...]))
    h3 = h2 + gate * proj
    hn = _rms(h3, gn_ref[...])
    if last:
        yp_ref, ys_ref = outs

        @pl.when(i < npt)
        def _():
            yp_ref[...] = hn

        @pl.when(i >= npt)
        def _():
            ys_ref[...] = hn
    else:
        h_ref, hn_ref = outs
        h_ref[...] = h3
        hn_ref[...] = hn.astype(BF16)


def _ple(d0, d1, h1, y, slab, p_p, p_s, layer, gple, wg, wp, gn, tm, last):
    n, d = h1.shape
    np_ = p_p.shape[1]
    npt = np_ // tm
    pd = p_p.shape[2]
    const = lambda i, a, b: (0, 0)
    rowmap = lambda i, a, b: (i, 0)
    if last:
        out_specs = [pl.BlockSpec((tm, d), lambda i, a, b: (jnp.minimum(i, npt - 1), 0)),
                     pl.BlockSpec((tm, d), lambda i, a, b: (jnp.maximum(i - npt, 0), 0))]
        out_shape = [jax.ShapeDtypeStruct((np_, d), F32), jax.ShapeDtypeStruct((n - np_, d), F32)]
    else:
        out_specs = [pl.BlockSpec((tm, d), rowmap), pl.BlockSpec((tm, d), rowmap)]
        out_shape = [jax.ShapeDtypeStruct((n, d), F32), jax.ShapeDtypeStruct((n, d), BF16)]
    grid_spec = pltpu.PrefetchScalarGridSpec(
        num_scalar_prefetch=2,
        grid=(n // tm,),
        in_specs=[
            pl.BlockSpec((tm, d), rowmap),
            pl.BlockSpec(memory_space=pl.ANY),
            pl.BlockSpec((tm, V7X_LANES), rowmap),
            pl.BlockSpec((None, tm, pd), lambda i, a, b: (layer, jnp.minimum(i, npt - 1), 0)),
            pl.BlockSpec((None, tm, pd), lambda i, a, b: (layer, jnp.maximum(i - npt, 0), 0)),
            pl.BlockSpec((1, d), const), pl.BlockSpec((d, d), const), pl.BlockSpec((pd, d), const),
            pl.BlockSpec((1, d), const),
        ],
        out_specs=out_specs,
        scratch_shapes=[pltpu.VMEM((2, TOP_K, tm) + y.shape[1:], y.dtype), pltpu.SemaphoreType.DMA((2,))],
    )
    return pl.pallas_call(
        functools.partial(_ple_kernel, npt=npt, last=last, tm=tm),
        grid_spec=grid_spec,
        out_shape=out_shape,
        compiler_params=_cparams(("arbitrary",)),
        name="ple",
    )(d0, d1, h1, y, slab, p_p, p_s, gple, wg, wp, gn)


def _rope_tables(pos, rope):
    half = rope // 2
    inv = 1.0 / (ROPE_THETA ** (jnp.arange(half, dtype=F32) * (2.0 / rope)))
    ang = pos.astype(F32)[:, None] * inv[None, :]
    pad = ((0, 0), (0, V7X_LANES - rope))
    cos = jnp.pad(jnp.tile(jnp.cos(ang), (1, 2)), pad)
    sin = jnp.pad(jnp.tile(jnp.sin(ang), (1, 2)), pad)
    return cos, sin


def _rot_last(w):
    half = w.shape[-1] // 2
    return jnp.concatenate([-w[..., half:], w[..., :half]], axis=-1)


def _pad_last(w, width):
    return jnp.pad(w, [(0, 0)] * (w.ndim - 1) + [(0, width - w.shape[-1])])


def kernel(x_prompt, x_sample, state_pool, state_conv, cache_latent, cache_krope, page_table, p_prompt, p_sample, norm_mix, norm_ffn, norm_ple, norm_final, pool_w_in, pool_w_grp, pool_scale, conv_w_pw1, conv_b_pw1, conv_w_dw, conv_b_dw, conv_ln_g, conv_ln_b, conv_w_pw2, conv_b_pw2, mla_w_dqkv, mla_q_norm, mla_kv_norm, mla_w_uq, mla_w_uk, mla_w_uv, mla_w_o, moe_w_group, moe_b_group, moe_w_expert, moe_b_expert, moe_w1, moe_w3, moe_w2, ple_w_gate, ple_w_proj):
    b, t, d = x_prompt.shape
    bs, ts, _ = x_sample.shape
    depth = norm_mix.shape[0]
    np_, ns = b * t, bs * ts
    n = np_ + ns
    tm = min(TOKEN_TILE, ns)
    assert np_ % tm == 0 and ns % tm == 0
    past_len = page_table.shape[1] * cache_latent.shape[2]
    n_groups = moe_w_group.shape[-1]
    n_experts = moe_w_expert.shape[-1]
    epg = n_experts // n_groups
    assert n_groups + n_experts <= V7X_LANES
    n_rows = (n * TOP_K + n_experts * (MOE_TILE - 1) + MOE_TILE - 1) // MOE_TILE * MOE_TILE
    pd = p_prompt.shape[-1]
    row = lambda v: v.reshape(1, -1)

    nh, kv_lora, nope = mla_w_uk.shape[1:]
    rope = cache_krope.shape[-1]
    q_lora = mla_w_dqkv.shape[-1] - kv_lora - rope
    scale = float((nope + rope) ** -0.5)
    mla_dims = (q_lora, kv_lora, rope, nh, nope, scale)
    cache_kr_t = jnp.swapaxes(cache_krope, 2, 3)

    h, hn = _prep(x_prompt.reshape(np_, d), x_sample.reshape(ns, d), row(norm_mix[0]), tm)
    p_p = p_prompt.reshape(depth, np_, pd)
    p_s = p_sample.reshape(depth, ns, pd)

    xs_base = jnp.zeros((n_rows, d // 2 // V7X_LANES, V7X_LANES), jnp.uint32)
    new_pool_p, new_pool_s, new_conv_p, new_conv_s = [], [], [], []
    new_lat_p, new_kr_p, new_lat_s, new_kr_s = [], [], [], []
    y_p = y_s = None
    for i in range(depth):
        kind, j = i % N_MIXERS, i // N_MIXERS
        if kind == 0:
            w_in = pool_w_in[j].astype(BF16)
            w_grp = pool_w_grp[j].astype(BF16)
            sc = row(pool_scale[j])
            mo_p, st_p = _pool_prompt(hn, w_in, w_grp, sc, b, t, min(tm, t))
            mo_s, st_s = _pool_sample(hn, np_, state_pool[j], w_in, w_grp, sc, bs, ts, past_len)
            new_pool_p.append(st_p)
            new_pool_s.append(st_s)
        elif kind == 1:
            w1 = conv_w_pw1[j].astype(BF16)
            w2 = conv_w_pw2[j].astype(BF16)
            args = (w1, row(conv_b_pw1[j]), conv_w_dw[j], row(conv_b_dw[j]), row(conv_ln_g[j]),
                    row(conv_ln_b[j]), w2, row(conv_b_pw2[j]))
            mo_p, st_p = _conv_prompt(hn, *args, b, t, min(CONV_TILE, t))
            mo_s, st_s = _conv_sample(hn, np_, state_conv[j], *args, bs, ts)
            new_conv_p.append(st_p)
            new_conv_s.append(st_s)
        else:
            wd = mla_w_dqkv[j]
            o = q_lora + kv_lora
            wd_k = wd[:, o:]
            wd = jnp.concatenate([wd[:, :o], _pad_last(wd_k, V7X_LANES),
                                  _pad_last(_rot_last(wd_k), V7X_LANES)], axis=1).astype(BF16)
            wuq = mla_w_uq[j].reshape(q_lora, nh, nope + rope)
            wq_rope = wuq[:, :, nope:]
            wuq = jnp.concatenate([
                wuq[:, :, :nope].reshape(q_lora, nh * nope),
                _pad_last(wq_rope, V7X_LANES).reshape(q_lora, nh * V7X_LANES),
                _pad_last(_rot_last(wq_rope), V7X_LANES).reshape(q_lora, nh * V7X_LANES)],
                axis=1).astype(BF16)
            wuk = jnp.swapaxes(mla_w_uk[j], 1, 2).astype(BF16)
            wuv = mla_w_uv[j].astype(BF16)
            wo = mla_w_o[j].astype(BF16)
            qn, kvn = row(mla_q_norm[j]), row(mla_kv_norm[j])
            tmp = min(tm, t)
            ntj = t // tmp
            cos_p, sin_p = _rope_tables(jnp.arange(t, dtype=jnp.int32), rope)
            ckv_p, kpe_p, kcat_p, qcat_p = _mla_qkv(
                hn, 0, np_, wd, qn, kvn, wuq, wuk, cos_p, sin_p, lambda ii: (ii % ntj, 0), tmp,
                mla_dims, BF16)
            cos_s, sin_s = _rope_tables(past_len + jnp.arange(ts, dtype=jnp.int32), rope)
            cos_s, sin_s = jnp.tile(cos_s, (bs, 1)), jnp.tile(sin_s, (bs, 1))
            ckv_s, kpe_s, _, qcat_s = _mla_qkv(
                hn, np_ // tm, ns, wd, qn, kvn, wuq, wuk, cos_s, sin_s, lambda ii: (ii, 0), tm,
                mla_dims, F32)
            o_p = _attn_prompt(qcat_p, kcat_p, b, t, nh, kv_lora)
            o_s = _attn_sample(page_table, qcat_s, ckv_s, kpe_s, cache_latent, cache_kr_t, j, bs, ts, nh)
            mo_p = _mla_out(o_p, wuv, wo, tm)
            mo_s = _mla_out(o_s, wuv, wo, tm)
            new_lat_p.append(ckv_p.reshape(b, t, kv_lora))
            new_kr_p.append(kpe_p.reshape(b, t, rope))
            new_lat_s.append(ckv_s.reshape(bs, ts, kv_lora))
            new_kr_s.append(kpe_s.reshape(bs, ts, rope))

        wr = _pad_last(jnp.concatenate([moe_w_group[i], moe_w_expert[i]], axis=1), V7X_LANES)
        br = _pad_last(jnp.concatenate([moe_b_group[i], moe_b_expert[i]])[None, :], V7X_LANES)
        whi = wr.astype(BF16)
        wlo = (wr - whi.astype(F32)).astype(BF16)
        h1, xn, slab, cnt = _route(h, mo_p, mo_s, row(norm_ffn[i]), whi, wlo, br, tm, n_groups, epg)
        d0, d1, blk_e, nblk = _dispatch_tables(slab, cnt, n_experts, n_groups, MOE_TILE, n_rows // MOE_TILE)
        xs = _dispatch(d0, d1, xn, xs_base, tm)
        y = _experts(blk_e, nblk, xs, moe_w1, moe_w3, moe_w2, i, MOE_TILE)
        xs_base = y
        last = i == depth - 1
        gn = row(norm_final) if last else row(norm_mix[i + 1])
        out = _ple(d0, d1, h1, y, slab, p_p, p_s, i, row(norm_ple[i]), ple_w_gate[i].astype(BF16),
                   ple_w_proj[i].astype(BF16), gn, tm, last)
        if last:
            y_p, y_s = out
        else:
            h, hn = out

    return (y_p.reshape(b, t, d), y_s.reshape(bs, ts, d),
            jnp.stack(new_pool_p), jnp.stack(new_pool_s),
            jnp.stack(new_conv_p), jnp.stack(new_conv_s),
            jnp.stack(new_lat_p), jnp.stack(new_kr_p),
            jnp.stack(new_lat_s), jnp.stack(new_kr_s))
```

```python
import functools

import jax
import jax.numpy as jnp
from jax import lax
from jax.experimental import pallas as pl
from jax.experimental.pallas import tpu as pltpu

F32 = jnp.float32
BF16 = jnp.bfloat16

POOL_WINDOWS = (2, 4, 8, 16)
TOP_K = 2
ROPE_THETA = 10000.0
RMS_EPS = 1e-6
LN_EPS = 1e-5
NEG_INF = -1e30
N_MIXERS = 3

V7X_LANES = 128
V7X_SUBLANES = 8
V7X_VMEM_LIMIT_BYTES = 48 * 1024 * 1024

TOKEN_TILE = 512
MOE_TILE = 512
CONV_TILE = 256
ATTN_Q_TILE = 256
ATTN_K_TILE = 512
DECODE_PAGE_CHUNK = 8
SAMPLE_SEQ_BLOCK = 32
DMA_ISSUE_UNROLL = 8


def _cparams(sem):
    return pltpu.CompilerParams(dimension_semantics=sem, vmem_limit_bytes=V7X_VMEM_LIMIT_BYTES)


def _rms(x, g):
    ms = jnp.mean(x * x, axis=-1, keepdims=True)
    return x * lax.rsqrt(ms + RMS_EPS) * g


def _dot(a, b):
    return jnp.dot(a, b, preferred_element_type=F32)


def _dot_t(a, b):
    return lax.dot_general(a, b, (((1,), (1,)), ((), ())), preferred_element_type=F32)


def _sigmoid(x):
    return 1.0 / (1.0 + jnp.exp(-x))


def _pack_bf16_pairs(x):
    w = x.shape[1] // 2
    hi = lax.bitcast_convert_type(x[:, :w].astype(BF16).astype(F32), jnp.uint32)
    lo = lax.bitcast_convert_type(x[:, w:].astype(BF16).astype(F32), jnp.uint32)
    return hi | (lo >> 16)


def _unpack_bf16_pairs(p):
    hi = lax.bitcast_convert_type(p & jnp.uint32(0xFFFF0000), F32)
    lo = lax.bitcast_convert_type(p << 16, F32)
    return hi, lo


def _issue_rows(n_rows, issue):
    def group(g, c):
        for u in range(V7X_SUBLANES):
            issue(g, u, u % 2)
        return c

    lax.fori_loop(0, n_rows // V7X_SUBLANES, group, 0)


def _prep_kernel(xp_ref, xs_ref, g_ref, h_ref, hn_ref, *, npt):
    i = pl.program_id(0)
    x = jnp.where(i < npt, xp_ref[...], xs_ref[...])
    h_ref[...] = x
    hn_ref[...] = _rms(x, g_ref[...]).astype(BF16)


def _prep(xp, xs, g, tm):
    np_, d = xp.shape
    ns = xs.shape[0]
    npt, nst = np_ // tm, ns // tm
    n = np_ + ns
    return pl.pallas_call(
        functools.partial(_prep_kernel, npt=npt),
        grid=(npt + nst,),
        in_specs=[
            pl.BlockSpec((tm, d), lambda i: (jnp.minimum(i, npt - 1), 0)),
            pl.BlockSpec((tm, d), lambda i: (jnp.maximum(i - npt, 0), 0)),
            pl.BlockSpec((1, d), lambda i: (0, 0)),
        ],
        out_specs=[pl.BlockSpec((tm, d), lambda i: (i, 0)),
                   pl.BlockSpec((tm, d), lambda i: (i, 0))],
        out_shape=[jax.ShapeDtypeStruct((n, d), F32), jax.ShapeDtypeStruct((n, d), BF16)],
        compiler_params=_cparams(("arbitrary",)),
        name="prep",
    )(xp, xs, g)


def _pool_prompt_kernel(hn_ref, win_ref, wg_ref, sc_ref, mo_ref, st_ref, ext_ref, *, tm, carry, state):
    j = pl.program_id(1)
    nj = pl.num_programs(1)
    d = hn_ref.shape[1]
    gd = d // len(POOL_WINDOWS)
    u = _dot(hn_ref[...], win_ref[...])

    @pl.when(j == 0)
    def _():
        ext_ref[0:carry, :] = jnp.zeros((carry, d), F32)

    @pl.when(j > 0)
    def _():
        ext_ref[0:carry, :] = ext_ref[tm:tm + carry, :]

    ext_ref[carry:carry + tm, :] = u
    pos = j * tm + lax.broadcasted_iota(jnp.int32, (tm, 1), 0)
    for g, w in enumerate(POOL_WINDOWS):
        c0, c1 = g * gd, (g + 1) * gd
        ug = u[:, c0:c1]
        s = ug
        for k in range(1, w):
            s = s + ext_ref[carry - k:carry - k + tm, c0:c1]
        cnt = jnp.minimum(pos + 1, w).astype(F32)
        z = s / cnt - ug
        y = _dot(z.astype(BF16), wg_ref[g])
        mo_ref[:, c0:c1] = y * sc_ref[:, c0:c1]

    @pl.when(j == nj - 1)
    def _():
        st_ref[0] = ext_ref[carry + tm - state:carry + tm, :]


def _pool_prompt(hn, w_in, w_grp, scale, b, t, tm):
    d = hn.shape[1]
    state = max(POOL_WINDOWS) - 1
    carry = 16
    ntj = t // tm
    return pl.pallas_call(
        functools.partial(_pool_prompt_kernel, tm=tm, carry=carry, state=state),
        grid=(b, ntj),
        in_specs=[
            pl.BlockSpec((tm, d), lambda bi, j: (bi * ntj + j, 0)),
            pl.BlockSpec((d, d), lambda bi, j: (0, 0)),
            pl.BlockSpec(w_grp.shape, lambda bi, j: (0, 0, 0)),
            pl.BlockSpec((1, d), lambda bi, j: (0, 0)),
        ],
        out_specs=[pl.BlockSpec((tm, d), lambda bi, j: (bi * ntj + j, 0)),
                   pl.BlockSpec((1, state, d), lambda bi, j: (bi, 0, 0))],
        out_shape=[jax.ShapeDtypeStruct((b * t, d), F32), jax.ShapeDtypeStruct((b, state, d), F32)],
        scratch_shapes=[pltpu.VMEM((carry + tm, d), F32)],
        compiler_params=_cparams(("arbitrary", "arbitrary")),
        name="pool_prompt",
    )(hn, w_in, w_grp, scale)


def _pool_sample_kernel(hn_ref, prev_ref, win_ref, wg_ref, sc_ref, mo_ref, st_ref, u_ref, z_ref,
                        *, bs, t, state, pos0):
    d = hn_ref.shape[1]
    gd = d // len(POOL_WINDOWS)
    u = _dot(hn_ref[...], win_ref[...])
    u_ref[...] = u.reshape(bs, t, d)

    def row(i, c0, c1):
        if i >= 0:
            return u_ref[:, i, c0:c1]
        return prev_ref[:, state + i, c0:c1]

    for g, w in enumerate(POOL_WINDOWS):
        c0, c1 = g * gd, (g + 1) * gd
        for ti in range(t):
            s = row(ti, c0, c1)
            for k in range(1, w):
                s = s + row(ti - k, c0, c1)
            cnt = float(min(pos0 + ti + 1, w))
            z_ref[:, ti, c0:c1] = s / cnt - row(ti, c0, c1)
    z = z_ref[...].reshape(bs * t, d)
    for g in range(len(POOL_WINDOWS)):
        c0, c1 = g * gd, (g + 1) * gd
        y = _dot(z[:, c0:c1].astype(BF16), wg_ref[g])
        mo_ref[:, c0:c1] = y * sc_ref[:, c0:c1]
    if t >= state:
        st_ref[...] = u_ref[:, t - state:t, :]
    else:
        st_ref[:, 0:state - t, :] = prev_ref[:, t:state, :]
        st_ref[:, state - t:state, :] = u_ref[...]


def _pool_sample(hn, row0, prev, w_in, w_grp, scale, b, t, pos0):
    d = hn.shape[1]
    state = prev.shape[1]
    bs = SAMPLE_SEQ_BLOCK
    blk0 = row0 // (bs * t)
    return pl.pallas_call(
        functools.partial(_pool_sample_kernel, bs=bs, t=t, state=state, pos0=pos0),
        grid=(b // bs,),
        in_specs=[
            pl.BlockSpec((bs * t, d), lambda i: (blk0 + i, 0)),
            pl.BlockSpec((bs, state, d), lambda i: (i, 0, 0)),
            pl.BlockSpec((d, d), lambda i: (0, 0)),
            pl.BlockSpec(w_grp.shape, lambda i: (0, 0, 0)),
            pl.BlockSpec((1, d), lambda i: (0, 0)),
        ],
        out_specs=[pl.BlockSpec((bs * t, d), lambda i: (i, 0)),
                   pl.BlockSpec((bs, state, d), lambda i: (i, 0, 0))],
        out_shape=[jax.ShapeDtypeStruct((b * t, d), F32), jax.ShapeDtypeStruct((b, state, d), F32)],
        scratch_shapes=[pltpu.VMEM((bs, t, d), F32), pltpu.VMEM((bs, t, d), F32)],
        compiler_params=_cparams(("arbitrary",)),
        name="pool_sample",
    )(hn, prev, w_in, w_grp, scale)


def _layernorm_silu(c, g, b):
    mu = jnp.mean(c, axis=-1, keepdims=True)
    xc = c - mu
    var = jnp.mean(xc * xc, axis=-1, keepdims=True)
    n = xc * lax.rsqrt(var + LN_EPS) * g + b
    return n * _sigmoid(n)


def _conv_prompt_kernel(hn_ref, w1_ref, b1_ref, wdw_ref, bdw_ref, lg_ref, lb_ref, w2_ref, b2_ref,
                        mo_ref, st_ref, ext_ref, c_ref, tap_ref, *, tm, carry, width, rc):
    j = pl.program_id(1)
    nj = pl.num_programs(1)
    d = hn_ref.shape[1]
    sub = V7X_SUBLANES

    @pl.when((pl.program_id(0) == 0) & (j == 0))
    def _():
        for k in range(width):
            tap_ref[k] = jnp.broadcast_to(wdw_ref[k:k + 1, :], (sub, d))

    a = _dot(hn_ref[...], w1_ref[...]) + b1_ref[...]
    glu = a[:, :d] * _sigmoid(a[:, d:])

    length = carry + tm

    @pl.when(j == 0)
    def _():
        ext_ref[0, 0:carry, :] = jnp.zeros((carry, d), F32)

    @pl.when(j > 0)
    def _():
        ext_ref[0, 0:carry, :] = ext_ref[0, tm:tm + carry, :]

    ext_ref[0, carry:carry + tm, :] = glu
    for m in range(1, V7X_SUBLANES):
        ext_ref[m, 0:length - V7X_SUBLANES, :] = ext_ref[0, m:length - V7X_SUBLANES + m, :]
    off = carry - (width - 1)

    def chunk(ci, _):
        r0 = pl.multiple_of(ci * rc, rc)
        acc = jnp.zeros((rc // sub, sub, d), F32) + bdw_ref[...]
        for k in range(width):
            q, m = divmod(off + k, sub)
            win = ext_ref[m, pl.ds(r0 + q * sub, rc), :].reshape(rc // sub, sub, d)
            acc = acc + win * tap_ref[k]
        c_ref[pl.ds(r0, rc), :] = acc.reshape(rc, d)
        return 0

    lax.fori_loop(0, tm // rc, chunk, 0)
    s = _layernorm_silu(c_ref[...], lg_ref[...], lb_ref[...])
    mo_ref[...] = _dot(s.astype(BF16), w2_ref[...]) + b2_ref[...]

    @pl.when(j == nj - 1)
    def _():
        st_ref[0] = ext_ref[0, carry + tm - (width - 1):carry + tm, :]


def _conv_prompt(hn, w1, b1, wdw, bdw, lg, lb, w2, b2, b, t, tm):
    d = hn.shape[1]
    width = wdw.shape[0]
    carry = 32
    ntj = t // tm
    const = lambda bi, j: (0, 0)
    return pl.pallas_call(
        functools.partial(_conv_prompt_kernel, tm=tm, carry=carry, width=width, rc=16),
        grid=(b, ntj),
        in_specs=[
            pl.BlockSpec((tm, d), lambda bi, j: (bi * ntj + j, 0)),
            pl.BlockSpec((d, 2 * d), const), pl.BlockSpec((1, 2 * d), const),
            pl.BlockSpec((width, d), const), pl.BlockSpec((1, d), const),
            pl.BlockSpec((1, d), const), pl.BlockSpec((1, d), const),
            pl.BlockSpec((d, d), const), pl.BlockSpec((1, d), const),
        ],
        out_specs=[pl.BlockSpec((tm, d), lambda bi, j: (bi * ntj + j, 0)),
                   pl.BlockSpec((1, width - 1, d), lambda bi, j: (bi, 0, 0))],
        out_shape=[jax.ShapeDtypeStruct((b * t, d), F32),
                   jax.ShapeDtypeStruct((b, width - 1, d), F32)],
        scratch_shapes=[pltpu.VMEM((V7X_SUBLANES, carry + tm, d), F32), pltpu.VMEM((tm, d), F32),
                        pltpu.VMEM((width, V7X_SUBLANES, d), F32)],
        compiler_params=_cparams(("arbitrary", "arbitrary")),
        name="conv_prompt",
    )(hn, w1, b1, wdw, bdw, lg, lb, w2, b2)


def _conv_sample_kernel(hn_ref, prev_ref, w1_ref, b1_ref, wdw_ref, bdw_ref, lg_ref, lb_ref, w2_ref,
                        b2_ref, mo_ref, st_ref, g_ref, c_ref, *, bs, t, width):
    d = hn_ref.shape[1]
    state = width - 1
    a = _dot(hn_ref[...], w1_ref[...]) + b1_ref[...]
    glu = a[:, :d] * _sigmoid(a[:, d:])
    g_ref[...] = glu.reshape(bs, t, d)

    def row(i):
        if i >= 0:
            return g_ref[:, i, :]
        return prev_ref[:, state + i, :]

    for ti in range(t):
        acc = jnp.zeros((bs, d), F32) + bdw_ref[...]
        for k in range(width):
            acc = acc + row(ti + k - state) * wdw_ref[k:k + 1, :]
        c_ref[:, ti, :] = acc
    s = _layernorm_silu(c_ref[...].reshape(bs * t, d), lg_ref[...], lb_ref[...])
    mo_ref[...] = _dot(s.astype(BF16), w2_ref[...]) + b2_ref[...]
    if t >= state:
        st_ref[...] = g_ref[:, t - state:t, :]
    else:
        st_ref[:, 0:state - t, :] = prev_ref[:, t:state, :]
        st_ref[:, state - t:state, :] = g_ref[...]


def _conv_sample(hn, row0, prev, w1, b1, wdw, bdw, lg, lb, w2, b2, b, t):
    d = hn.shape[1]
    width = wdw.shape[0]
    bs = SAMPLE_SEQ_BLOCK
    blk0 = row0 // (bs * t)
    const = lambda i: (0, 0)
    return pl.pallas_call(
        functools.partial(_conv_sample_kernel, bs=bs, t=t, width=width),
        grid=(b // bs,),
        in_specs=[
            pl.BlockSpec((bs * t, d), lambda i: (blk0 + i, 0)),
            pl.BlockSpec((bs, width - 1, d), lambda i: (i, 0, 0)),
            pl.BlockSpec((d, 2 * d), const), pl.BlockSpec((1, 2 * d), const),
            pl.BlockSpec((width, d), const), pl.BlockSpec((1, d), const),
            pl.BlockSpec((1, d), const), pl.BlockSpec((1, d), const),
            pl.BlockSpec((d, d), const), pl.BlockSpec((1, d), const),
        ],
        out_specs=[pl.BlockSpec((bs * t, d), lambda i: (i, 0)),
                   pl.BlockSpec((bs, width - 1, d), lambda i: (i, 0, 0))],
        out_shape=[jax.ShapeDtypeStruct((b * t, d), F32),
                   jax.ShapeDtypeStruct((b, width - 1, d), F32)],
        scratch_shapes=[pltpu.VMEM((bs, t, d), F32), pltpu.VMEM((bs, t, d), F32)],
        compiler_params=_cparams(("arbitrary",)),
        name="conv_sample",
    )(hn, prev, w1, b1, wdw, bdw, lg, lb, w2, b2)


def _mla_qkv_kernel(hn_ref, wd_ref, qn_ref, kvn_ref, wuq_ref, wuk_ref, cos_ref, sin_ref,
                    ckv_ref, kpe_ref, kcat_ref, qcat_ref, *, q_lora, kv_lora, rope, nh, nope, scale):
    lanes = V7X_LANES
    qk = kv_lora + lanes
    o = q_lora + kv_lora
    qn = nh * nope
    rows = hn_ref.shape[0] // 2
    halves = [(r * rows, (r + 1) * rows) for r in range(2)]
    dqs = [_dot(hn_ref[a:b, :], wd_ref[...]) for a, b in halves]
    qs = []
    for (a, b), dq in zip(halves, dqs):
        cq = _rms(dq[:, :q_lora], qn_ref[...]).astype(BF16)
        ckv = _rms(dq[:, q_lora:q_lora + kv_lora], kvn_ref[...])
        kblk = dq[:, o:o + lanes] * cos_ref[a:b, :] + dq[:, o + lanes:o + 2 * lanes] * sin_ref[a:b, :]
        ckv_ref[a:b, :] = ckv
        kpe_ref[a:b, :] = kblk[:, :rope]
        kcat_ref[a:b, :kv_lora] = ckv.astype(BF16)
        kcat_ref[a:b, kv_lora:] = kblk.astype(BF16)
        qs.append(_dot(cq, wuq_ref[...]))
    for (a, b), q in zip(halves, qs):
        cos = cos_ref[a:b, :]
        sin = sin_ref[a:b, :]
        for h in range(nh):
            ql = _dot(q[:, h * nope:(h + 1) * nope].astype(BF16), wuk_ref[h])
            r0 = qn + h * lanes
            r1 = qn + (nh + h) * lanes
            qpe = q[:, r0:r0 + lanes] * cos + q[:, r1:r1 + lanes] * sin
            qcat_ref[a:b, h * qk:h * qk + kv_lora] = (ql * scale).astype(qcat_ref.dtype)
            qcat_ref[a:b, h * qk + kv_lora:(h + 1) * qk] = (qpe * scale).astype(qcat_ref.dtype)


def _mla_qkv(hn, blk0, nrows, wd, qn, kvn, wuq, wuk, cos, sin, cos_map, tm, dims, q_dtype):
    q_lora, kv_lora, rope, nh, nope, scale = dims
    d = hn.shape[1]
    qk = kv_lora + V7X_LANES
    const2 = lambda i: (0, 0)
    return pl.pallas_call(
        functools.partial(_mla_qkv_kernel, q_lora=q_lora, kv_lora=kv_lora, rope=rope, nh=nh,
                          nope=nope, scale=scale),
        grid=(nrows // tm,),
        in_specs=[
            pl.BlockSpec((tm, d), lambda i: (blk0 + i, 0)),
            pl.BlockSpec(wd.shape, const2), pl.BlockSpec((1, q_lora), const2),
            pl.BlockSpec((1, kv_lora), const2), pl.BlockSpec(wuq.shape, const2),
            pl.BlockSpec(wuk.shape, lambda i: (0, 0, 0)),
            pl.BlockSpec((tm, V7X_LANES), cos_map), pl.BlockSpec((tm, V7X_LANES), cos_map),
        ],
        out_specs=[pl.BlockSpec((tm, kv_lora), lambda i: (i, 0)),
                   pl.BlockSpec((tm, rope), lambda i: (i, 0)),
                   pl.BlockSpec((tm, qk), lambda i: (i, 0)),
                   pl.BlockSpec((tm, nh * qk), lambda i: (i, 0))],
        out_shape=[jax.ShapeDtypeStruct((nrows, kv_lora), F32),
                   jax.ShapeDtypeStruct((nrows, rope), F32),
                   jax.ShapeDtypeStruct((nrows, qk), BF16),
                   jax.ShapeDtypeStruct((nrows, nh * qk), q_dtype)],
        compiler_params=_cparams(("arbitrary",)),
        name="mla_qkv",
    )(hn, wd, qn, kvn, wuq, wuk, cos, sin)


def _softmax_update(s, v, m_ref, l_ref, acc_ref):
    m_old = m_ref[...]
    m_new = jnp.maximum(m_old, jnp.max(s, axis=1, keepdims=True))
    alpha = jnp.exp(m_old - m_new)
    p = jnp.exp(s - m_new)
    l_ref[...] = alpha * l_ref[...] + jnp.sum(p, axis=1, keepdims=True)
    acc_ref[...] = alpha * acc_ref[...] + _dot(p.astype(BF16), v)
    m_ref[...] = m_new


def _attn_prompt_kernel(q_ref, k_ref, o_ref, m_ref, l_ref, acc_ref, *, tq, tk, nh, qk, kv):
    i = pl.program_id(1)
    m_ref[...] = jnp.full(m_ref.shape, NEG_INF, F32)
    l_ref[...] = jnp.zeros(l_ref.shape, F32)
    acc_ref[...] = jnp.zeros(acc_ref.shape, F32)
    qpos = i * tq + lax.broadcasted_iota(jnp.int32, (tq, 1), 0)
    n_full = (i * tq + 1) // tk
    n_chunks = (i * tq + tq - 1) // tk + 1

    def make_body(masked):
        def body(kk, c):
            ks = pl.multiple_of(kk * tk, tk)
            kc = k_ref[pl.ds(ks, tk), :]
            v = kc[:, :kv]
            if masked:
                vis = ks + lax.broadcasted_iota(jnp.int32, (tq, tk), 1) <= qpos
            scores = [_dot_t(q_ref[:, h * qk:(h + 1) * qk], kc) for h in range(nh)]
            for h in range(nh):
                s = jnp.where(vis, scores[h], NEG_INF) if masked else scores[h]
                _softmax_update(s, v, m_ref.at[h], l_ref.at[h], acc_ref.at[h])
            return c
        return body

    lax.fori_loop(0, n_full, make_body(False), 0)
    lax.fori_loop(n_full, n_chunks, make_body(True), 0)
    for h in range(nh):
        o_ref[:, h * kv:(h + 1) * kv] = (acc_ref[h] / l_ref[h]).astype(BF16)


def _attn_prompt(qcat, kcat, b, t, nh, kv):
    qk = kcat.shape[1]
    tq = min(ATTN_Q_TILE, t)
    tk = min(ATTN_K_TILE, t)
    nq = t // tq
    return pl.pallas_call(
        functools.partial(_attn_prompt_kernel, tq=tq, tk=tk, nh=nh, qk=qk, kv=kv),
        grid=(b, nq),
        in_specs=[
            pl.BlockSpec((tq, nh * qk), lambda bi, i: (bi * nq + i, 0)),
            pl.BlockSpec((t, qk), lambda bi, i: (bi, 0)),
        ],
        out_specs=pl.BlockSpec((tq, nh * kv), lambda bi, i: (bi * nq + i, 0)),
        out_shape=jax.ShapeDtypeStruct((b * t, nh * kv), BF16),
        scratch_shapes=[pltpu.VMEM((nh, tq, 1), F32), pltpu.VMEM((nh, tq, 1), F32),
                        pltpu.VMEM((nh, tq, kv), F32)],
        compiler_params=_cparams(("arbitrary", "arbitrary")),
        name="mla_attn_prompt",
    )(qcat, kcat)


def _page_copies(pt_ref, lat_hbm, kr_hbm, lat_buf, kr_buf, sem, layer, npages, seq, slot):
    def body(p, c):
        idx = pt_ref[seq * npages + p]
        pltpu.make_async_copy(lat_hbm.at[layer, idx], lat_buf.at[slot, p], sem.at[0, slot]).start()
        pltpu.make_async_copy(kr_hbm.at[layer, idx], kr_buf.at[slot, p], sem.at[1, slot]).start()
        return c

    lax.fori_loop(0, npages, body, 0, unroll=DMA_ISSUE_UNROLL)


def _attn_sample_kernel(pt_ref, q_ref, cn_ref, krn_ref, lat_hbm, kr_hbm, o_ref,
                        lat_buf, kr_buf, sem, s_ref, cb_ref,
                        *, layer, npages, chunk, t, nh, qk, kv, rope, page):
    bi = pl.program_id(0)
    nb = pl.num_programs(0)
    slot = bi % 2
    fetch = functools.partial(_page_copies, pt_ref, lat_hbm, kr_hbm, lat_buf, kr_buf, sem, layer, npages)

    @pl.when(bi == 0)
    def _():
        fetch(0, 0)

    @pl.when(bi + 1 < nb)
    def _():
        fetch(bi + 1, 1 - slot)

    q = jnp.concatenate([q_ref[:, h * qk:(h + 1) * qk] for h in range(nh)], axis=0).astype(BF16)
    ql = q[:, :kv]
    qp = q[:, kv:kv + rope]

    pltpu.make_async_copy(lat_hbm.at[layer, pl.ds(0, npages)], lat_buf.at[slot], sem.at[0, slot]).wait()
    pltpu.make_async_copy(kr_hbm.at[layer, pl.ds(0, npages)], kr_buf.at[slot], sem.at[1, slot]).wait()

    width = chunk * page
    n_chunks = npages // chunk
    for ch in range(n_chunks):
        c = lat_buf[slot, ch * chunk:(ch + 1) * chunk].reshape(width, kv).astype(BF16)
        cb_ref[ch * width:(ch + 1) * width, :] = c
        krt = jnp.concatenate([kr_buf[slot, ch * chunk + p] for p in range(chunk)], axis=1).astype(BF16)
        s_ref[:, ch * width:(ch + 1) * width] = _dot_t(ql, c) + _dot(qp, krt)

    rows = nh * t
    cn = jnp.concatenate([cn_ref[...], jnp.zeros((page - t, kv), F32)], axis=0).astype(BF16)
    krn = jnp.concatenate([krn_ref[...], jnp.zeros((page - t, rope), F32)], axis=0).astype(BF16)
    sn = _dot_t(ql, cn) + _dot_t(qp, krn)
    qstep = lax.broadcasted_iota(jnp.int32, (rows, 1), 0) & (t - 1)
    kstep = lax.broadcasted_iota(jnp.int32, (rows, page), 1)
    sn = jnp.where(kstep <= qstep, sn, NEG_INF)

    m = jnp.maximum(jnp.max(s_ref[...], axis=1, keepdims=True), jnp.max(sn, axis=1, keepdims=True))
    pn = jnp.exp(sn - m)
    l = jnp.sum(pn, axis=1, keepdims=True)
    acc = _dot(pn.astype(BF16), cn)
    for ch in range(n_chunks):
        p = jnp.exp(s_ref[:, ch * width:(ch + 1) * width] - m)
        l = l + jnp.sum(p, axis=1, keepdims=True)
        acc = acc + _dot(p.astype(BF16), cb_ref[ch * width:(ch + 1) * width, :])
    o = acc / l
    for h in range(nh):
        o_ref[:, h * kv:(h + 1) * kv] = o[h * t:(h + 1) * t, :]


def _attn_sample(page_table, qcat, cn, krn, cache_lat, cache_kr_t, layer, b, t, nh):
    kv = cn.shape[1]
    rope = krn.shape[1]
    qk = qcat.shape[1] // nh
    page = cache_lat.shape[2]
    npages = page_table.shape[1]
    chunk = min(DECODE_PAGE_CHUNK, npages)
    assert npages % chunk == 0 and t % V7X_SUBLANES == 0 and t <= page and t & (t - 1) == 0
    rows = t * nh
    grid_spec = pltpu.PrefetchScalarGridSpec(
        num_scalar_prefetch=1,
        grid=(b,),
        in_specs=[
            pl.BlockSpec((t, nh * qk), lambda bi, ptr: (bi, 0)),
            pl.BlockSpec((t, kv), lambda bi, ptr: (bi, 0)),
            pl.BlockSpec((t, rope), lambda bi, ptr: (bi, 0)),
            pl.BlockSpec(memory_space=pl.ANY),
            pl.BlockSpec(memory_space=pl.ANY),
        ],
        out_specs=pl.BlockSpec((t, nh * kv), lambda bi, ptr: (bi, 0)),
        scratch_shapes=[
            pltpu.VMEM((2, npages, page, kv), F32),
            pltpu.VMEM((2, npages, rope, page), F32),
            pltpu.SemaphoreType.DMA((2, 2)),
            pltpu.VMEM((rows, npages * page), F32), pltpu.VMEM((npages * page, kv), BF16),
        ],
    )
    return pl.pallas_call(
        functools.partial(_attn_sample_kernel, layer=layer, npages=npages, chunk=chunk, t=t, nh=nh,
                          qk=qk, kv=kv, rope=rope, page=page),
        grid_spec=grid_spec,
        out_shape=jax.ShapeDtypeStruct((b * t, nh * kv), F32),
        compiler_params=_cparams(("arbitrary",)),
        name="mla_attn_sample",
    )(page_table.reshape(-1), qcat, cn, krn, cache_lat, cache_kr_t)


def _mla_out_kernel(o_ref, wuv_ref, wo_ref, mo_ref, *, nh, kv):
    vs = [_dot(o_ref[:, h * kv:(h + 1) * kv].astype(BF16), wuv_ref[h]).astype(BF16) for h in range(nh)]
    mo_ref[...] = _dot(jnp.concatenate(vs, axis=1), wo_ref[...])


def _mla_out(o, wuv, wo, tm):
    n = o.shape[0]
    nh, kv, vd = wuv.shape
    d = wo.shape[1]
    return pl.pallas_call(
        functools.partial(_mla_out_kernel, nh=nh, kv=kv),
        grid=(n // tm,),
        in_specs=[pl.BlockSpec((tm, nh * kv), lambda i: (i, 0)),
                  pl.BlockSpec(wuv.shape, lambda i: (0, 0, 0)),
                  pl.BlockSpec(wo.shape, lambda i: (0, 0))],
        out_specs=pl.BlockSpec((tm, d), lambda i: (i, 0)),
        out_shape=jax.ShapeDtypeStruct((n, d), F32),
        compiler_params=_cparams(("arbitrary",)),
        name="mla_out",
    )(o, wuv, wo)


def _route_kernel(h_ref, mop_ref, mos_ref, g_ref, whi_ref, wlo_ref, br_ref,
                  h1_ref, xn_ref, slab_ref, cnt_ref, carry_ref, *, npt, tm, n_groups, epg):
    i = pl.program_id(0)

    @pl.when(i == 0)
    def _():
        carry_ref[...] = jnp.zeros(carry_ref.shape, F32)

    h1 = h_ref[...] + jnp.where(i < npt, mop_ref[...], mos_ref[...])
    h1_ref[...] = h1
    xn = _rms(h1, g_ref[...])
    xn_ref[...] = _pack_bf16_pairs(xn)
    x_hi = xn.astype(BF16)
    x_lo = (xn - x_hi.astype(F32)).astype(BF16)
    whi = whi_ref[...]
    logits = _dot(x_hi, whi) + _dot(x_lo, whi) + _dot(x_hi, wlo_ref[...]) + br_ref[...]

    lane = lax.broadcasted_iota(jnp.int32, (tm, V7X_LANES), 1).astype(F32)
    big = float(V7X_LANES)
    gl = jnp.where(lane < n_groups, logits, NEG_INF)
    gmax = jnp.max(gl, axis=1, keepdims=True)
    gsel = jnp.min(jnp.where(gl == gmax, lane, big), axis=1, keepdims=True)
    gprob = 1.0 / jnp.sum(jnp.exp(gl - gmax), axis=1, keepdims=True)
    lo = n_groups + gsel * epg
    el = jnp.where(lane >= lo, jnp.where(lane < lo + epg, logits, NEG_INF), NEG_INF)
    m1 = jnp.max(el, axis=1, keepdims=True)
    i1 = jnp.min(jnp.where(el == m1, lane, big), axis=1, keepdims=True)
    el2 = jnp.where(lane == i1, NEG_INF, el)
    m2 = jnp.max(el2, axis=1, keepdims=True)
    i2 = jnp.min(jnp.where(el2 == m2, lane, big), axis=1, keepdims=True)
    r = jnp.exp(m2 - m1)
    g1 = gprob / (1.0 + r)
    g2 = gprob * r / (1.0 + r)

    oh1 = jnp.where(lane == i1, 1.0, 0.0)
    oh2 = jnp.where(lane == i2, 1.0, 0.0)
    oh = oh1 + oh2
    rr = lax.broadcasted_iota(jnp.int32, (tm, tm), 0)
    cc = lax.broadcasted_iota(jnp.int32, (tm, tm), 1)
    tri = jnp.where(rr > cc, 1.0, 0.0).astype(BF16)
    carry = carry_ref[0:1, :]
    cum = _dot(tri, oh.astype(BF16)) + carry
    rank1 = jnp.sum(oh1 * cum, axis=1, keepdims=True)
    rank2 = jnp.sum(oh2 * cum, axis=1, keepdims=True)
    new_carry = carry + jnp.sum(oh, axis=0, keepdims=True)
    carry_ref[...] = jnp.broadcast_to(new_carry, carry_ref.shape)
    cnt_ref[...] = jnp.broadcast_to(new_carry, cnt_ref.shape)

    slab = jnp.where(lane == 0, i1 - n_groups, 0.0)
    slab = jnp.where(lane == 1, i2 - n_groups, slab)
    slab = jnp.where(lane == 2, g1, slab)
    slab = jnp.where(lane == 3, g2, slab)
    slab = jnp.where(lane == 4, rank1, slab)
    slab = jnp.where(lane == 5, rank2, slab)
    slab_ref[...] = slab


def _route(h, mo_p, mo_s, g, whi, wlo, br, tm, n_groups, epg):
    n, d = h.shape
    npt = mo_p.shape[0] // tm
    const = lambda i: (0, 0)
    return pl.pallas_call(
        functools.partial(_route_kernel, npt=npt, tm=tm, n_groups=n_groups, epg=epg),
        grid=(n // tm,),
        in_specs=[
            pl.BlockSpec((tm, d), lambda i: (i, 0)),
            pl.BlockSpec((tm, d), lambda i: (jnp.minimum(i, npt - 1), 0)),
            pl.BlockSpec((tm, d), lambda i: (jnp.maximum(i - npt, 0), 0)),
            pl.BlockSpec((1, d), const),
            pl.BlockSpec((d, V7X_LANES), const), pl.BlockSpec((d, V7X_LANES), const),
            pl.BlockSpec((1, V7X_LANES), const),
        ],
        out_specs=[pl.BlockSpec((tm, d), lambda i: (i, 0)),
                   pl.BlockSpec((tm, d // 2), lambda i: (i, 0)),
                   pl.BlockSpec((tm, V7X_LANES), lambda i: (i, 0)),
                   pl.BlockSpec((V7X_SUBLANES, V7X_LANES), const)],
        out_shape=[jax.ShapeDtypeStruct((n, d), F32), jax.ShapeDtypeStruct((n, d // 2), jnp.uint32),
                   jax.ShapeDtypeStruct((n, V7X_LANES), F32),
                   jax.ShapeDtypeStruct((V7X_SUBLANES, V7X_LANES), F32)],
        scratch_shapes=[pltpu.VMEM((V7X_SUBLANES, V7X_LANES), F32)],
        compiler_params=_cparams(("arbitrary",)),
        name="route",
    )(h, mo_p, mo_s, g, whi, wlo, br)


def _dispatch_tables(slab, cnt, n_experts, n_groups, tm, n_blk):
    counts = cnt[0, n_groups:n_groups + n_experts].astype(jnp.int32)
    padded = (counts + tm - 1) // tm * tm
    pad_ends = jnp.cumsum(padded)
    pad_starts = pad_ends - padded
    eids = jnp.arange(n_experts, dtype=jnp.int32)

    def dest(k):
        e = slab[:, k].astype(jnp.int32)
        start = jnp.sum(jnp.where(e[:, None] == eids[None, :], pad_starts[None, :], 0), axis=1)
        return start + slab[:, 4 + k].astype(jnp.int32)

    blk_start = jnp.arange(n_blk, dtype=jnp.int32) * tm
    blk_e = jnp.sum((blk_start[:, None] >= pad_ends[None, :]).astype(jnp.int32), axis=1)
    blk_e = jnp.minimum(blk_e, n_experts - 1)
    nblk = pad_ends[-1:] // tm
    return dest(0), dest(1), blk_e, nblk


def _dispatch_kernel(d0_ref, d1_ref, xn_ref, xs_in, xs_out, sem, *, tm):
    del xs_in
    i = pl.program_id(0)
    base = i * tm

    def issue(g, u, priority):
        src = xn_ref.at[g, pl.ds(u, 1)]
        r = base + g * V7X_SUBLANES + u
        pltpu.make_async_copy(src, xs_out.at[pl.ds(d0_ref[r], 1)], sem).start(priority)
        pltpu.make_async_copy(src, xs_out.at[pl.ds(d1_ref[r], 1)], sem).start(1 - priority)

    _issue_rows(tm, issue)
    for _ in range(TOP_K):
        pltpu.make_async_copy(xn_ref, xn_ref, sem).wait()


def _dispatch(d0, d1, xn, xs_base, tm):
    n, w = xn.shape
    sub = V7X_SUBLANES
    grid_spec = pltpu.PrefetchScalarGridSpec(
        num_scalar_prefetch=2,
        grid=(n // tm,),
        in_specs=[pl.BlockSpec((tm // sub, sub, w), lambda i, a, b: (i, 0, 0)),
                  pl.BlockSpec(memory_space=pl.ANY)],
        out_specs=pl.BlockSpec(memory_space=pl.ANY),
        scratch_shapes=[pltpu.SemaphoreType.DMA],
    )
    return pl.pallas_call(
        functools.partial(_dispatch_kernel, tm=tm),
        grid_spec=grid_spec,
        out_shape=jax.ShapeDtypeStruct(xs_base.shape, xs_base.dtype),
        input_output_aliases={3: 0},
        compiler_params=_cparams(("arbitrary",)),
        name="moe_dispatch",
    )(d0, d1, xn.reshape(n // sub, sub, w), xs_base)


def _experts_kernel(be_ref, nb_ref, xs_ref, w1_ref, w3_ref, w2_ref, y_ref, w1b, w3b, w2b):
    i = pl.program_id(0)

    @pl.when(i < nb_ref[0])
    def _():
        e = be_ref[i]
        e_prev = be_ref[jnp.maximum(i - 1, 0)]

        @pl.when((i == 0) | (e != e_prev))
        def _():
            w1b[...] = w1_ref[...].astype(BF16)
            w3b[...] = w3_ref[...].astype(BF16)
            w2b[...] = w2_ref[...].astype(BF16)

        half = xs_ref.shape[1]
        rows = xs_ref.shape[0] // 2
        ups = []
        for r in range(2):
            xa, xb = _unpack_bf16_pairs(xs_ref[r * rows:(r + 1) * rows, :])
            xa = xa.astype(BF16)
            xb = xb.astype(BF16)
            a = _dot(xa, w1b[0:half, :]) + _dot(xb, w1b[half:, :])
            g = _dot(xa, w3b[0:half, :]) + _dot(xb, w3b[half:, :])
            ups.append((a, g))
        for r, (a, g) in enumerate(ups):
            hmid = (a * _sigmoid(a)) * g
            y_ref[r * rows:(r + 1) * rows, :] = _pack_bf16_pairs(_dot(hmid.astype(BF16), w2b[...]))


def _experts(blk_e, nblk, xs, w1, w3, w2, layer, tm):
    n_rows = xs.shape[0]
    d = w1.shape[-2]
    f = w1.shape[-1]
    wmap = lambda i, be, nb: (layer, be[i], 0, 0)
    xmap = lambda i, be, nb: (jnp.minimum(i, nb[0] - 1), 0)
    grid_spec = pltpu.PrefetchScalarGridSpec(
        num_scalar_prefetch=2,
        grid=(n_rows // tm,),
        in_specs=[
            pl.BlockSpec((tm,) + xs.shape[1:], xmap),
            pl.BlockSpec((None, None, d, f), wmap),
            pl.BlockSpec((None, None, d, f), wmap),
            pl.BlockSpec((None, None, f, d), wmap),
        ],
        out_specs=pl.BlockSpec((tm,) + xs.shape[1:], xmap),
        scratch_shapes=[pltpu.VMEM((d, f), BF16), pltpu.VMEM((d, f), BF16), pltpu.VMEM((f, d), BF16)],
    )
    return pl.pallas_call(
        _experts_kernel,
        grid_spec=grid_spec,
        out_shape=jax.ShapeDtypeStruct(xs.shape, xs.dtype),
        input_output_aliases={2: 0},
        compiler_params=_cparams(("arbitrary",)),
        name="moe_experts",
    )(blk_e, nblk, xs, w1, w3, w2)


def _combine_copies(d0_ref, d1_ref, y_hbm, ybuf, sem, tm, tile, slot):
    base = tile * tm

    def issue(g, u, priority):
        r = base + g * V7X_SUBLANES + u
        pltpu.make_async_copy(y_hbm.at[pl.ds(d0_ref[r], 1)], ybuf.at[slot, 0, g, pl.ds(u, 1)],
                              sem.at[slot]).start(priority)
        pltpu.make_async_copy(y_hbm.at[pl.ds(d1_ref[r], 1)], ybuf.at[slot, 1, g, pl.ds(u, 1)],
                              sem.at[slot]).start(1 - priority)

    _issue_rows(tm, issue)


def _ple_kernel(d0_ref, d1_ref, h1_ref, y_hbm, slab_ref, pp_ref, ps_ref, gple_ref, wg_ref, wp_ref,
                gn_ref, *rest, npt, last, tm):
    ybuf, sem = rest[-2:]
    outs = rest[:-2]
    i = pl.program_id(0)
    nt = pl.num_programs(0)
    slot = i % 2
    fetch = functools.partial(_combine_copies, d0_ref, d1_ref, y_hbm, ybuf, sem, tm)

    @pl.when(i == 0)
    def _():
        fetch(0, 0)

    @pl.when(i + 1 < nt)
    def _():
        fetch(i + 1, 1 - slot)

    for k in range(TOP_K):
        pltpu.make_async_copy(ybuf.at[slot, k], ybuf.at[slot, k], sem.at[slot]).wait()

    p = jnp.where(i < npt, pp_ref[...], ps_ref[...]).astype(BF16)
    proj = _dot(p, wp_ref[...])
    slab = slab_ref[...]
    w = ybuf.shape[-1]
    y0 = jnp.concatenate(_unpack_bf16_pairs(ybuf[slot, 0].reshape(tm, w)), axis=1)
    y1 = jnp.concatenate(_unpack_bf16_pairs(ybuf[slot, 1].reshape(tm, w)), axis=1)
    h2 = h1_ref[...] + slab[:, 2:3] * y0 + slab[:, 3:4] * y1
    gate = _sigmoid(_dot(_rms(h2, gple_ref[...]).astype(BF16), wg_ref[...]))
    h3 = h2 + gate * proj
    hn = _rms(h3, gn_ref[...])
    if last:
        yp_ref, ys_ref = outs

        @pl.when(i < npt)
        def _():
            yp_ref[...] = hn

        @pl.when(i >= npt)
        def _():
            ys_ref[...] = hn
    else:
        h_ref, hn_ref = outs
        h_ref[...] = h3
        hn_ref[...] = hn.astype(BF16)


def _ple(d0, d1, h1, y, slab, p_p, p_s, layer, gple, wg, wp, gn, tm, last):
    n, d = h1.shape
    np_ = p_p.shape[1]
    npt = np_ // tm
    pd = p_p.shape[2]
    const = lambda i, a, b: (0, 0)
    rowmap = lambda i, a, b: (i, 0)
    if last:
        out_specs = [pl.BlockSpec((tm, d), lambda i, a, b: (jnp.minimum(i, npt - 1), 0)),
                     pl.BlockSpec((tm, d), lambda i, a, b: (jnp.maximum(i - npt, 0), 0))]
        out_shape = [jax.ShapeDtypeStruct((np_, d), F32), jax.ShapeDtypeStruct((n - np_, d), F32)]
    else:
        out_specs = [pl.BlockSpec((tm, d), rowmap), pl.BlockSpec((tm, d), rowmap)]
        out_shape = [jax.ShapeDtypeStruct((n, d), F32), jax.ShapeDtypeStruct((n, d), BF16)]
    grid_spec = pltpu.PrefetchScalarGridSpec(
        num_scalar_prefetch=2,
        grid=(n // tm,),
        in_specs=[
            pl.BlockSpec((tm, d), rowmap),
            pl.BlockSpec(memory_space=pl.ANY),
            pl.BlockSpec((tm, V7X_LANES), rowmap),
            pl.BlockSpec((None, tm, pd), lambda i, a, b: (layer, jnp.minimum(i, npt - 1), 0)),
            pl.BlockSpec((None, tm, pd), lambda i, a, b: (layer, jnp.maximum(i - npt, 0), 0)),
            pl.BlockSpec((1, d), const), pl.BlockSpec((d, d), const), pl.BlockSpec((pd, d), const),
            pl.BlockSpec((1, d), const),
        ],
        out_specs=out_specs,
        scratch_shapes=[pltpu.VMEM((2, TOP_K, tm // V7X_SUBLANES, V7X_SUBLANES, y.shape[1]), y.dtype),
                        pltpu.SemaphoreType.DMA((2,))],
    )
    return pl.pallas_call(
        functools.partial(_ple_kernel, npt=npt, last=last, tm=tm),
        grid_spec=grid_spec,
        out_shape=out_shape,
        compiler_params=_cparams(("arbitrary",)),
        name="ple",
    )(d0, d1, h1, y, slab, p_p, p_s, gple, wg, wp, gn)


def _rope_tables(pos, rope):
    half = rope // 2
    inv = 1.0 / (ROPE_THETA ** (jnp.arange(half, dtype=F32) * (2.0 / rope)))
    ang = pos.astype(F32)[:, None] * inv[None, :]
    pad = ((0, 0), (0, V7X_LANES - rope))
    cos = jnp.pad(jnp.tile(jnp.cos(ang), (1, 2)), pad)
    sin = jnp.pad(jnp.tile(jnp.sin(ang), (1, 2)), pad)
    return cos, sin


def _rot_last(w):
    half = w.shape[-1] // 2
    return jnp.concatenate([-w[..., half:], w[..., :half]], axis=-1)


def _pad_last(w, width):
    return jnp.pad(w, [(0, 0)] * (w.ndim - 1) + [(0, width - w.shape[-1])])


def kernel(x_prompt, x_sample, state_pool, state_conv, cache_latent, cache_krope, page_table, p_prompt, p_sample, norm_mix, norm_ffn, norm_ple, norm_final, pool_w_in, pool_w_grp, pool_scale, conv_w_pw1, conv_b_pw1, conv_w_dw, conv_b_dw, conv_ln_g, conv_ln_b, conv_w_pw2, conv_b_pw2, mla_w_dqkv, mla_q_norm, mla_kv_norm, mla_w_uq, mla_w_uk, mla_w_uv, mla_w_o, moe_w_group, moe_b_group, moe_w_expert, moe_b_expert, moe_w1, moe_w3, moe_w2, ple_w_gate, ple_w_proj):
    b, t, d = x_prompt.shape
    bs, ts, _ = x_sample.shape
    depth = norm_mix.shape[0]
    np_, ns = b * t, bs * ts
    n = np_ + ns
    tm = min(TOKEN_TILE, ns)
    assert np_ % tm == 0 and ns % tm == 0
    past_len = page_table.shape[1] * cache_latent.shape[2]
    n_groups = moe_w_group.shape[-1]
    n_experts = moe_w_expert.shape[-1]
    epg = n_experts // n_groups
    assert n_groups + n_experts <= V7X_LANES
    n_rows = (n * TOP_K + n_experts * (MOE_TILE - 1) + MOE_TILE - 1) // MOE_TILE * MOE_TILE
    pd = p_prompt.shape[-1]
    row = lambda v: v.reshape(1, -1)

    nh, kv_lora, nope = mla_w_uk.shape[1:]
    rope = cache_krope.shape[-1]
    q_lora = mla_w_dqkv.shape[-1] - kv_lora - rope
    scale = float((nope + rope) ** -0.5)
    mla_dims = (q_lora, kv_lora, rope, nh, nope, scale)
    cache_kr_t = jnp.swapaxes(cache_krope, 2, 3)

    h, hn = _prep(x_prompt.reshape(np_, d), x_sample.reshape(ns, d), row(norm_mix[0]), tm)
    p_p = p_prompt.reshape(depth, np_, pd)
    p_s = p_sample.reshape(depth, ns, pd)

    xs_base = jnp.zeros((n_rows, d // 2), jnp.uint32)
    new_pool_p, new_pool_s, new_conv_p, new_conv_s = [], [], [], []
    new_lat_p, new_kr_p, new_lat_s, new_kr_s = [], [], [], []
    y_p = y_s = None
    for i in range(depth):
        kind, j = i % N_MIXERS, i // N_MIXERS
        if kind == 0:
            w_in = pool_w_in[j].astype(BF16)
            w_grp = pool_w_grp[j].astype(BF16)
            sc = row(pool_scale[j])
            mo_p, st_p = _pool_prompt(hn, w_in, w_grp, sc, b, t, min(tm, t))
            mo_s, st_s = _pool_sample(hn, np_, state_pool[j], w_in, w_grp, sc, bs, ts, past_len)
            new_pool_p.append(st_p)
            new_pool_s.append(st_s)
        elif kind == 1:
            w1 = conv_w_pw1[j].astype(BF16)
            w2 = conv_w_pw2[j].astype(BF16)
            args = (w1, row(conv_b_pw1[j]), conv_w_dw[j], row(conv_b_dw[j]), row(conv_ln_g[j]),
                    row(conv_ln_b[j]), w2, row(conv_b_pw2[j]))
            mo_p, st_p = _conv_prompt(hn, *args, b, t, min(CONV_TILE, t))
            mo_s, st_s = _conv_sample(hn, np_, state_conv[j], *args, bs, ts)
            new_conv_p.append(st_p)
            new_conv_s.append(st_s)
        else:
            wd = mla_w_dqkv[j]
            o = q_lora + kv_lora
            wd_k = wd[:, o:]
            wd = jnp.concatenate([wd[:, :o], _pad_last(wd_k, V7X_LANES),
                                  _pad_last(_rot_last(wd_k), V7X_LANES)], axis=1).astype(BF16)
            wuq = mla_w_uq[j].reshape(q_lora, nh, nope + rope)
            wq_rope = wuq[:, :, nope:]
            wuq = jnp.concatenate([
                wuq[:, :, :nope].reshape(q_lora, nh * nope),
                _pad_last(wq_rope, V7X_LANES).reshape(q_lora, nh * V7X_LANES),
                _pad_last(_rot_last(wq_rope), V7X_LANES).reshape(q_lora, nh * V7X_LANES)],
                axis=1).astype(BF16)
            wuk = jnp.swapaxes(mla_w_uk[j], 1, 2).astype(BF16)
            wuv = mla_w_uv[j].astype(BF16)
            wo = mla_w_o[j].astype(BF16)
            qn, kvn = row(mla_q_norm[j]), row(mla_kv_norm[j])
            tmp = min(tm, t)
            ntj = t // tmp
            cos_p, sin_p = _rope_tables(jnp.arange(t, dtype=jnp.int32), rope)
            ckv_p, kpe_p, kcat_p, qcat_p = _mla_qkv(
                hn, 0, np_, wd, qn, kvn, wuq, wuk, cos_p, sin_p, lambda ii: (ii % ntj, 0), tmp,
                mla_dims, BF16)
            cos_s, sin_s = _rope_tables(past_len + jnp.arange(ts, dtype=jnp.int32), rope)
            cos_s, sin_s = jnp.tile(cos_s, (bs, 1)), jnp.tile(sin_s, (bs, 1))
            ckv_s, kpe_s, _, qcat_s = _mla_qkv(
                hn, np_ // tm, ns, wd, qn, kvn, wuq, wuk, cos_s, sin_s, lambda ii: (ii, 0), tm,
                mla_dims, F32)
            o_p = _attn_prompt(qcat_p, kcat_p, b, t, nh, kv_lora)
            o_s = _attn_sample(page_table, qcat_s, ckv_s, kpe_s, cache_latent, cache_kr_t, j, bs, ts, nh)
            mo_p = _mla_out(o_p, wuv, wo, tm)
            mo_s = _mla_out(o_s, wuv, wo, tm)
            new_lat_p.append(ckv_p.reshape(b, t, kv_lora))
            new_kr_p.append(kpe_p.reshape(b, t, rope))
            new_lat_s.append(ckv_s.reshape(bs, ts, kv_lora))
            new_kr_s.append(kpe_s.reshape(bs, ts, rope))

        wr = _pad_last(jnp.concatenate([moe_w_group[i], moe_w_expert[i]], axis=1), V7X_LANES)
        br = _pad_last(jnp.concatenate([moe_b_group[i], moe_b_expert[i]])[None, :], V7X_LANES)
        whi = wr.astype(BF16)
        wlo = (wr - whi.astype(F32)).astype(BF16)
        h1, xn, slab, cnt = _route(h, mo_p, mo_s, row(norm_ffn[i]), whi, wlo, br, tm, n_groups, epg)
        d0, d1, blk_e, nblk = _dispatch_tables(slab, cnt, n_experts, n_groups, MOE_TILE, n_rows // MOE_TILE)
        xs = _dispatch(d0, d1, xn, xs_base, tm)
        y = _experts(blk_e, nblk, xs, moe_w1, moe_w3, moe_w2, i, MOE_TILE)
        xs_base = y
        last = i == depth - 1
        gn = row(norm_final) if last else row(norm_mix[i + 1])
        out = _ple(d0, d1, h1, y, slab, p_p, p_s, i, row(norm_ple[i]), ple_w_gate[i].astype(BF16),
                   ple_w_proj[i].astype(BF16), gn, tm, last)
        if last:
            y_p, y_s = out
        else:
            h, hn = out

    return (y_p.reshape(b, t, d), y_s.reshape(bs, ts, d),
            jnp.stack(new_pool_p), jnp.stack(new_pool_s),
            jnp.stack(new_conv_p), jnp.stack(new_conv_s),
            jnp.stack(new_lat_p), jnp.stack(new_kr_p),
            jnp.stack(new_lat_s), jnp.stack(new_kr_s))
```

```python
import functools

import jax
import jax.numpy as jnp
from jax import lax
from jax.experimental import pallas as pl
from jax.experimental.pallas import tpu as pltpu

F32 = jnp.float32
BF16 = jnp.bfloat16

POOL_WINDOWS = (2, 4, 8, 16)
TOP_K = 2
ROPE_THETA = 10000.0
RMS_EPS = 1e-6
LN_EPS = 1e-5
NEG_INF = -1e30
N_MIXERS = 3

V7X_LANES = 128
V7X_SUBLANES = 8
V7X_VMEM_LIMIT_BYTES = 48 * 1024 * 1024

TOKEN_TILE = 512
MOE_TILE = 512
CONV_TILE = 256
ATTN_Q_TILE = 512
ATTN_K_TILE = 512
DECODE_PAGE_CHUNK = 8
SAMPLE_SEQ_BLOCK = 32
DMA_ISSUE_UNROLL = 8


def _cparams(sem):
    return pltpu.CompilerParams(dimension_semantics=sem, vmem_limit_bytes=V7X_VMEM_LIMIT_BYTES)


def _rms(x, g):
    ms = jnp.mean(x * x, axis=-1, keepdims=True)
    return x * lax.rsqrt(ms + RMS_EPS) * g


def _dot(a, b):
    return jnp.dot(a, b, preferred_element_type=F32)


def _dot_t(a, b):
    return lax.dot_general(a, b, (((1,), (1,)), ((), ())), preferred_element_type=F32)


def _sigmoid(x):
    return 1.0 / (1.0 + jnp.exp(-x))


def _pack_bf16_pairs(x):
    w = x.shape[1] // 2
    hi = lax.bitcast_convert_type(x[:, :w].astype(BF16).astype(F32), jnp.uint32)
    lo = lax.bitcast_convert_type(x[:, w:].astype(BF16).astype(F32), jnp.uint32)
    return hi | (lo >> 16)


def _unpack_bf16_pairs(p):
    hi = lax.bitcast_convert_type(p & jnp.uint32(0xFFFF0000), F32)
    lo = lax.bitcast_convert_type(p << 16, F32)
    return hi, lo


def _issue_rows(n_rows, issue):
    def group(g, c):
        for u in range(V7X_SUBLANES):
            issue(g, u, u % 2)
        return c

    lax.fori_loop(0, n_rows // V7X_SUBLANES, group, 0)


def _prep_kernel(xp_ref, xs_ref, g_ref, h_ref, hn_ref, *, npt):
    i = pl.program_id(0)
    x = jnp.where(i < npt, xp_ref[...], xs_ref[...])
    h_ref[...] = x
    hn_ref[...] = _rms(x, g_ref[...]).astype(BF16)


def _prep(xp, xs, g, tm):
    np_, d = xp.shape
    ns = xs.shape[0]
    npt, nst = np_ // tm, ns // tm
    n = np_ + ns
    return pl.pallas_call(
        functools.partial(_prep_kernel, npt=npt),
        grid=(npt + nst,),
        in_specs=[
            pl.BlockSpec((tm, d), lambda i: (jnp.minimum(i, npt - 1), 0)),
            pl.BlockSpec((tm, d), lambda i: (jnp.maximum(i - npt, 0), 0)),
            pl.BlockSpec((1, d), lambda i: (0, 0)),
        ],
        out_specs=[pl.BlockSpec((tm, d), lambda i: (i, 0)),
                   pl.BlockSpec((tm, d), lambda i: (i, 0))],
        out_shape=[jax.ShapeDtypeStruct((n, d), F32), jax.ShapeDtypeStruct((n, d), BF16)],
        compiler_params=_cparams(("arbitrary",)),
        name="prep",
    )(xp, xs, g)


def _pool_prompt_kernel(hn_ref, win_ref, wg_ref, sc_ref, mo_ref, st_ref, ext_ref, *, tm, carry, state):
    j = pl.program_id(1)
    nj = pl.num_programs(1)
    d = hn_ref.shape[1]
    gd = d // len(POOL_WINDOWS)
    u = _dot(hn_ref[...], win_ref[...])

    @pl.when(j == 0)
    def _():
        ext_ref[0:carry, :] = jnp.zeros((carry, d), F32)

    @pl.when(j > 0)
    def _():
        ext_ref[0:carry, :] = ext_ref[tm:tm + carry, :]

    ext_ref[carry:carry + tm, :] = u
    pos = j * tm + lax.broadcasted_iota(jnp.int32, (tm, 1), 0)
    for g, w in enumerate(POOL_WINDOWS):
        c0, c1 = g * gd, (g + 1) * gd
        ug = u[:, c0:c1]
        s = ug
        for k in range(1, w):
            s = s + ext_ref[carry - k:carry - k + tm, c0:c1]
        cnt = jnp.minimum(pos + 1, w).astype(F32)
        z = s / cnt - ug
        y = _dot(z.astype(BF16), wg_ref[g])
        mo_ref[:, c0:c1] = y * sc_ref[:, c0:c1]

    @pl.when(j == nj - 1)
    def _():
        st_ref[0] = ext_ref[carry + tm - state:carry + tm, :]


def _pool_prompt(hn, w_in, w_grp, scale, b, t, tm):
    d = hn.shape[1]
    state = max(POOL_WINDOWS) - 1
    carry = 16
    ntj = t // tm
    return pl.pallas_call(
        functools.partial(_pool_prompt_kernel, tm=tm, carry=carry, state=state),
        grid=(b, ntj),
        in_specs=[
            pl.BlockSpec((tm, d), lambda bi, j: (bi * ntj + j, 0)),
            pl.BlockSpec((d, d), lambda bi, j: (0, 0)),
            pl.BlockSpec(w_grp.shape, lambda bi, j: (0, 0, 0)),
            pl.BlockSpec((1, d), lambda bi, j: (0, 0)),
        ],
        out_specs=[pl.BlockSpec((tm, d), lambda bi, j: (bi * ntj + j, 0)),
                   pl.BlockSpec((1, state, d), lambda bi, j: (bi, 0, 0))],
        out_shape=[jax.ShapeDtypeStruct((b * t, d), F32), jax.ShapeDtypeStruct((b, state, d), F32)],
        scratch_shapes=[pltpu.VMEM((carry + tm, d), F32)],
        compiler_params=_cparams(("arbitrary", "arbitrary")),
        name="pool_prompt",
    )(hn, w_in, w_grp, scale)


def _pool_sample_kernel(hn_ref, prev_ref, win_ref, wg_ref, sc_ref, mo_ref, st_ref, u_ref, z_ref,
                        *, bs, t, state, pos0):
    d = hn_ref.shape[1]
    gd = d // len(POOL_WINDOWS)
    u = _dot(hn_ref[...], win_ref[...])
    u_ref[...] = u.reshape(bs, t, d)

    def row(i, c0, c1):
        if i >= 0:
            return u_ref[:, i, c0:c1]
        return prev_ref[:, state + i, c0:c1]

    for g, w in enumerate(POOL_WINDOWS):
        c0, c1 = g * gd, (g + 1) * gd
        for ti in range(t):
            s = row(ti, c0, c1)
            for k in range(1, w):
                s = s + row(ti - k, c0, c1)
            cnt = float(min(pos0 + ti + 1, w))
            z_ref[:, ti, c0:c1] = s / cnt - row(ti, c0, c1)
    z = z_ref[...].reshape(bs * t, d)
    for g in range(len(POOL_WINDOWS)):
        c0, c1 = g * gd, (g + 1) * gd
        y = _dot(z[:, c0:c1].astype(BF16), wg_ref[g])
        mo_ref[:, c0:c1] = y * sc_ref[:, c0:c1]
    if t >= state:
        st_ref[...] = u_ref[:, t - state:t, :]
    else:
        st_ref[:, 0:state - t, :] = prev_ref[:, t:state, :]
        st_ref[:, state - t:state, :] = u_ref[...]


def _pool_sample(hn, row0, prev, w_in, w_grp, scale, b, t, pos0):
    d = hn.shape[1]
    state = prev.shape[1]
    bs = SAMPLE_SEQ_BLOCK
    blk0 = row0 // (bs * t)
    return pl.pallas_call(
        functools.partial(_pool_sample_kernel, bs=bs, t=t, state=state, pos0=pos0),
        grid=(b // bs,),
        in_specs=[
            pl.BlockSpec((bs * t, d), lambda i: (blk0 + i, 0)),
            pl.BlockSpec((bs, state, d), lambda i: (i, 0, 0)),
            pl.BlockSpec((d, d), lambda i: (0, 0)),
            pl.BlockSpec(w_grp.shape, lambda i: (0, 0, 0)),
            pl.BlockSpec((1, d), lambda i: (0, 0)),
        ],
        out_specs=[pl.BlockSpec((bs * t, d), lambda i: (i, 0)),
                   pl.BlockSpec((bs, state, d), lambda i: (i, 0, 0))],
        out_shape=[jax.ShapeDtypeStruct((b * t, d), F32), jax.ShapeDtypeStruct((b, state, d), F32)],
        scratch_shapes=[pltpu.VMEM((bs, t, d), F32), pltpu.VMEM((bs, t, d), F32)],
        compiler_params=_cparams(("arbitrary",)),
        name="pool_sample",
    )(hn, prev, w_in, w_grp, scale)


def _layernorm_silu(c, g, b):
    mu = jnp.mean(c, axis=-1, keepdims=True)
    xc = c - mu
    var = jnp.mean(xc * xc, axis=-1, keepdims=True)
    n = xc * lax.rsqrt(var + LN_EPS) * g + b
    return n * _sigmoid(n)


def _conv_prompt_kernel(hn_ref, w1_ref, b1_ref, wdw_ref, bdw_ref, lg_ref, lb_ref, w2_ref, b2_ref,
                        mo_ref, st_ref, ext_ref, c_ref, tap_ref, *, tm, carry, width, rc):
    j = pl.program_id(1)
    nj = pl.num_programs(1)
    d = hn_ref.shape[1]
    sub = V7X_SUBLANES

    @pl.when((pl.program_id(0) == 0) & (j == 0))
    def _():
        for k in range(width):
            tap_ref[k] = jnp.broadcast_to(wdw_ref[k:k + 1, :], (sub, d))

    a = _dot(hn_ref[...], w1_ref[...]) + b1_ref[...]
    glu = a[:, :d] * _sigmoid(a[:, d:])

    length = carry + tm

    @pl.when(j == 0)
    def _():
        ext_ref[0, 0:carry, :] = jnp.zeros((carry, d), F32)

    @pl.when(j > 0)
    def _():
        ext_ref[0, 0:carry, :] = ext_ref[0, tm:tm + carry, :]

    ext_ref[0, carry:carry + tm, :] = glu
    for m in range(1, V7X_SUBLANES):
        ext_ref[m, 0:length - V7X_SUBLANES, :] = ext_ref[0, m:length - V7X_SUBLANES + m, :]
    off = carry - (width - 1)

    def chunk(ci, _):
        r0 = pl.multiple_of(ci * rc, rc)
        acc = jnp.zeros((rc // sub, sub, d), F32) + bdw_ref[...]
        for k in range(width):
            q, m = divmod(off + k, sub)
            win = ext_ref[m, pl.ds(r0 + q * sub, rc), :].reshape(rc // sub, sub, d)
            acc = acc + win * tap_ref[k]
        c_ref[pl.ds(r0, rc), :] = acc.reshape(rc, d)
        return 0

    lax.fori_loop(0, tm // rc, chunk, 0)
    s = _layernorm_silu(c_ref[...], lg_ref[...], lb_ref[...])
    mo_ref[...] = _dot(s.astype(BF16), w2_ref[...]) + b2_ref[...]

    @pl.when(j == nj - 1)
    def _():
        st_ref[0] = ext_ref[0, carry + tm - (width - 1):carry + tm, :]


def _conv_prompt(hn, w1, b1, wdw, bdw, lg, lb, w2, b2, b, t, tm):
    d = hn.shape[1]
    width = wdw.shape[0]
    carry = 32
    ntj = t // tm
    const = lambda bi, j: (0, 0)
    return pl.pallas_call(
        functools.partial(_conv_prompt_kernel, tm=tm, carry=carry, width=width, rc=16),
        grid=(b, ntj),
        in_specs=[
            pl.BlockSpec((tm, d), lambda bi, j: (bi * ntj + j, 0)),
            pl.BlockSpec((d, 2 * d), const), pl.BlockSpec((1, 2 * d), const),
            pl.BlockSpec((width, d), const), pl.BlockSpec((1, d), const),
            pl.BlockSpec((1, d), const), pl.BlockSpec((1, d), const),
            pl.BlockSpec((d, d), const), pl.BlockSpec((1, d), const),
        ],
        out_specs=[pl.BlockSpec((tm, d), lambda bi, j: (bi * ntj + j, 0)),
                   pl.BlockSpec((1, width - 1, d), lambda bi, j: (bi, 0, 0))],
        out_shape=[jax.ShapeDtypeStruct((b * t, d), F32),
                   jax.ShapeDtypeStruct((b, width - 1, d), F32)],
        scratch_shapes=[pltpu.VMEM((V7X_SUBLANES, carry + tm, d), F32), pltpu.VMEM((tm, d), F32),
                        pltpu.VMEM((width, V7X_SUBLANES, d), F32)],
        compiler_params=_cparams(("arbitrary", "arbitrary")),
        name="conv_prompt",
    )(hn, w1, b1, wdw, bdw, lg, lb, w2, b2)


def _conv_sample_kernel(hn_ref, prev_ref, w1_ref, b1_ref, wdw_ref, bdw_ref, lg_ref, lb_ref, w2_ref,
                        b2_ref, mo_ref, st_ref, g_ref, c_ref, *, bs, t, width):
    d = hn_ref.shape[1]
    state = width - 1
    a = _dot(hn_ref[...], w1_ref[...]) + b1_ref[...]
    glu = a[:, :d] * _sigmoid(a[:, d:])
    g_ref[...] = glu.reshape(bs, t, d)

    def row(i):
        if i >= 0:
            return g_ref[:, i, :]
        return prev_ref[:, state + i, :]

    for ti in range(t):
        acc = jnp.zeros((bs, d), F32) + bdw_ref[...]
        for k in range(width):
            acc = acc + row(ti + k - state) * wdw_ref[k:k + 1, :]
        c_ref[:, ti, :] = acc
    s = _layernorm_silu(c_ref[...].reshape(bs * t, d), lg_ref[...], lb_ref[...])
    mo_ref[...] = _dot(s.astype(BF16), w2_ref[...]) + b2_ref[...]
    if t >= state:
        st_ref[...] = g_ref[:, t - state:t, :]
    else:
        st_ref[:, 0:state - t, :] = prev_ref[:, t:state, :]
        st_ref[:, state - t:state, :] = g_ref[...]


def _conv_sample(hn, row0, prev, w1, b1, wdw, bdw, lg, lb, w2, b2, b, t):
    d = hn.shape[1]
    width = wdw.shape[0]
    bs = SAMPLE_SEQ_BLOCK
    blk0 = row0 // (bs * t)
    const = lambda i: (0, 0)
    return pl.pallas_call(
        functools.partial(_conv_sample_kernel, bs=bs, t=t, width=width),
        grid=(b // bs,),
        in_specs=[
            pl.BlockSpec((bs * t, d), lambda i: (blk0 + i, 0)),
            pl.BlockSpec((bs, width - 1, d), lambda i: (i, 0, 0)),
            pl.BlockSpec((d, 2 * d), const), pl.BlockSpec((1, 2 * d), const),
            pl.BlockSpec((width, d), const), pl.BlockSpec((1, d), const),
            pl.BlockSpec((1, d), const), pl.BlockSpec((1, d), const),
            pl.BlockSpec((d, d), const), pl.BlockSpec((1, d), const),
        ],
        out_specs=[pl.BlockSpec((bs * t, d), lambda i: (i, 0)),
                   pl.BlockSpec((bs, width - 1, d), lambda i: (i, 0, 0))],
        out_shape=[jax.ShapeDtypeStruct((b * t, d), F32),
                   jax.ShapeDtypeStruct((b, width - 1, d), F32)],
        scratch_shapes=[pltpu.VMEM((bs, t, d), F32), pltpu.VMEM((bs, t, d), F32)],
        compiler_params=_cparams(("arbitrary",)),
        name="conv_sample",
    )(hn, prev, w1, b1, wdw, bdw, lg, lb, w2, b2)


def _mla_qkv_kernel(hn_ref, wd_ref, qn_ref, kvn_ref, wuq_ref, wuk_ref, cos_ref, sin_ref,
                    ckv_ref, kpe_ref, kcat_ref, qcat_ref, *, q_lora, kv_lora, rope, nh, nope, scale):
    lanes = V7X_LANES
    qk = kv_lora + lanes
    o = q_lora + kv_lora
    qn = nh * nope
    rows = hn_ref.shape[0] // 2
    halves = [(r * rows, (r + 1) * rows) for r in range(2)]
    dqs = [_dot(hn_ref[a:b, :], wd_ref[...]) for a, b in halves]
    qs = []
    for (a, b), dq in zip(halves, dqs):
        cq = _rms(dq[:, :q_lora], qn_ref[...]).astype(BF16)
        ckv = _rms(dq[:, q_lora:q_lora + kv_lora], kvn_ref[...])
        kblk = dq[:, o:o + lanes] * cos_ref[a:b, :] + dq[:, o + lanes:o + 2 * lanes] * sin_ref[a:b, :]
        ckv_ref[a:b, :] = ckv
        kpe_ref[a:b, :] = kblk[:, :rope]
        kcat_ref[a:b, :kv_lora] = ckv.astype(BF16)
        kcat_ref[a:b, kv_lora:] = kblk.astype(BF16)
        qs.append(_dot(cq, wuq_ref[...]))
    for (a, b), q in zip(halves, qs):
        cos = cos_ref[a:b, :]
        sin = sin_ref[a:b, :]
        for h in range(nh):
            ql = _dot(q[:, h * nope:(h + 1) * nope].astype(BF16), wuk_ref[h])
            r0 = qn + h * lanes
            r1 = qn + (nh + h) * lanes
            qpe = q[:, r0:r0 + lanes] * cos + q[:, r1:r1 + lanes] * sin
            qcat_ref[a:b, h * qk:h * qk + kv_lora] = (ql * scale).astype(qcat_ref.dtype)
            qcat_ref[a:b, h * qk + kv_lora:(h + 1) * qk] = (qpe * scale).astype(qcat_ref.dtype)


def _mla_qkv(hn, blk0, nrows, wd, qn, kvn, wuq, wuk, cos, sin, cos_map, tm, dims, q_dtype):
    q_lora, kv_lora, rope, nh, nope, scale = dims
    d = hn.shape[1]
    qk = kv_lora + V7X_LANES
    const2 = lambda i: (0, 0)
    return pl.pallas_call(
        functools.partial(_mla_qkv_kernel, q_lora=q_lora, kv_lora=kv_lora, rope=rope, nh=nh,
                          nope=nope, scale=scale),
        grid=(nrows // tm,),
        in_specs=[
            pl.BlockSpec((tm, d), lambda i: (blk0 + i, 0)),
            pl.BlockSpec(wd.shape, const2), pl.BlockSpec((1, q_lora), const2),
            pl.BlockSpec((1, kv_lora), const2), pl.BlockSpec(wuq.shape, const2),
            pl.BlockSpec(wuk.shape, lambda i: (0, 0, 0)),
            pl.BlockSpec((tm, V7X_LANES), cos_map), pl.BlockSpec((tm, V7X_LANES), cos_map),
        ],
        out_specs=[pl.BlockSpec((tm, kv_lora), lambda i: (i, 0)),
                   pl.BlockSpec((tm, rope), lambda i: (i, 0)),
                   pl.BlockSpec((tm, qk), lambda i: (i, 0)),
                   pl.BlockSpec((tm, nh * qk), lambda i: (i, 0))],
        out_shape=[jax.ShapeDtypeStruct((nrows, kv_lora), F32),
                   jax.ShapeDtypeStruct((nrows, rope), F32),
                   jax.ShapeDtypeStruct((nrows, qk), BF16),
                   jax.ShapeDtypeStruct((nrows, nh * qk), q_dtype)],
        compiler_params=_cparams(("arbitrary",)),
        name="mla_qkv",
    )(hn, wd, qn, kvn, wuq, wuk, cos, sin)


def _softmax_update(s, v, m_ref, l_ref, acc_ref):
    m_old = m_ref[...]
    m_new = jnp.maximum(m_old, jnp.max(s, axis=1, keepdims=True))
    alpha = jnp.exp(m_old - m_new)
    p = jnp.exp(s - m_new)
    l_ref[...] = alpha * l_ref[...] + jnp.sum(p, axis=1, keepdims=True)
    acc_ref[...] = alpha * acc_ref[...] + _dot(p.astype(BF16), v)
    m_ref[...] = m_new


def _attn_prompt_kernel(q_ref, k_ref, o_ref, m_ref, l_ref, acc_ref, *, tq, tk, nh, qk, kv):
    i = pl.program_id(1)
    m_ref[...] = jnp.full(m_ref.shape, NEG_INF, F32)
    l_ref[...] = jnp.zeros(l_ref.shape, F32)
    acc_ref[...] = jnp.zeros(acc_ref.shape, F32)
    qpos = i * tq + lax.broadcasted_iota(jnp.int32, (tq, 1), 0)
    n_full = (i * tq + 1) // tk
    n_chunks = (i * tq + tq - 1) // tk + 1

    def make_body(masked):
        def body(kk, c):
            ks = pl.multiple_of(kk * tk, tk)
            kc = k_ref[pl.ds(ks, tk), :]
            v = kc[:, :kv]
            if masked:
                vis = ks + lax.broadcasted_iota(jnp.int32, (tq, tk), 1) <= qpos
            scores = [_dot_t(q_ref[:, h * qk:(h + 1) * qk], kc) for h in range(nh)]
            for h in range(nh):
                s = jnp.where(vis, scores[h], NEG_INF) if masked else scores[h]
                _softmax_update(s, v, m_ref.at[h], l_ref.at[h], acc_ref.at[h])
            return c
        return body

    lax.fori_loop(0, n_full, make_body(False), 0)
    lax.fori_loop(n_full, n_chunks, make_body(True), 0)
    for h in range(nh):
        o_ref[:, h * kv:(h + 1) * kv] = (acc_ref[h] / l_ref[h]).astype(BF16)


def _attn_prompt(qcat, kcat, b, t, nh, kv):
    qk = kcat.shape[1]
    tq = min(ATTN_Q_TILE, t)
    tk = min(ATTN_K_TILE, t)
    nq = t // tq
    return pl.pallas_call(
        functools.partial(_attn_prompt_kernel, tq=tq, tk=tk, nh=nh, qk=qk, kv=kv),
        grid=(b, nq),
        in_specs=[
            pl.BlockSpec((tq, nh * qk), lambda bi, i: (bi * nq + i, 0)),
            pl.BlockSpec((t, qk), lambda bi, i: (bi, 0)),
        ],
        out_specs=pl.BlockSpec((tq, nh * kv), lambda bi, i: (bi * nq + i, 0)),
        out_shape=jax.ShapeDtypeStruct((b * t, nh * kv), BF16),
        scratch_shapes=[pltpu.VMEM((nh, tq, 1), F32), pltpu.VMEM((nh, tq, 1), F32),
                        pltpu.VMEM((nh, tq, kv), F32)],
        compiler_params=_cparams(("arbitrary", "arbitrary")),
        name="mla_attn_prompt",
    )(qcat, kcat)


def _page_copies(pt_ref, lat_hbm, kr_hbm, lat_buf, kr_buf, sem, layer, npages, seq, slot):
    def body(p, c):
        idx = pt_ref[seq * npages + p]
        pltpu.make_async_copy(lat_hbm.at[layer, idx], lat_buf.at[slot, p], sem.at[0, slot]).start()
        pltpu.make_async_copy(kr_hbm.at[layer, idx], kr_buf.at[slot, p], sem.at[1, slot]).start()
        return c

    lax.fori_loop(0, npages, body, 0, unroll=DMA_ISSUE_UNROLL)


def _attn_sample_kernel(pt_ref, q_ref, cn_ref, krn_ref, lat_hbm, kr_hbm, o_ref,
                        lat_buf, kr_buf, sem, s_ref, cb_ref,
                        *, layer, npages, chunk, t, nh, qk, kv, rope, page):
    bi = pl.program_id(0)
    nb = pl.num_programs(0)
    slot = bi % 2
    fetch = functools.partial(_page_copies, pt_ref, lat_hbm, kr_hbm, lat_buf, kr_buf, sem, layer, npages)

    @pl.when(bi == 0)
    def _():
        fetch(0, 0)

    @pl.when(bi + 1 < nb)
    def _():
        fetch(bi + 1, 1 - slot)

    q = jnp.concatenate([q_ref[:, h * qk:(h + 1) * qk] for h in range(nh)], axis=0).astype(BF16)
    ql = q[:, :kv]
    qp = q[:, kv:kv + rope]

    pltpu.make_async_copy(lat_hbm.at[layer, pl.ds(0, npages)], lat_buf.at[slot], sem.at[0, slot]).wait()
    pltpu.make_async_copy(kr_hbm.at[layer, pl.ds(0, npages)], kr_buf.at[slot], sem.at[1, slot]).wait()

    width = chunk * page
    n_chunks = npages // chunk
    for ch in range(n_chunks):
        c = lat_buf[slot, ch * chunk:(ch + 1) * chunk].reshape(width, kv).astype(BF16)
        cb_ref[ch * width:(ch + 1) * width, :] = c
        krt = jnp.concatenate([kr_buf[slot, ch * chunk + p] for p in range(chunk)], axis=1).astype(BF16)
        s_ref[:, ch * width:(ch + 1) * width] = _dot_t(ql, c) + _dot(qp, krt)

    rows = nh * t
    cn = jnp.concatenate([cn_ref[...], jnp.zeros((page - t, kv), F32)], axis=0).astype(BF16)
    krn = jnp.concatenate([krn_ref[...], jnp.zeros((page - t, rope), F32)], axis=0).astype(BF16)
    sn = _dot_t(ql, cn) + _dot_t(qp, krn)
    qstep = lax.broadcasted_iota(jnp.int32, (rows, 1), 0) & (t - 1)
    kstep = lax.broadcasted_iota(jnp.int32, (rows, page), 1)
    sn = jnp.where(kstep <= qstep, sn, NEG_INF)

    m = jnp.maximum(jnp.max(s_ref[...], axis=1, keepdims=True), jnp.max(sn, axis=1, keepdims=True))
    pn = jnp.exp(sn - m)
    l = jnp.sum(pn, axis=1, keepdims=True)
    acc = _dot(pn.astype(BF16), cn)
    for ch in range(n_chunks):
        p = jnp.exp(s_ref[:, ch * width:(ch + 1) * width] - m)
        l = l + jnp.sum(p, axis=1, keepdims=True)
        acc = acc + _dot(p.astype(BF16), cb_ref[ch * width:(ch + 1) * width, :])
    o = acc / l
    for h in range(nh):
        o_ref[:, h * kv:(h + 1) * kv] = o[h * t:(h + 1) * t, :]


def _attn_sample(page_table, qcat, cn, krn, cache_lat, cache_kr_t, layer, b, t, nh):
    kv = cn.shape[1]
    rope = krn.shape[1]
    qk = qcat.shape[1] // nh
    page = cache_lat.shape[2]
    npages = page_table.shape[1]
    chunk = min(DECODE_PAGE_CHUNK, npages)
    assert npages % chunk == 0 and t % V7X_SUBLANES == 0 and t <= page and t & (t - 1) == 0
    rows = t * nh
    grid_spec = pltpu.PrefetchScalarGridSpec(
        num_scalar_prefetch=1,
        grid=(b,),
        in_specs=[
            pl.BlockSpec((t, nh * qk), lambda bi, ptr: (bi, 0)),
            pl.BlockSpec((t, kv), lambda bi, ptr: (bi, 0)),
            pl.BlockSpec((t, rope), lambda bi, ptr: (bi, 0)),
            pl.BlockSpec(memory_space=pl.ANY),
            pl.BlockSpec(memory_space=pl.ANY),
        ],
        out_specs=pl.BlockSpec((t, nh * kv), lambda bi, ptr: (bi, 0)),
        scratch_shapes=[
            pltpu.VMEM((2, npages, page, kv), F32),
            pltpu.VMEM((2, npages, rope, page), F32),
            pltpu.SemaphoreType.DMA((2, 2)),
            pltpu.VMEM((rows, npages * page), F32), pltpu.VMEM((npages * page, kv), BF16),
        ],
    )
    return pl.pallas_call(
        functools.partial(_attn_sample_kernel, layer=layer, npages=npages, chunk=chunk, t=t, nh=nh,
                          qk=qk, kv=kv, rope=rope, page=page),
        grid_spec=grid_spec,
        out_shape=jax.ShapeDtypeStruct((b * t, nh * kv), F32),
        compiler_params=_cparams(("arbitrary",)),
        name="mla_attn_sample",
    )(page_table.reshape(-1), qcat, cn, krn, cache_lat, cache_kr_t)


def _mla_out_kernel(o_ref, wuv_ref, wo_ref, mo_ref, *, nh, kv):
    vs = [_dot(o_ref[:, h * kv:(h + 1) * kv].astype(BF16), wuv_ref[h]).astype(BF16) for h in range(nh)]
    mo_ref[...] = _dot(jnp.concatenate(vs, axis=1), wo_ref[...])


def _mla_out(o, wuv, wo, tm):
    n = o.shape[0]
    nh, kv, vd = wuv.shape
    d = wo.shape[1]
    return pl.pallas_call(
        functools.partial(_mla_out_kernel, nh=nh, kv=kv),
        grid=(n // tm,),
        in_specs=[pl.BlockSpec((tm, nh * kv), lambda i: (i, 0)),
                  pl.BlockSpec(wuv.shape, lambda i: (0, 0, 0)),
                  pl.BlockSpec(wo.shape, lambda i: (0, 0))],
        out_specs=pl.BlockSpec((tm, d), lambda i: (i, 0)),
        out_shape=jax.ShapeDtypeStruct((n, d), F32),
        compiler_params=_cparams(("arbitrary",)),
        name="mla_out",
    )(o, wuv, wo)


def _route_kernel(h_ref, mop_ref, mos_ref, g_ref, whi_ref, wlo_ref, br_ref,
                  h1_ref, xn_ref, slab_ref, cnt_ref, carry_ref, *, npt, tm, n_groups, epg):
    i = pl.program_id(0)

    @pl.when(i == 0)
    def _():
        carry_ref[...] = jnp.zeros(carry_ref.shape, F32)

    h1 = h_ref[...] + jnp.where(i < npt, mop_ref[...], mos_ref[...])
    h1_ref[...] = h1
    xn = _rms(h1, g_ref[...])
    xn_ref[...] = _pack_bf16_pairs(xn)
    x_hi = xn.astype(BF16)
    x_lo = (xn - x_hi.astype(F32)).astype(BF16)
    whi = whi_ref[...]
    logits = _dot(x_hi, whi) + _dot(x_lo, whi) + _dot(x_hi, wlo_ref[...]) + br_ref[...]

    lane = lax.broadcasted_iota(jnp.int32, (tm, V7X_LANES), 1).astype(F32)
    big = float(V7X_LANES)
    gl = jnp.where(lane < n_groups, logits, NEG_INF)
    gmax = jnp.max(gl, axis=1, keepdims=True)
    gsel = jnp.min(jnp.where(gl == gmax, lane, big), axis=1, keepdims=True)
    gprob = 1.0 / jnp.sum(jnp.exp(gl - gmax), axis=1, keepdims=True)
    lo = n_groups + gsel * epg
    el = jnp.where(lane >= lo, jnp.where(lane < lo + epg, logits, NEG_INF), NEG_INF)
    m1 = jnp.max(el, axis=1, keepdims=True)
    i1 = jnp.min(jnp.where(el == m1, lane, big), axis=1, keepdims=True)
    el2 = jnp.where(lane == i1, NEG_INF, el)
    m2 = jnp.max(el2, axis=1, keepdims=True)
    i2 = jnp.min(jnp.where(el2 == m2, lane, big), axis=1, keepdims=True)
    r = jnp.exp(m2 - m1)
    g1 = gprob / (1.0 + r)
    g2 = gprob * r / (1.0 + r)

    oh1 = jnp.where(lane == i1, 1.0, 0.0)
    oh2 = jnp.where(lane == i2, 1.0, 0.0)
    oh = oh1 + oh2
    rr = lax.broadcasted_iota(jnp.int32, (tm, tm), 0)
    cc = lax.broadcasted_iota(jnp.int32, (tm, tm), 1)
    tri = jnp.where(rr > cc, 1.0, 0.0).astype(BF16)
    carry = carry_ref[0:1, :]
    cum = _dot(tri, oh.astype(BF16)) + carry
    rank1 = jnp.sum(oh1 * cum, axis=1, keepdims=True)
    rank2 = jnp.sum(oh2 * cum, axis=1, keepdims=True)
    new_carry = carry + jnp.sum(oh, axis=0, keepdims=True)
    carry_ref[...] = jnp.broadcast_to(new_carry, carry_ref.shape)
    cnt_ref[...] = jnp.broadcast_to(new_carry, cnt_ref.shape)

    slab = jnp.where(lane == 0, i1 - n_groups, 0.0)
    slab = jnp.where(lane == 1, i2 - n_groups, slab)
    slab = jnp.where(lane == 2, g1, slab)
    slab = jnp.where(lane == 3, g2, slab)
    slab = jnp.where(lane == 4, rank1, slab)
    slab = jnp.where(lane == 5, rank2, slab)
    slab_ref[...] = slab


def _route(h, mo_p, mo_s, g, whi, wlo, br, tm, n_groups, epg):
    n, d = h.shape
    npt = mo_p.shape[0] // tm
    const = lambda i: (0, 0)
    return pl.pallas_call(
        functools.partial(_route_kernel, npt=npt, tm=tm, n_groups=n_groups, epg=epg),
        grid=(n // tm,),
        in_specs=[
            pl.BlockSpec((tm, d), lambda i: (i, 0)),
            pl.BlockSpec((tm, d), lambda i: (jnp.minimum(i, npt - 1), 0)),
            pl.BlockSpec((tm, d), lambda i: (jnp.maximum(i - npt, 0), 0)),
            pl.BlockSpec((1, d), const),
            pl.BlockSpec((d, V7X_LANES), const), pl.BlockSpec((d, V7X_LANES), const),
            pl.BlockSpec((1, V7X_LANES), const),
        ],
        out_specs=[pl.BlockSpec((tm, d), lambda i: (i, 0)),
                   pl.BlockSpec((tm, d // 2), lambda i: (i, 0)),
                   pl.BlockSpec((tm, V7X_LANES), lambda i: (i, 0)),
                   pl.BlockSpec((V7X_SUBLANES, V7X_LANES), const)],
        out_shape=[jax.ShapeDtypeStruct((n, d), F32), jax.ShapeDtypeStruct((n, d // 2), jnp.uint32),
                   jax.ShapeDtypeStruct((n, V7X_LANES), F32),
                   jax.ShapeDtypeStruct((V7X_SUBLANES, V7X_LANES), F32)],
        scratch_shapes=[pltpu.VMEM((V7X_SUBLANES, V7X_LANES), F32)],
        compiler_params=_cparams(("arbitrary",)),
        name="route",
    )(h, mo_p, mo_s, g, whi, wlo, br)


def _dispatch_tables(slab, cnt, n_experts, n_groups, tm, n_blk):
    counts = cnt[0, n_groups:n_groups + n_experts].astype(jnp.int32)
    padded = (counts + tm - 1) // tm * tm
    pad_ends = jnp.cumsum(padded)
    pad_starts = pad_ends - padded
    eids = jnp.arange(n_experts, dtype=jnp.int32)

    def dest(k):
        e = slab[:, k].astype(jnp.int32)
        start = jnp.sum(jnp.where(e[:, None] == eids[None, :], pad_starts[None, :], 0), axis=1)
        return start + slab[:, 4 + k].astype(jnp.int32)

    blk_start = jnp.arange(n_blk, dtype=jnp.int32) * tm
    blk_e = jnp.sum((blk_start[:, None] >= pad_ends[None, :]).astype(jnp.int32), axis=1)
    blk_e = jnp.minimum(blk_e, n_experts - 1)
    nblk = pad_ends[-1:] // tm
    return dest(0), dest(1), blk_e, nblk


def _dispatch_kernel(d0_ref, d1_ref, xn_ref, xs_in, xs_out, sem, *, tm):
    del xs_in
    i = pl.program_id(0)
    base = i * tm

    def issue(g, u, priority):
        src = xn_ref.at[g, pl.ds(u, 1)]
        r = base + g * V7X_SUBLANES + u
        pltpu.make_async_copy(src, xs_out.at[pl.ds(d0_ref[r], 1)], sem).start(priority)
        pltpu.make_async_copy(src, xs_out.at[pl.ds(d1_ref[r], 1)], sem).start(1 - priority)

    _issue_rows(tm, issue)
    for _ in range(TOP_K):
        pltpu.make_async_copy(xn_ref, xn_ref, sem).wait()


def _dispatch(d0, d1, xn, xs_base, tm):
    n, w = xn.shape
    sub = V7X_SUBLANES
    grid_spec = pltpu.PrefetchScalarGridSpec(
        num_scalar_prefetch=2,
        grid=(n // tm,),
        in_specs=[pl.BlockSpec((tm // sub, sub, w), lambda i, a, b: (i, 0, 0)),
                  pl.BlockSpec(memory_space=pl.ANY)],
        out_specs=pl.BlockSpec(memory_space=pl.ANY),
        scratch_shapes=[pltpu.SemaphoreType.DMA],
    )
    return pl.pallas_call(
        functools.partial(_dispatch_kernel, tm=tm),
        grid_spec=grid_spec,
        out_shape=jax.ShapeDtypeStruct(xs_base.shape, xs_base.dtype),
        input_output_aliases={3: 0},
        compiler_params=_cparams(("arbitrary",)),
        name="moe_dispatch",
    )(d0, d1, xn.reshape(n // sub, sub, w), xs_base)


def _experts_kernel(be_ref, nb_ref, xs_ref, w1_ref, w3_ref, w2_ref, y_ref, w1b, w3b, w2b):
    i = pl.program_id(0)

    @pl.when(i < nb_ref[0])
    def _():
        e = be_ref[i]
        e_prev = be_ref[jnp.maximum(i - 1, 0)]

        @pl.when((i == 0) | (e != e_prev))
        def _():
            w1b[...] = w1_ref[...].astype(BF16)
            w3b[...] = w3_ref[...].astype(BF16)
            w2b[...] = w2_ref[...].astype(BF16)

        half = xs_ref.shape[1]
        rows = xs_ref.shape[0] // 2
        ups = []
        for r in range(2):
            xa, xb = _unpack_bf16_pairs(xs_ref[r * rows:(r + 1) * rows, :])
            xa = xa.astype(BF16)
            xb = xb.astype(BF16)
            a = _dot(xa, w1b[0:half, :]) + _dot(xb, w1b[half:, :])
            g = _dot(xa, w3b[0:half, :]) + _dot(xb, w3b[half:, :])
            ups.append((a, g))
        for r, (a, g) in enumerate(ups):
            hmid = (a * _sigmoid(a)) * g
            y_ref[r * rows:(r + 1) * rows, :] = _pack_bf16_pairs(_dot(hmid.astype(BF16), w2b[...]))


def _experts(blk_e, nblk, xs, w1, w3, w2, layer, tm):
    n_rows = xs.shape[0]
    d = w1.shape[-2]
    f = w1.shape[-1]
    wmap = lambda i, be, nb: (layer, be[i], 0, 0)
    xmap = lambda i, be, nb: (jnp.minimum(i, nb[0] - 1), 0)
    grid_spec = pltpu.PrefetchScalarGridSpec(
        num_scalar_prefetch=2,
        grid=(n_rows // tm,),
        in_specs=[
            pl.BlockSpec((tm,) + xs.shape[1:], xmap),
            pl.BlockSpec((None, None, d, f), wmap),
            pl.BlockSpec((None, None, d, f), wmap),
            pl.BlockSpec((None, None, f, d), wmap),
        ],
        out_specs=pl.BlockSpec((tm,) + xs.shape[1:], xmap),
        scratch_shapes=[pltpu.VMEM((d, f), BF16), pltpu.VMEM((d, f), BF16), pltpu.VMEM((f, d), BF16)],
    )
    return pl.pallas_call(
        _experts_kernel,
        grid_spec=grid_spec,
        out_shape=jax.ShapeDtypeStruct(xs.shape, xs.dtype),
        input_output_aliases={2: 0},
        compiler_params=_cparams(("arbitrary",)),
        name="moe_experts",
    )(blk_e, nblk, xs, w1, w3, w2)


def _combine_copies(d0_ref, d1_ref, y_hbm, ybuf, sem, tm, tile, slot):
    base = tile * tm

    def issue(g, u, priority):
        r = base + g * V7X_SUBLANES + u
        pltpu.make_async_copy(y_hbm.at[pl.ds(d0_ref[r], 1)], ybuf.at[slot, 0, g, pl.ds(u, 1)],
                              sem.at[slot]).start(priority)
        pltpu.make_async_copy(y_hbm.at[pl.ds(d1_ref[r], 1)], ybuf.at[slot, 1, g, pl.ds(u, 1)],
                              sem.at[slot]).start(1 - priority)

    _issue_rows(tm, issue)


def _ple_kernel(d0_ref, d1_ref, h1_ref, y_hbm, slab_ref, pp_ref, ps_ref, gple_ref, wg_ref, wp_ref,
                gn_ref, *rest, npt, last, tm):
    ybuf, sem = rest[-2:]
    outs = rest[:-2]
    i = pl.program_id(0)
    nt = pl.num_programs(0)
    slot = i % 2
    fetch = functools.partial(_combine_copies, d0_ref, d1_ref, y_hbm, ybuf, sem, tm)

    @pl.when(i == 0)
    def _():
        fetch(0, 0)

    @pl.when(i + 1 < nt)
    def _():
        fetch(i + 1, 1 - slot)

    for k in range(TOP_K):
        pltpu.make_async_copy(ybuf.at[slot, k], ybuf.at[slot, k], sem.at[slot]).wait()

    p = jnp.where(i < npt, pp_ref[...], ps_ref[...]).astype(BF16)
    proj = _dot(p, wp_ref[...])
    slab = slab_ref[...]
    w = ybuf.shape[-1]
    y0 = jnp.concatenate(_unpack_bf16_pairs(ybuf[slot, 0].reshape(tm, w)), axis=1)
    y1 = jnp.concatenate(_unpack_bf16_pairs(ybuf[slot, 1].reshape(tm, w)), axis=1)
    h2 = h1_ref[...] + slab[:, 2:3] * y0 + slab[:, 3:4] * y1
    gate = _sigmoid(_dot(_rms(h2, gple_ref[...]).astype(BF16), wg_ref[...]))
    h3 = h2 + gate * proj
    hn = _rms(h3, gn_ref[...])
    if last:
        yp_ref, ys_ref = outs

        @pl.when(i < npt)
        def _():
            yp_ref[...] = hn

        @pl.when(i >= npt)
        def _():
            ys_ref[...] = hn
    else:
        h_ref, hn_ref = outs
        h_ref[...] = h3
        hn_ref[...] = hn.astype(BF16)


def _ple(d0, d1, h1, y, slab, p_p, p_s, layer, gple, wg, wp, gn, tm, last):
    n, d = h1.shape
    np_ = p_p.shape[1]
    npt = np_ // tm
    pd = p_p.shape[2]
    const = lambda i, a, b: (0, 0)
    rowmap = lambda i, a, b: (i, 0)
    if last:
        out_specs = [pl.BlockSpec((tm, d), lambda i, a, b: (jnp.minimum(i, npt - 1), 0)),
                     pl.BlockSpec((tm, d), lambda i, a, b: (jnp.maximum(i - npt, 0), 0))]
        out_shape = [jax.ShapeDtypeStruct((np_, d), F32), jax.ShapeDtypeStruct((n - np_, d), F32)]
    else:
        out_specs = [pl.BlockSpec((tm, d), rowmap), pl.BlockSpec((tm, d), rowmap)]
        out_shape = [jax.ShapeDtypeStruct((n, d), F32), jax.ShapeDtypeStruct((n, d), BF16)]
    grid_spec = pltpu.PrefetchScalarGridSpec(
        num_scalar_prefetch=2,
        grid=(n // tm,),
        in_specs=[
            pl.BlockSpec((tm, d), rowmap),
            pl.BlockSpec(memory_space=pl.ANY),
            pl.BlockSpec((tm, V7X_LANES), rowmap),
            pl.BlockSpec((None, tm, pd), lambda i, a, b: (layer, jnp.minimum(i, npt - 1), 0)),
            pl.BlockSpec((None, tm, pd), lambda i, a, b: (layer, jnp.maximum(i - npt, 0), 0)),
            pl.BlockSpec((1, d), const), pl.BlockSpec((d, d), const), pl.BlockSpec((pd, d), const),
            pl.BlockSpec((1, d), const),
        ],
        out_specs=out_specs,
        scratch_shapes=[pltpu.VMEM((2, TOP_K, tm // V7X_SUBLANES, V7X_SUBLANES, y.shape[1]), y.dtype),
                        pltpu.SemaphoreType.DMA((2,))],
    )
    return pl.pallas_call(
        functools.partial(_ple_kernel, npt=npt, last=last, tm=tm),
        grid_spec=grid_spec,
        out_shape=out_shape,
        compiler_params=_cparams(("arbitrary",)),
        name="ple",
    )(d0, d1, h1, y, slab, p_p, p_s, gple, wg, wp, gn)


def _rope_tables(pos, rope):
    half = rope // 2
    inv = 1.0 / (ROPE_THETA ** (jnp.arange(half, dtype=F32) * (2.0 / rope)))
    ang = pos.astype(F32)[:, None] * inv[None, :]
    pad = ((0, 0), (0, V7X_LANES - rope))
    cos = jnp.pad(jnp.tile(jnp.cos(ang), (1, 2)), pad)
    sin = jnp.pad(jnp.tile(jnp.sin(ang), (1, 2)), pad)
    return cos, sin


def _rot_last(w):
    half = w.shape[-1] // 2
    return jnp.concatenate([-w[..., half:], w[..., :half]], axis=-1)


def _pad_last(w, width):
    return jnp.pad(w, [(0, 0)] * (w.ndim - 1) + [(0, width - w.shape[-1])])


def kernel(x_prompt, x_sample, state_pool, state_conv, cache_latent, cache_krope, page_table, p_prompt, p_sample, norm_mix, norm_ffn, norm_ple, norm_final, pool_w_in, pool_w_grp, pool_scale, conv_w_pw1, conv_b_pw1, conv_w_dw, conv_b_dw, conv_ln_g, conv_ln_b, conv_w_pw2, conv_b_pw2, mla_w_dqkv, mla_q_norm, mla_kv_norm, mla_w_uq, mla_w_uk, mla_w_uv, mla_w_o, moe_w_group, moe_b_group, moe_w_expert, moe_b_expert, moe_w1, moe_w3, moe_w2, ple_w_gate, ple_w_proj):
    b, t, d = x_prompt.shape
    bs, ts, _ = x_sample.shape
    depth = norm_mix.shape[0]
    np_, ns = b * t, bs * ts
    n = np_ + ns
    tm = min(TOKEN_TILE, ns)
    assert np_ % tm == 0 and ns % tm == 0
    past_len = page_table.shape[1] * cache_latent.shape[2]
    n_groups = moe_w_group.shape[-1]
    n_experts = moe_w_expert.shape[-1]
    epg = n_experts // n_groups
    assert n_groups + n_experts <= V7X_LANES
    n_rows = (n * TOP_K + n_experts * (MOE_TILE - 1) + MOE_TILE - 1) // MOE_TILE * MOE_TILE
    pd = p_prompt.shape[-1]
    row = lambda v: v.reshape(1, -1)

    nh, kv_lora, nope = mla_w_uk.shape[1:]
    rope = cache_krope.shape[-1]
    q_lora = mla_w_dqkv.shape[-1] - kv_lora - rope
    scale = float((nope + rope) ** -0.5)
    mla_dims = (q_lora, kv_lora, rope, nh, nope, scale)
    cache_kr_t = jnp.swapaxes(cache_krope, 2, 3)

    h, hn = _prep(x_prompt.reshape(np_, d), x_sample.reshape(ns, d), row(norm_mix[0]), tm)
    p_p = p_prompt.reshape(depth, np_, pd)
    p_s = p_sample.reshape(depth, ns, pd)

    xs_base = jnp.zeros((n_rows, d // 2), jnp.uint32)
    new_pool_p, new_pool_s, new_conv_p, new_conv_s = [], [], [], []
    new_lat_p, new_kr_p, new_lat_s, new_kr_s = [], [], [], []
    y_p = y_s = None
    for i in range(depth):
        kind, j = i % N_MIXERS, i // N_MIXERS
        if kind == 0:
            w_in = pool_w_in[j].astype(BF16)
            w_grp = pool_w_grp[j].astype(BF16)
            sc = row(pool_scale[j])
            mo_p, st_p = _pool_prompt(hn, w_in, w_grp, sc, b, t, min(tm, t))
            mo_s, st_s = _pool_sample(hn, np_, state_pool[j], w_in, w_grp, sc, bs, ts, past_len)
            new_pool_p.append(st_p)
            new_pool_s.append(st_s)
        elif kind == 1:
            w1 = conv_w_pw1[j].astype(BF16)
            w2 = conv_w_pw2[j].astype(BF16)
            args = (w1, row(conv_b_pw1[j]), conv_w_dw[j], row(conv_b_dw[j]), row(conv_ln_g[j]),
                    row(conv_ln_b[j]), w2, row(conv_b_pw2[j]))
            mo_p, st_p = _conv_prompt(hn, *args, b, t, min(CONV_TILE, t))
            mo_s, st_s = _conv_sample(hn, np_, state_conv[j], *args, bs, ts)
            new_conv_p.append(st_p)
            new_conv_s.append(st_s)
        else:
            wd = mla_w_dqkv[j]
            o = q_lora + kv_lora
            wd_k = wd[:, o:]
            wd = jnp.concatenate([wd[:, :o], _pad_last(wd_k, V7X_LANES),
                                  _pad_last(_rot_last(wd_k), V7X_LANES)], axis=1).astype(BF16)
            wuq = mla_w_uq[j].reshape(q_lora, nh, nope + rope)
            wq_rope = wuq[:, :, nope:]
            wuq = jnp.concatenate([
                wuq[:, :, :nope].reshape(q_lora, nh * nope),
                _pad_last(wq_rope, V7X_LANES).reshape(q_lora, nh * V7X_LANES),
                _pad_last(_rot_last(wq_rope), V7X_LANES).reshape(q_lora, nh * V7X_LANES)],
                axis=1).astype(BF16)
            wuk = jnp.swapaxes(mla_w_uk[j], 1, 2).astype(BF16)
            wuv = mla_w_uv[j].astype(BF16)
            wo = mla_w_o[j].astype(BF16)
            qn, kvn = row(mla_q_norm[j]), row(mla_kv_norm[j])
            tmp = min(tm, t)
            ntj = t // tmp
            cos_p, sin_p = _rope_tables(jnp.arange(t, dtype=jnp.int32), rope)
            ckv_p, kpe_p, kcat_p, qcat_p = _mla_qkv(
                hn, 0, np_, wd, qn, kvn, wuq, wuk, cos_p, sin_p, lambda ii: (ii % ntj, 0), tmp,
                mla_dims, BF16)
            cos_s, sin_s = _rope_tables(past_len + jnp.arange(ts, dtype=jnp.int32), rope)
            cos_s, sin_s = jnp.tile(cos_s, (bs, 1)), jnp.tile(sin_s, (bs, 1))
            ckv_s, kpe_s, _, qcat_s = _mla_qkv(
                hn, np_ // tm, ns, wd, qn, kvn, wuq, wuk, cos_s, sin_s, lambda ii: (ii, 0), tm,
                mla_dims, F32)
            o_p = _attn_prompt(qcat_p, kcat_p, b, t, nh, kv_lora)
            o_s = _attn_sample(page_table, qcat_s, ckv_s, kpe_s, cache_latent, cache_kr_t, j, bs, ts, nh)
            mo_p = _mla_out(o_p, wuv, wo, tm)
            mo_s = _mla_out(o_s, wuv, wo, tm)
            new_lat_p.append(ckv_p.reshape(b, t, kv_lora))
            new_kr_p.append(kpe_p.reshape(b, t, rope))
            new_lat_s.append(ckv_s.reshape(bs, ts, kv_lora))
            new_kr_s.append(kpe_s.reshape(bs, ts, rope))

        wr = _pad_last(jnp.concatenate([moe_w_group[i], moe_w_expert[i]], axis=1), V7X_LANES)
        br = _pad_last(jnp.concatenate([moe_b_group[i], moe_b_expert[i]])[None, :], V7X_LANES)
        whi = wr.astype(BF16)
        wlo = (wr - whi.astype(F32)).astype(BF16)
        h1, xn, slab, cnt = _route(h, mo_p, mo_s, row(norm_ffn[i]), whi, wlo, br, tm, n_groups, epg)
        d0, d1, blk_e, nblk = _dispatch_tables(slab, cnt, n_experts, n_groups, MOE_TILE, n_rows // MOE_TILE)
        xs = _dispatch(d0, d1, xn, xs_base, tm)
        y = _experts(blk_e, nblk, xs, moe_w1, moe_w3, moe_w2, i, MOE_TILE)
        xs_base = y
        last = i == depth - 1
        gn = row(norm_final) if last else row(norm_mix[i + 1])
        out = _ple(d0, d1, h1, y, slab, p_p, p_s, i, row(norm_ple[i]), ple_w_gate[i].astype(BF16),
                   ple_w_proj[i].astype(BF16), gn, tm, last)
        if last:
            y_p, y_s = out
        else:
            h, hn = out

    return (y_p.reshape(b, t, d), y_s.reshape(bs, ts, d),
            jnp.stack(new_pool_p), jnp.stack(new_pool_s),
            jnp.stack(new_conv_p), jnp.stack(new_conv_s),
            jnp.stack(new_lat_p), jnp.stack(new_kr_p),
            jnp.stack(new_lat_s), jnp.stack(new_kr_s))
```
